```python
import jax
import jax.numpy as jnp
from jax import lax
import numpy as np

D_MODEL = 1024
BATCH = 8
SEQ = 2048
DEPTH = 4
DEC_BATCH = 32
DEC_SEQ = 4
PAST_LEN = 16384
PAGE_SIZE = 128

N_A_LAYERS = DEPTH // 2
N_B_LAYERS = DEPTH - N_A_LAYERS
POOL_WINDOWS = (2, 4, 8, 16)
N_POOL_GROUPS = len(POOL_WINDOWS)
POOL_WIDTH = D_MODEL
POOL_GROUP = POOL_WIDTH // N_POOL_GROUPS
POOL_BUF = max(POOL_WINDOWS) - 1
MLA_HEADS = 8
QK_NOPE = 128
QK_ROPE = 64
V_HEAD = 128
KV_RANK = 256
Q_RANK = 384
MLA_WIDTH = MLA_HEADS * V_HEAD
MLA_SCALE = (QK_NOPE + QK_ROPE) ** -0.5
ROPE_THETA = 10000.0
Q_BLOCK = 128
MEM_TOKENS = 256
MEM_HEADS = 4
MEM_HEAD_DIM = 128
MEM_WIDTH = MEM_HEADS * MEM_HEAD_DIM
MEM_SCALE = MEM_HEAD_DIM ** -0.5
IN_A = 2 * POOL_WIDTH + 2 * MEM_WIDTH
IN_B = Q_RANK + MLA_WIDTH + 2 * MEM_WIDTH
OUT_W = POOL_WIDTH + MEM_WIDTH
EPS = 1e-6

kernel_name = 'yoco_pool_mla_memory_decoder'


def rmsnorm(x, g):
    xf = x.astype(jnp.float32)
    y = xf * lax.rsqrt(jnp.mean(xf * xf, axis=-1, keepdims=True) + EPS)
    return (y * g.astype(jnp.float32)).astype(x.dtype)


def rope(x, pos):
    half = x.shape[-1] // 2
    inv = ROPE_THETA ** (-jnp.arange(half, dtype=jnp.float32) / half)
    ang = pos.astype(jnp.float32)[:, None] * inv[None, :]
    cos = jnp.cos(ang)[None, :, None, :]
    sin = jnp.sin(ang)[None, :, None, :]
    xf = x.astype(jnp.float32)
    x1, x2 = xf[..., :half], xf[..., half:]
    return jnp.concatenate([x1 * cos - x2 * sin, x1 * sin + x2 * cos], axis=-1).astype(x.dtype)


def pool_mix(u_ext, pos, w_grp, scale):
    B = u_ext.shape[0]
    T = pos.shape[0]
    uf = u_ext.astype(jnp.float32)
    cs = jnp.concatenate([jnp.zeros((B, 1, POOL_WIDTH), jnp.float32), jnp.cumsum(uf, axis=1)], axis=1)
    end = cs[:, POOL_BUF + 1:]
    cur = uf[:, POOL_BUF:]
    groups = []
    for g, w in enumerate(POOL_WINDOWS):
        c0, c1 = g * POOL_GROUP, (g + 1) * POOL_GROUP
        start = cs[:, POOL_BUF + 1 - w: POOL_BUF + 1 - w + T, c0:c1]
        cnt = jnp.minimum(pos + 1, w).astype(jnp.float32)[None, :, None]
        groups.append((end[..., c0:c1] - start) / cnt - cur[..., c0:c1])
    pooled = jnp.stack(groups, axis=2)
    mixed = jnp.einsum('btgc,gcd->btgd', pooled, w_grp.astype(jnp.float32))
    return (mixed.reshape(B, T, POOL_WIDTH) * scale.astype(jnp.float32)).astype(u_ext.dtype)


def mem_project(mem, g, w):
    B, M, _ = mem.shape
    return (rmsnorm(mem, g) @ w).reshape(B, M, MEM_HEADS, MEM_HEAD_DIM)


def mem_attend(q, mk, mv):
    B, T = q.shape[0], q.shape[1]
    s = jnp.einsum('bthd,bmhd->bhtm', q, mk).astype(jnp.float32) * MEM_SCALE
    p = jax.nn.softmax(s, axis=-1).astype(mv.dtype)
    return jnp.einsum('bhtm,bmhd->bthd', p, mv).reshape(B, T, MEM_WIDTH)


def shared_latent_kv(x, pos, g_kv_in, w_kv_down, g_kv_latent):
    kv = rmsnorm(x, g_kv_in) @ w_kv_down
    ckv = rmsnorm(kv[..., :KV_RANK], g_kv_latent)
    krope = rope(kv[..., KV_RANK:][:, :, None, :], pos)[:, :, 0, :]
    return ckv, krope


def mla_attend(q_lat, q_rope, ckv, krope, qpos, kpos):
    s = (jnp.einsum('bthc,bsc->bhts', q_lat, ckv).astype(jnp.float32)
         + jnp.einsum('bthr,bsr->bhts', q_rope, krope).astype(jnp.float32)) * MLA_SCALE
    mask = kpos[None, :] <= qpos[:, None]
    s = jnp.where(mask[None, None], s, -1e30)
    p = jax.nn.softmax(s, axis=-1).astype(ckv.dtype)
    return jnp.einsum('bhts,bsc->bthc', p, ckv)


def mla_mix(c_q, pos, ckv, krope, g_q, w_q_up, w_k_up, w_v_up):
    B, T, _ = c_q.shape
    q = (rmsnorm(c_q, g_q) @ w_q_up).reshape(B, T, MLA_HEADS, QK_NOPE + QK_ROPE)
    q_rope = rope(q[..., QK_NOPE:], pos)
    q_lat = jnp.einsum('bthd,chd->bthc', q[..., :QK_NOPE], w_k_up)
    kpos = jnp.arange(ckv.shape[1], dtype=jnp.int32)
    if T >= Q_BLOCK and T % Q_BLOCK == 0:
        nb = T // Q_BLOCK
        ql = q_lat.reshape(B, nb, Q_BLOCK, MLA_HEADS, KV_RANK).swapaxes(0, 1)
        qr = q_rope.reshape(B, nb, Q_BLOCK, MLA_HEADS, QK_ROPE).swapaxes(0, 1)
        qp = pos.reshape(nb, Q_BLOCK)
        o = lax.map(lambda a: mla_attend(a[0], a[1], ckv, krope, a[2], kpos), (ql, qr, qp))
        o_lat = o.swapaxes(0, 1).reshape(B, T, MLA_HEADS, KV_RANK)
    else:
        o_lat = mla_attend(q_lat, q_rope, ckv, krope, pos, kpos)
    return jnp.einsum('bthc,chd->bthd', o_lat, w_v_up).reshape(B, T, MLA_WIDTH)


def trunk(x, pos, pool_prev, ckv_past, krope_past, mem_k, mem_v,
          g_norm, w_in_a, w_pool_grp, pool_scale, w_in_b, g_q_latent, w_q_up,
          g_kv_in, w_kv_down, g_kv_latent, w_k_up, w_v_up, w_out, g_final):
    B, T, _ = x.shape
    pool_new = []
    ckv_new = krope_new = ckv_all = krope_all = None
    for l in range(DEPTH):
        if l == N_A_LAYERS:
            ckv_new, krope_new = shared_latent_kv(x, pos, g_kv_in, w_kv_down, g_kv_latent)
            if ckv_past is None:
                ckv_all, krope_all = ckv_new, krope_new
            else:
                ckv_all = jnp.concatenate([ckv_past, ckv_new], axis=1)
                krope_all = jnp.concatenate([krope_past, krope_new], axis=1)
        h = rmsnorm(x, g_norm[l])
        if l < N_A_LAYERS:
            z = h @ w_in_a[l]
            u, gate_t, q_m, gate_m = jnp.split(z, [POOL_WIDTH, 2 * POOL_WIDTH, 2 * POOL_WIDTH + MEM_WIDTH], axis=-1)
            prev = jnp.zeros((B, POOL_BUF, POOL_WIDTH), u.dtype) if pool_prev is None else pool_prev[l]
            u_ext = jnp.concatenate([prev, u], axis=1)
            pool_new.append(u_ext[:, -POOL_BUF:])
            tok = pool_mix(u_ext, pos, w_pool_grp[l], pool_scale[l])
        else:
            j = l - N_A_LAYERS
            z = h @ w_in_b[j]
            c_q, gate_t, q_m, gate_m = jnp.split(z, [Q_RANK, Q_RANK + MLA_WIDTH, Q_RANK + MLA_WIDTH + MEM_WIDTH], axis=-1)
            tok = mla_mix(c_q, pos, ckv_all, krope_all, g_q_latent[j], w_q_up[j], w_k_up, w_v_up)
        mem_o = mem_attend(q_m.reshape(B, T, MEM_HEADS, MEM_HEAD_DIM), mem_k[l], mem_v[l])
        mixed = jnp.concatenate([tok * jax.nn.silu(gate_t), mem_o * jax.nn.silu(gate_m)], axis=-1)
        x = x + mixed @ w_out[l]
    return rmsnorm(x, g_final), jnp.stack(pool_new, axis=0), ckv_new, krope_new


def setup_inputs(seed: int = 0) -> dict:
    key = jax.random.key(seed)
    ks = jax.random.split(key, 32)
    f32 = jnp.float32

    def nrm(k, shape, scale=1.0):
        return jax.random.normal(k, shape, f32) * scale

    def gain(k, shape):
        return 1.0 + 0.02 * jax.random.normal(k, shape, f32)

    n_pages = PAST_LEN // PAGE_SIZE
    n_used = DEC_BATCH * n_pages
    n_pool = n_used + max(1, n_used // 4)
    page_table = jax.random.permutation(ks[7], n_pool)[:n_used].reshape(DEC_BATCH, n_pages).astype(jnp.int32)
    return {
        'x_prompt': nrm(ks[0], (BATCH, SEQ, D_MODEL)),
        'x_sample': nrm(ks[1], (DEC_BATCH, DEC_SEQ, D_MODEL)),
        'state_pool': nrm(ks[2], (N_A_LAYERS, DEC_BATCH, POOL_BUF, POOL_WIDTH)),
        'cache_ckv': nrm(ks[3], (n_pool, PAGE_SIZE, KV_RANK)),
        'cache_krope': nrm(ks[4], (n_pool, PAGE_SIZE, QK_ROPE)),
        'cache_mem_k': nrm(ks[5], (DEPTH, DEC_BATCH, MEM_TOKENS, MEM_HEADS, MEM_HEAD_DIM)),
        'cache_mem_v': nrm(ks[6], (DEPTH, DEC_BATCH, MEM_TOKENS, MEM_HEADS, MEM_HEAD_DIM)),
        'page_table': page_table,
        'mem_prompt': nrm(ks[8], (BATCH, MEM_TOKENS, D_MODEL)),
        'g_norm': gain(ks[9], (DEPTH, D_MODEL)),
        'w_in_a': nrm(ks[10], (N_A_LAYERS, D_MODEL, IN_A), D_MODEL ** -0.5),
        'w_pool_grp': nrm(ks[11], (N_A_LAYERS, N_POOL_GROUPS, POOL_GROUP, POOL_GROUP), POOL_GROUP ** -0.5),
        'pool_scale': gain(ks[12], (N_A_LAYERS, POOL_WIDTH)),
        'w_in_b': nrm(ks[13], (N_B_LAYERS, D_MODEL, IN_B), D_MODEL ** -0.5),
        'g_q_latent': gain(ks[14], (N_B_LAYERS, Q_RANK)),
        'w_q_up': nrm(ks[15], (N_B_LAYERS, Q_RANK, MLA_HEADS * (QK_NOPE + QK_ROPE)), Q_RANK ** -0.5),
        'g_kv_in': gain(ks[16], (D_MODEL,)),
        'w_kv_down': nrm(ks[17], (D_MODEL, KV_RANK + QK_ROPE), D_MODEL ** -0.5),
        'g_kv_latent': gain(ks[18], (KV_RANK,)),
        'w_k_up': nrm(ks[19], (KV_RANK, MLA_HEADS, QK_NOPE), KV_RANK ** -0.5),
        'w_v_up': nrm(ks[20], (KV_RANK, MLA_HEADS, V_HEAD), KV_RANK ** -0.5),
        'g_mem': gain(ks[21], (DEPTH, D_MODEL)),
        'w_mem_k': nrm(ks[22], (DEPTH, D_MODEL, MEM_WIDTH), D_MODEL ** -0.5),
        'w_mem_v': nrm(ks[23], (DEPTH, D_MODEL, MEM_WIDTH), D_MODEL ** -0.5),
        'w_out': nrm(ks[24], (DEPTH, OUT_W, D_MODEL), OUT_W ** -0.5),
        'g_final': gain(ks[25], (D_MODEL,)),
    }


def reference(x_prompt, x_sample, state_pool, cache_ckv, cache_krope, cache_mem_k, cache_mem_v,
              page_table, mem_prompt, g_norm, w_in_a, w_pool_grp, pool_scale, w_in_b, g_q_latent,
              w_q_up, g_kv_in, w_kv_down, g_kv_latent, w_k_up, w_v_up, g_mem, w_mem_k, w_mem_v,
              w_out, g_final):
    weights = (g_norm, w_in_a, w_pool_grp, pool_scale, w_in_b, g_q_latent, w_q_up,
               g_kv_in, w_kv_down, g_kv_latent, w_k_up, w_v_up, w_out, g_final)
    pos_p = jnp.arange(x_prompt.shape[1], dtype=jnp.int32)
    mem_k_p = jnp.stack([mem_project(mem_prompt, g_mem[l], w_mem_k[l]) for l in range(DEPTH)], axis=0)
    mem_v_p = jnp.stack([mem_project(mem_prompt, g_mem[l], w_mem_v[l]) for l in range(DEPTH)], axis=0)
    y_p, pool_p, ckv_p, krope_p = trunk(x_prompt, pos_p, None, None, None, mem_k_p, mem_v_p, *weights)
    db, n_pages = page_table.shape
    past = n_pages * cache_ckv.shape[1]
    ckv_past = cache_ckv[page_table].reshape(db, past, KV_RANK)
    krope_past = cache_krope[page_table].reshape(db, past, QK_ROPE)
    pos_s = past + jnp.arange(x_sample.shape[1], dtype=jnp.int32)
    y_s, pool_s, ckv_s, krope_s = trunk(x_sample, pos_s, state_pool, ckv_past, krope_past,
                                        cache_mem_k, cache_mem_v, *weights)
    return (y_p, y_s, pool_p, pool_s, ckv_p, krope_p, ckv_s, krope_s, mem_k_p, mem_v_p)
```

```python
import functools

import jax
import jax.numpy as jnp
from jax import lax
from jax.experimental import pallas as pl
from jax.experimental.pallas import tpu as pltpu

F32 = jnp.float32
BF16 = jnp.bfloat16

D_MODEL = 1024
DEPTH = 4
N_A = 2
POOL_WINDOWS = (2, 4, 8, 16)
POOL_GROUP = 256
POOL_BUF = 15
MLA_HEADS = 8
QK_NOPE = 128
QK_ROPE = 64
V_HEAD = 128
KV_RANK = 256
Q_RANK = 384
MLA_WIDTH = MLA_HEADS * V_HEAD
MLA_SCALE = (QK_NOPE + QK_ROPE) ** -0.5
ROPE_THETA = 10000.0
MEM_TOKENS = 256
MEM_HEADS = 4
MEM_HEAD_DIM = 128
MEM_WIDTH = MEM_HEADS * MEM_HEAD_DIM
MEM_SCALE = MEM_HEAD_DIM ** -0.5
EPS = 1e-6
NEG = -1e30

KCAT = 384
SUBLANES = 8
LANES = 128
HALO = 16
E_OFF = SUBLANES + HALO
VMEM_LIMIT = 56 * 1024 * 1024


def _cparams(n_axes):
    return pltpu.CompilerParams(dimension_semantics=("arbitrary",) * n_axes,
                                vmem_limit_bytes=VMEM_LIMIT)


def _rms(x, g):
    return x * lax.rsqrt(jnp.mean(x * x, axis=-1, keepdims=True) + EPS) * g


def _dot(a, b):
    return jnp.dot(a, b, preferred_element_type=F32)


def _dot_nt(a, b):
    return lax.dot_general(a, b, (((1,), (1,)), ((), ())), preferred_element_type=F32)


def _silu(x):
    return x * jax.nn.sigmoid(x)


def _rope_lanes(a, cos, sin):
    lane = lax.broadcasted_iota(jnp.int32, a.shape, 1)
    up = pltpu.roll(a, 32, 1)
    dn = pltpu.roll(a, 96, 1)
    rot = jnp.where((lane & 63) < 32, -dn, up)
    return a * cos + rot * sin


def _mem_attend(qm, mk_ref, mv_ref):
    outs = []
    for h in range(MEM_HEADS):
        c0, c1 = h * MEM_HEAD_DIM, (h + 1) * MEM_HEAD_DIM
        q = (qm[:, c0:c1] * MEM_SCALE).astype(BF16)
        k = mk_ref[0, 0, :, c0:c1].astype(BF16)
        v = mv_ref[0, 0, :, c0:c1].astype(BF16)
        s = _dot_nt(q, k)
        m = jnp.max(s, axis=-1, keepdims=True)
        p = jnp.exp(s - m)
        l = jnp.sum(p, axis=-1, keepdims=True)
        outs.append(_dot(p.astype(BF16), v) * (1.0 / l))
    return outs


def _mem_proj_kernel(mem_ref, g_ref, wk_ref, wv_ref, mk_ref, mv_ref):
    x = mem_ref[0]
    xn = x * lax.rsqrt(jnp.mean(x * x, axis=-1, keepdims=True) + EPS)
    for l in range(DEPTH):
        h = (xn * g_ref[l:l + 1, :]).astype(BF16)
        mk_ref[l, 0] = _dot(h, wk_ref[l])
        mv_ref[l, 0] = _dot(h, wv_ref[l])


def _mem_project(mem, g_mem, wk, wv):
    B, M, _ = mem.shape
    out = jax.ShapeDtypeStruct((DEPTH, B, M, MEM_WIDTH), F32)
    return pl.pallas_call(
        _mem_proj_kernel,
        grid=(B,),
        in_specs=[
            pl.BlockSpec((1, M, D_MODEL), lambda b: (b, 0, 0)),
            pl.BlockSpec((DEPTH, D_MODEL), lambda b: (0, 0)),
            pl.BlockSpec((DEPTH, D_MODEL, MEM_WIDTH), lambda b: (0, 0, 0)),
            pl.BlockSpec((DEPTH, D_MODEL, MEM_WIDTH), lambda b: (0, 0, 0)),
        ],
        out_specs=[
            pl.BlockSpec((DEPTH, 1, M, MEM_WIDTH), lambda b: (0, b, 0, 0)),
            pl.BlockSpec((DEPTH, 1, M, MEM_WIDTH), lambda b: (0, b, 0, 0)),
        ],
        out_shape=[out, out],
        compiler_params=_cparams(1),
        name="mem_project",
    )(mem, g_mem, wk, wv)


def _a_layer_kernel(x_ref, prev_ref, g_ref, win_ref, wgrp_ref, pscale_ref, mk_ref, mv_ref, wout_ref,
                    xo_ref, pool_ref, e_sc, sa_sc, sb_sc, *, tq, n_t, n_valid, pos0):
    t = pl.program_id(1)
    rows = HALO + tq

    @pl.when(t == 0)
    def _():
        zeros = jnp.zeros((SUBLANES, D_MODEL), F32)
        e_sc[0:SUBLANES, :] = zeros
        sa_sc[0:SUBLANES, :] = zeros
        sb_sc[0:SUBLANES, :] = zeros
        e_sc[SUBLANES:E_OFF, :] = prev_ref[0]

    x = x_ref[0]
    h = _rms(x, g_ref[...]).astype(BF16)
    e_sc[E_OFF:E_OFF + tq, :] = _dot(h, win_ref[:, 0:1024])

    lo = SUBLANES
    sa_sc[lo:lo + rows, :] = e_sc[lo:lo + rows, :] + e_sc[lo - 1:lo - 1 + rows, :]
    sb_sc[lo:lo + rows, 256:1024] = sa_sc[lo:lo + rows, 256:1024] + sa_sc[lo - 2:lo - 2 + rows, 256:1024]
    sa_sc[lo:lo + rows, 512:1024] = sb_sc[lo:lo + rows, 512:1024] + sb_sc[lo - 4:lo - 4 + rows, 512:1024]
    s16 = sa_sc[E_OFF:E_OFF + tq, 768:1024] + sa_sc[E_OFF - 8:E_OFF - 8 + tq, 768:1024]
    sums = (sa_sc[E_OFF:E_OFF + tq, 0:256], sb_sc[E_OFF:E_OFF + tq, 256:512],
            sa_sc[E_OFF:E_OFF + tq, 512:768], s16)

    pos = pos0 + t * tq + lax.broadcasted_iota(jnp.int32, (tq, 1), 0)
    mixed_tok = []
    for g, w in enumerate(POOL_WINDOWS):
        c0, c1 = g * POOL_GROUP, (g + 1) * POOL_GROUP
        inv_cnt = 1.0 / jnp.minimum(pos + 1, w).astype(F32)
        pooled = sums[g] * inv_cnt - e_sc[E_OFF:E_OFF + tq, c0:c1]
        mixed = _dot(pooled.astype(BF16), wgrp_ref[g]) * pscale_ref[:, c0:c1]
        gate = _dot(h, win_ref[:, 1024 + c0:1024 + c1])
        mixed_tok.append((mixed * _silu(gate)).astype(BF16))

    qm = _dot(h, win_ref[:, 2048:2560])
    gate_m = _dot(h, win_ref[:, 2560:3072])
    mem_o = _mem_attend(qm, mk_ref, mv_ref)

    acc = x
    for g in range(4):
        acc = acc + _dot(mixed_tok[g], wout_ref[g * 256:(g + 1) * 256, :])
    for hd in range(MEM_HEADS):
        c0, c1 = hd * MEM_HEAD_DIM, (hd + 1) * MEM_HEAD_DIM
        mm = (mem_o[hd] * _silu(gate_m[:, c0:c1])).astype(BF16)
        acc = acc + _dot(mm, wout_ref[1024 + c0:1024 + c1, :])
    xo_ref[0] = acc

    @pl.when(t == n_t - 1)
    def _():
        pool_ref[0] = e_sc[E_OFF + n_valid - HALO:E_OFF + n_valid, :]

    if n_t > 1:
        @pl.when(t < n_t - 1)
        def _():
            e_sc[SUBLANES:E_OFF, :] = e_sc[SUBLANES + tq:E_OFF + tq, :]


def _a_layer(x, prev, g, w_in, w_grp, pscale, mem_k, mem_v, w_out, *, layer, tq, n_valid, pos0):
    B, T, _ = x.shape
    n_t = T // tq
    rows = E_OFF + tq
    kern = functools.partial(_a_layer_kernel, tq=tq, n_t=n_t, n_valid=n_valid, pos0=pos0)
    return pl.pallas_call(
        kern,
        grid=(B, n_t),
        in_specs=[
            pl.BlockSpec((1, tq, D_MODEL), lambda b, t: (b, t, 0)),
            pl.BlockSpec((1, HALO, D_MODEL), lambda b, t: (b, 0, 0)),
            pl.BlockSpec((1, D_MODEL), lambda b, t: (0, 0)),
            pl.BlockSpec((D_MODEL, 3072), lambda b, t: (0, 0)),
            pl.BlockSpec((4, POOL_GROUP, POOL_GROUP), lambda b, t: (0, 0, 0)),
            pl.BlockSpec((1, D_MODEL), lambda b, t: (0, 0)),
            pl.BlockSpec((1, 1, MEM_TOKENS, MEM_WIDTH), lambda b, t: (layer, b, 0, 0)),
            pl.BlockSpec((1, 1, MEM_TOKENS, MEM_WIDTH), lambda b, t: (layer, b, 0, 0)),
            pl.BlockSpec((1536, D_MODEL), lambda b, t: (0, 0)),
        ],
        out_specs=[
            pl.BlockSpec((1, tq, D_MODEL), lambda b, t: (b, t, 0)),
            pl.BlockSpec((1, HALO, D_MODEL), lambda b, t: (b, 0, 0)),
        ],
        out_shape=[jax.ShapeDtypeStruct((B, T, D_MODEL), F32),
                   jax.ShapeDtypeStruct((B, HALO, D_MODEL), F32)],
        scratch_shapes=[pltpu.VMEM((rows, D_MODEL), F32)] * 3,
        compiler_params=_cparams(2),
        name=f"a_layer{layer}_{B}",
    )(x, prev, g, w_in, w_grp, pscale, mem_k, mem_v, w_out)


def _kv_latent_kernel(x_ref, gin_ref, wkv_ref, glat_ref, cos_ref, sin_ref, ckv_ref, kr_ref, kcat_ref):
    h = _rms(x_ref[0], gin_ref[...]).astype(BF16)
    kv = _dot(h, wkv_ref[...])
    ckv = _rms(kv[:, 0:KV_RANK], glat_ref[...])
    kr = _rope_lanes(kv[:, KV_RANK:KCAT], cos_ref[...], sin_ref[...])
    ckv_ref[0] = ckv
    kr_ref[0] = kr[:, 0:QK_ROPE]
    kcat_ref[0, :, 0:KV_RANK] = ckv.astype(BF16)
    kcat_ref[0, :, KV_RANK:KCAT] = kr.astype(BF16)


def _kv_latent(x, g_in, w_kv, g_lat, cos, sin, *, tq):
    B, T, _ = x.shape
    return pl.pallas_call(
        _kv_latent_kernel,
        grid=(B, T // tq),
        in_specs=[
            pl.BlockSpec((1, tq, D_MODEL), lambda b, t: (b, t, 0)),
            pl.BlockSpec((1, D_MODEL), lambda b, t: (0, 0)),
            pl.BlockSpec((D_MODEL, KCAT), lambda b, t: (0, 0)),
            pl.BlockSpec((1, KV_RANK), lambda b, t: (0, 0)),
            pl.BlockSpec((tq, LANES), lambda b, t: (t, 0)),
            pl.BlockSpec((tq, LANES), lambda b, t: (t, 0)),
        ],
        out_specs=[
            pl.BlockSpec((1, tq, KV_RANK), lambda b, t: (b, t, 0)),
            pl.BlockSpec((1, tq, QK_ROPE), lambda b, t: (b, t, 0)),
            pl.BlockSpec((1, tq, KCAT), lambda b, t: (b, t, 0)),
        ],
        out_shape=[jax.ShapeDtypeStruct((B, T, KV_RANK), F32),
                   jax.ShapeDtypeStruct((B, T, QK_ROPE), F32),
                   jax.ShapeDtypeStruct((B, T, KCAT), BF16)],
        compiler_params=_cparams(2),
        name=f"kv_latent_{B}",
    )(x, g_in, w_kv, g_lat, cos, sin)


def _b_pre_kernel(x_ref, g_ref, wcq_ref, wrest_ref, gq_ref, wqn_ref, wqr_ref, wkup_ref, cos_ref, sin_ref,
                  q_ref, zrest_ref):
    h = _rms(x_ref[0], g_ref[...]).astype(BF16)
    zrest_ref[0] = _dot(h, wrest_ref[...])
    cn = _rms(_dot(h, wcq_ref[...]), gq_ref[...]).astype(BF16)
    qn = _dot(cn, wqn_ref[...])
    qr = _dot(cn, wqr_ref[...])
    for hd in range(MLA_HEADS):
        ql = _dot(qn[:, hd * QK_NOPE:(hd + 1) * QK_NOPE].astype(BF16), wkup_ref[hd])
        q_ref[0, hd, :, 0:KV_RANK] = (ql * MLA_SCALE).astype(BF16)
    cos = cos_ref[...]
    sin = sin_ref[...]
    lane = lax.broadcasted_iota(jnp.int32, cos.shape, 1)
    for j in range(MLA_HEADS // 2):
        rr = _rope_lanes(qr[:, j * LANES:(j + 1) * LANES], cos, sin) * MLA_SCALE
        q_ref[0, 2 * j, :, KV_RANK:KCAT] = jnp.where(lane < QK_ROPE, rr, 0.0).astype(BF16)
        q_ref[0, 2 * j + 1, :, KV_RANK:KCAT] = jnp.where(lane < QK_ROPE, pltpu.roll(rr, 64, 1), 0.0).astype(BF16)


def _b_pre(x, g, w_cq, w_rest, g_q, w_qn, w_qr, w_kup, cos, sin, *, tq, tag):
    B, T, _ = x.shape
    const2 = lambda b, t: (0, 0)
    return pl.pallas_call(
        _b_pre_kernel,
        grid=(B, T // tq),
        in_specs=[
            pl.BlockSpec((1, tq, D_MODEL), lambda b, t: (b, t, 0)),
            pl.BlockSpec((1, D_MODEL), const2),
            pl.BlockSpec((D_MODEL, Q_RANK), const2),
            pl.BlockSpec((D_MODEL, 2048), const2),
            pl.BlockSpec((1, Q_RANK), const2),
            pl.BlockSpec((Q_RANK, MLA_HEADS * QK_NOPE), const2),
            pl.BlockSpec((Q_RANK, MLA_HEADS * QK_ROPE), const2),
            pl.BlockSpec((MLA_HEADS, QK_NOPE, KV_RANK), lambda b, t: (0, 0, 0)),
            pl.BlockSpec((tq, LANES), lambda b, t: (t, 0)),
            pl.BlockSpec((tq, LANES), lambda b, t: (t, 0)),
        ],
        out_specs=[
            pl.BlockSpec((1, MLA_HEADS, tq, KCAT), lambda b, t: (b, 0, t, 0)),
            pl.BlockSpec((1, tq, 2048), lambda b, t: (b, t, 0)),
        ],
        out_shape=[jax.ShapeDtypeStruct((B, MLA_HEADS, T, KCAT), BF16),
                   jax.ShapeDtypeStruct((B, T, 2048), F32)],
        compiler_params=_cparams(2),
        name=f"b_pre_{tag}",
    )(x, g, w_cq, w_rest, g_q, w_qn, w_qr, w_kup, cos, sin)


def _flash_kernel(q_ref, k_ref, wv_ref, o_ref, m_sc, l_sc, acc_sc, *, tq, tk):
    i = pl.program_id(1)
    mrows = MLA_HEADS * tq
    q = q_ref[0].reshape(mrows, KCAT)
    m_sc[...] = jnp.full((mrows, 1), NEG, F32)
    l_sc[...] = jnp.zeros((mrows, 1), F32)
    acc_sc[...] = jnp.zeros((mrows, KV_RANK), F32)

    def block(j, masked):
        k = k_ref[0, pl.ds(pl.multiple_of(j * tk, tk), tk), :]
        s = _dot_nt(q, k)
        if masked:
            qpos = i * tq + (lax.broadcasted_iota(jnp.int32, (mrows, tk), 0) & (tq - 1))
            kpos = j * tk + lax.broadcasted_iota(jnp.int32, (mrows, tk), 1)
            s = jnp.where(kpos <= qpos, s, NEG)
        m_prev = m_sc[...]
        m_new = jnp.maximum(m_prev, jnp.max(s, axis=-1, keepdims=True))
        alpha = jnp.exp(m_prev - m_new)
        p = jnp.exp(s - m_new)
        l_sc[...] = alpha * l_sc[...] + jnp.sum(p, axis=-1, keepdims=True)
        acc_sc[...] = alpha * acc_sc[...] + _dot(p.astype(BF16), k[:, 0:KV_RANK])
        m_sc[...] = m_new

    n_full = (i * tq) // tk

    def body(j, carry):
        block(j, False)
        return carry

    lax.fori_loop(0, n_full, body, 0)
    block(n_full, True)

    o = acc_sc[...] * (1.0 / l_sc[...])
    for hd in range(MLA_HEADS):
        oh = o[hd * tq:(hd + 1) * tq, :].astype(BF16)
        o_ref[0, :, hd * V_HEAD:(hd + 1) * V_HEAD] = _dot(oh, wv_ref[hd])


def _flash(q, kcat, w_vup, *, tq, tk):
    B, H, T, _ = q.shape
    mrows = H * tq
    kern = functools.partial(_flash_kernel, tq=tq, tk=tk)
    return pl.pallas_call(
        kern,
        grid=(B, T // tq),
        in_specs=[
            pl.BlockSpec((1, H, tq, KCAT), lambda b, i: (b, 0, i, 0)),
            pl.BlockSpec((1, T, KCAT), lambda b, i: (b, 0, 0)),
            pl.BlockSpec((H, KV_RANK, V_HEAD), lambda b, i: (0, 0, 0)),
        ],
        out_specs=pl.BlockSpec((1, tq, MLA_WIDTH), lambda b, i: (b, i, 0)),
        out_shape=jax.ShapeDtypeStruct((B, T, MLA_WIDTH), F32),
        scratch_shapes=[pltpu.VMEM((mrows, 1), F32), pltpu.VMEM((mrows, 1), F32),
                        pltpu.VMEM((mrows, KV_RANK), F32)],
        compiler_params=_cparams(2),
        name="flash_prompt",
    )(q, kcat, w_vup)


def _paged_kernel(pt_ref, q_ref, knew_ref, ckv_hbm, kr_hbm, o_ref,
                  ckv_buf, kr_buf, sem, m_sc, l_sc, acc_sc, *, n_b, n_ch, cp, page, tp, n_new):
    total = n_b * n_ch
    ck = cp * page

    def copies(bc, slot):
        b = bc // n_ch
        c = bc % n_ch
        out = []
        for p in range(cp):
            pg = pt_ref[b, c * cp + p]
            out.append(pltpu.make_async_copy(ckv_hbm.at[pg], ckv_buf.at[slot, pl.ds(p * page, page)],
                                             sem.at[0, slot]))
            out.append(pltpu.make_async_copy(kr_hbm.at[pg], kr_buf.at[slot, pl.ds(p * page, page)],
                                             sem.at[1, slot]))
        return out

    def update(s, v_bf16):
        m_prev = m_sc[...]
        m_new = jnp.maximum(m_prev, jnp.max(s, axis=-1, keepdims=True))
        alpha = jnp.exp(m_prev - m_new)
        p = jnp.exp(s - m_new)
        l_sc[...] = alpha * l_sc[...] + jnp.sum(p, axis=-1, keepdims=True)
        acc_sc[...] = alpha * acc_sc[...] + _dot(p.astype(BF16), v_bf16)
        m_sc[...] = m_new

    for cpy in copies(0, 0):
        cpy.start()

    def body(bc, carry):
        slot = bc % 2
        b = bc // n_ch
        c = bc % n_ch

        @pl.when(bc + 1 < total)
        def _():
            for cpy in copies(bc + 1, 1 - slot):
                cpy.start()

        @pl.when(c == 0)
        def _():
            m_sc[...] = jnp.full(m_sc.shape, NEG, F32)
            l_sc[...] = jnp.zeros(l_sc.shape, F32)
            acc_sc[...] = jnp.zeros(acc_sc.shape, F32)

        for cpy in copies(bc, slot):
            cpy.wait()

        q = q_ref[b]
        kc = ckv_buf[slot].astype(BF16)
        kr = kr_buf[slot].astype(BF16)
        s = _dot_nt(q[:, 0:KV_RANK], kc) + _dot_nt(q[:, KV_RANK:KV_RANK + QK_ROPE], kr)
        update(s, kc)

        @pl.when(c == n_ch - 1)
        def _():
            kn = knew_ref[b]
            sn = _dot_nt(q, kn)
            tpos = lax.broadcasted_iota(jnp.int32, sn.shape, 0) & (tp - 1)
            col = lax.broadcasted_iota(jnp.int32, sn.shape, 1)
            sn = jnp.where((col <= tpos) & (col < n_new), sn, NEG)
            update(sn, kn[:, 0:KV_RANK])
            o_ref[b] = acc_sc[...] * (1.0 / l_sc[...])

        return carry

    lax.fori_loop(0, total, body, 0)


def _paged_attention(page_table, q, knew, cache_ckv, cache_krope, *, cp, n_new):
    n_b, n_pages = page_table.shape
    page = cache_ckv.shape[1]
    rows = q.shape[1]
    tp = rows // MLA_HEADS
    n_ch = n_pages // cp
    kern = functools.partial(_paged_kernel, n_b=n_b, n_ch=n_ch, cp=cp, page=page, tp=tp, n_new=n_new)
    grid_spec = pltpu.PrefetchScalarGridSpec(
        num_scalar_prefetch=1,
        grid=(1,),
        in_specs=[
            pl.BlockSpec(q.shape, lambda i, pt: (0, 0, 0)),
            pl.BlockSpec(knew.shape, lambda i, pt: (0, 0, 0)),
            pl.BlockSpec(memory_space=pl.ANY),
            pl.BlockSpec(memory_space=pl.ANY),
        ],
        out_specs=pl.BlockSpec((n_b, rows, KV_RANK), lambda i, pt: (0, 0, 0)),
        scratch_shapes=[
            pltpu.VMEM((2, cp * page, KV_RANK), F32),
            pltpu.VMEM((2, cp * page, QK_ROPE), F32),
            pltpu.SemaphoreType.DMA((2, 2)),
            pltpu.VMEM((rows, 1), F32),
            pltpu.VMEM((rows, 1), F32),
            pltpu.VMEM((rows, KV_RANK), F32),
        ],
    )
    return pl.pallas_call(
        kern,
        grid_spec=grid_spec,
        out_shape=jax.ShapeDtypeStruct((n_b, rows, KV_RANK), F32),
        compiler_params=_cparams(1),
        name="paged_attention",
    )(page_table, q, knew, cache_ckv, cache_krope)


def _v_up_kernel(o_ref, wv_ref, tok_ref):
    n_b, _, tp, _ = o_ref.shape
    for hd in range(MLA_HEADS):
        oh = o_ref[:, hd].reshape(n_b * tp, KV_RANK).astype(BF16)
        tok_ref[:, :, hd * V_HEAD:(hd + 1) * V_HEAD] = _dot(oh, wv_ref[hd]).reshape(n_b, tp, V_HEAD)


def _v_up(o, w_vup):
    n_b, H, tp, _ = o.shape
    return pl.pallas_call(
        _v_up_kernel,
        out_shape=jax.ShapeDtypeStruct((n_b, tp, MLA_WIDTH), F32),
        compiler_params=pltpu.CompilerParams(vmem_limit_bytes=VMEM_LIMIT),
        name="v_up_sample",
    )(o, w_vup)


def _b_post_kernel(x_ref, tok_ref, z_ref, mk_ref, mv_ref, wout_ref, gf_ref, xo_ref, *, final):
    x = x_ref[0]
    acc = x
    for g in range(4):
        c0, c1 = g * 256, (g + 1) * 256
        mt = (tok_ref[0, :, c0:c1] * _silu(z_ref[0, :, c0:c1])).astype(BF16)
        acc = acc + _dot(mt, wout_ref[c0:c1, :])
    mem_o = _mem_attend(z_ref[0, :, 1024:1536], mk_ref, mv_ref)
    for hd in range(MEM_HEADS):
        c0, c1 = hd * MEM_HEAD_DIM, (hd + 1) * MEM_HEAD_DIM
        mm = (mem_o[hd] * _silu(z_ref[0, :, 1536 + c0:1536 + c1])).astype(BF16)
        acc = acc + _dot(mm, wout_ref[1024 + c0:1024 + c1, :])
    if final:
        acc = _rms(acc, gf_ref[...])
    xo_ref[0] = acc


def _b_post(x, tok, zrest, mem_k, mem_v, w_out, g_final, *, layer, tq, final, tag):
    B, T, _ = x.shape
    kern = functools.partial(_b_post_kernel, final=final)
    return pl.pallas_call(
        kern,
        grid=(B, T // tq),
        in_specs=[
            pl.BlockSpec((1, tq, D_MODEL), lambda b, t: (b, t, 0)),
            pl.BlockSpec((1, tq, MLA_WIDTH), lambda b, t: (b, t, 0)),
            pl.BlockSpec((1, tq, 2048), lambda b, t: (b, t, 0)),
            pl.BlockSpec((1, 1, MEM_TOKENS, MEM_WIDTH), lambda b, t: (layer, b, 0, 0)),
            pl.BlockSpec((1, 1, MEM_TOKENS, MEM_WIDTH), lambda b, t: (layer, b, 0, 0)),
            pl.BlockSpec((1536, D_MODEL), lambda b, t: (0, 0)),
            pl.BlockSpec((1, D_MODEL), lambda b, t: (0, 0)),
        ],
        out_specs=pl.BlockSpec((1, tq, D_MODEL), lambda b, t: (b, t, 0)),
        out_shape=jax.ShapeDtypeStruct((B, T, D_MODEL), F32),
        compiler_params=_cparams(2),
        name=f"b_post{layer}_{tag}",
    )(x, tok, zrest, mem_k, mem_v, w_out, g_final)


def _rope_tables(pos):
    half = QK_ROPE // 2
    inv = ROPE_THETA ** (-jnp.arange(half, dtype=F32) / half)
    ang = pos.astype(F32)[:, None] * inv[None, :]
    return jnp.tile(jnp.cos(ang), (1, LANES // half)), jnp.tile(jnp.sin(ang), (1, LANES // half))


def kernel(x_prompt, x_sample, state_pool, cache_ckv, cache_krope, cache_mem_k, cache_mem_v, page_table, mem_prompt, g_norm, w_in_a, w_pool_grp, pool_scale, w_in_b, g_q_latent, w_q_up, g_kv_in, w_kv_down, g_kv_latent, w_k_up, w_v_up, g_mem, w_mem_k, w_mem_v, w_out, g_final):
    B, T, _ = x_prompt.shape
    SB, ST, _ = x_sample.shape
    TP = SUBLANES
    past = page_table.shape[1] * cache_ckv.shape[1]

    w_in_a16 = w_in_a.astype(BF16)
    w_grp16 = w_pool_grp.astype(BF16)
    w_out16 = w_out.astype(BF16)
    w_cq16 = w_in_b[:, :, :Q_RANK].astype(BF16)
    w_rest16 = w_in_b[:, :, Q_RANK:].astype(BF16)
    wq = w_q_up.reshape(N_A, Q_RANK, MLA_HEADS, QK_NOPE + QK_ROPE)
    w_qn16 = wq[..., :QK_NOPE].reshape(N_A, Q_RANK, MLA_HEADS * QK_NOPE).astype(BF16)
    w_qr16 = wq[..., QK_NOPE:].reshape(N_A, Q_RANK, MLA_HEADS * QK_ROPE).astype(BF16)
    w_kup16 = jnp.transpose(w_k_up, (1, 2, 0)).astype(BF16)
    w_vup16 = jnp.transpose(w_v_up, (1, 0, 2)).astype(BF16)
    w_kv16 = jnp.pad(w_kv_down, ((0, 0), (0, KCAT - KV_RANK - QK_ROPE))).astype(BF16)
    w_mk16 = w_mem_k.astype(BF16)
    w_mv16 = w_mem_v.astype(BF16)
    g_kv_in2 = g_kv_in[None]
    g_lat2 = g_kv_latent[None]
    g_final2 = g_final[None]

    def trunk(x, prev, mem_k, mem_v, pos, *, tag, tq, tq_kv, n_valid, attend):
        pools = []
        for l in range(N_A):
            x, pool = _a_layer(x, prev[l], g_norm[l][None], w_in_a16[l], w_grp16[l], pool_scale[l][None],
                               mem_k, mem_v, w_out16[l], layer=l, tq=tq, n_valid=n_valid, pos0=pos)
            pools.append(pool[:, 1:])
        cos, sin = _rope_tables(pos + jnp.arange(x.shape[1], dtype=jnp.int32))
        ckv, krope, kcat = _kv_latent(x, g_kv_in2, w_kv16, g_lat2, cos, sin, tq=tq_kv)
        for j in range(DEPTH - N_A):
            l = N_A + j
            q, zrest = _b_pre(x, g_norm[l][None], w_cq16[j], w_rest16[j], g_q_latent[j][None],
                              w_qn16[j], w_qr16[j], w_kup16, cos, sin, tq=tq, tag=f"{tag}{l}")
            tok = attend(q, kcat)
            x = _b_post(x, tok, zrest, mem_k, mem_v, w_out16[l], g_final2, layer=l, tq=tq,
                        final=(l == DEPTH - 1), tag=tag)
        return x, jnp.stack(pools, axis=0), ckv, krope

    mem_k_p, mem_v_p = _mem_project(mem_prompt, g_mem, w_mk16, w_mv16)
    prev_p = jnp.zeros((N_A, B, HALO, D_MODEL), F32)
    y_p, pool_p, ckv_p, krope_p = trunk(
        x_prompt, prev_p, mem_k_p, mem_v_p, 0, tag="p", tq=256, tq_kv=512, n_valid=256,
        attend=lambda q, kcat: _flash(q, kcat, w_vup16, tq=128, tk=256))

    xs = jnp.pad(x_sample, ((0, 0), (0, TP - ST), (0, 0)))
    prev_s = jnp.pad(state_pool, ((0, 0), (0, 0), (HALO - POOL_BUF, 0), (0, 0)))
    mem_k_s = cache_mem_k.reshape(DEPTH, SB, MEM_TOKENS, MEM_WIDTH)
    mem_v_s = cache_mem_v.reshape(DEPTH, SB, MEM_TOKENS, MEM_WIDTH)

    def attend_sample(q, kcat):
        qf = q.reshape(SB, MLA_HEADS * TP, KCAT)
        knew = jnp.pad(kcat, ((0, 0), (0, LANES - TP), (0, 0)))
        o = _paged_attention(page_table, qf, knew, cache_ckv, cache_krope, cp=8, n_new=ST)
        return _v_up(o.reshape(SB, MLA_HEADS, TP, KV_RANK), w_vup16)

    y_s, pool_s, ckv_s, krope_s = trunk(
        xs, prev_s, mem_k_s, mem_v_s, past, tag="s", tq=TP, tq_kv=TP, n_valid=ST, attend=attend_sample)

    mem_shape = (DEPTH, B, MEM_TOKENS, MEM_HEADS, MEM_HEAD_DIM)
    return (y_p, y_s[:, :ST], pool_p, pool_s, ckv_p, krope_p, ckv_s[:, :ST], krope_s[:, :ST],
            mem_k_p.reshape(mem_shape), mem_v_p.reshape(mem_shape))
```

```python
import functools

import jax
import jax.numpy as jnp
from jax import lax
from jax.experimental import pallas as pl
from jax.experimental.pallas import tpu as pltpu

F32 = jnp.float32
BF16 = jnp.bfloat16

D_MODEL = 1024
DEPTH = 4
N_A = 2
POOL_WINDOWS = (2, 4, 8, 16)
POOL_GROUP = 256
POOL_BUF = 15
MLA_HEADS = 8
QK_NOPE = 128
QK_ROPE = 64
V_HEAD = 128
KV_RANK = 256
Q_RANK = 384
MLA_WIDTH = MLA_HEADS * V_HEAD
MLA_SCALE = (QK_NOPE + QK_ROPE) ** -0.5
ROPE_THETA = 10000.0
MEM_TOKENS = 256
MEM_HEADS = 4
MEM_HEAD_DIM = 128
MEM_WIDTH = MEM_HEADS * MEM_HEAD_DIM
MEM_SCALE = MEM_HEAD_DIM ** -0.5
EPS = 1e-6
NEG = -1e30
LOG2E = 1.4426950408889634

KCAT = 384
SUBLANES = 8
LANES = 128
HALO = 16
E_OFF = SUBLANES + HALO
VMEM_LIMIT = 56 * 1024 * 1024
MEM_ROWS = MEM_TOKENS * MEM_HEADS


def _cparams(n_axes):
    return pltpu.CompilerParams(dimension_semantics=("arbitrary",) * n_axes,
                                vmem_limit_bytes=VMEM_LIMIT)


def _rms(x, g):
    return x * lax.rsqrt(jnp.mean(x * x, axis=-1, keepdims=True) + EPS) * g


def _dot(a, b):
    return jnp.dot(a, b, preferred_element_type=F32)


def _dot_nt(a, b):
    return lax.dot_general(a, b, (((1,), (1,)), ((), ())), preferred_element_type=F32)


def _silu(x):
    return x * jax.nn.sigmoid(x)


def _rope_lanes(a, cos, sin):
    lane = lax.broadcasted_iota(jnp.int32, a.shape, 1)
    up = pltpu.roll(a, 32, 1)
    dn = pltpu.roll(a, 96, 1)
    rot = jnp.where((lane & 63) < 32, -dn, up)
    return a * cos + rot * sin


def _rep(x, n):
    return x if n == 1 else jnp.concatenate([x] * n, axis=1)


def _mem_attend(qm, mk_ref, mv_ref):
    outs = []
    for h in range(MEM_HEADS):
        c0, c1 = h * MEM_HEAD_DIM, (h + 1) * MEM_HEAD_DIM
        q = (qm[:, c0:c1] * (MEM_SCALE * LOG2E)).astype(BF16)
        k = mk_ref[0, 0, pl.ds(h, MEM_TOKENS, stride=MEM_HEADS), :].astype(BF16)
        v = mv_ref[0, 0, pl.ds(h, MEM_TOKENS, stride=MEM_HEADS), :].astype(BF16)
        s = _dot_nt(q, k)
        m = jnp.max(s, axis=-1, keepdims=True)
        p = jnp.exp2(s - m)
        l = jnp.sum(p, axis=-1, keepdims=True)
        outs.append(_dot(p.astype(BF16), v) * (1.0 / l))
    return outs


def _mem_proj_kernel(mem_ref, g_ref, wk_ref, wv_ref, mk_ref, mv_ref):
    x = mem_ref[0]
    xn = x * lax.rsqrt(jnp.mean(x * x, axis=-1, keepdims=True) + EPS)
    for l in range(DEPTH):
        h = (xn * g_ref[l:l + 1, :]).astype(BF16)
        mk = _dot(h, wk_ref[l])
        mv = _dot(h, wv_ref[l])
        for hd in range(MEM_HEADS):
            c0, c1 = hd * MEM_HEAD_DIM, (hd + 1) * MEM_HEAD_DIM
            mk_ref[l, 0, pl.ds(hd, MEM_TOKENS, stride=MEM_HEADS), :] = mk[:, c0:c1]
            mv_ref[l, 0, pl.ds(hd, MEM_TOKENS, stride=MEM_HEADS), :] = mv[:, c0:c1]


def _mem_project(mem, g_mem, wk, wv):
    B, M, _ = mem.shape
    out = jax.ShapeDtypeStruct((DEPTH, B, MEM_ROWS, MEM_HEAD_DIM), F32)
    return pl.pallas_call(
        _mem_proj_kernel,
        grid=(B,),
        in_specs=[
            pl.BlockSpec((1, M, D_MODEL), lambda b: (b, 0, 0)),
            pl.BlockSpec((DEPTH, D_MODEL), lambda b: (0, 0)),
            pl.BlockSpec((DEPTH, D_MODEL, MEM_WIDTH), lambda b: (0, 0, 0)),
            pl.BlockSpec((DEPTH, D_MODEL, MEM_WIDTH), lambda b: (0, 0, 0)),
        ],
        out_specs=[
            pl.BlockSpec((DEPTH, 1, MEM_ROWS, MEM_HEAD_DIM), lambda b: (0, b, 0, 0)),
            pl.BlockSpec((DEPTH, 1, MEM_ROWS, MEM_HEAD_DIM), lambda b: (0, b, 0, 0)),
        ],
        out_shape=[out, out],
        compiler_params=_cparams(1),
        name="mem_project",
    )(mem, g_mem, wk, wv)


def _a_layer_kernel(x_ref, prev_ref, g_ref, win_ref, wgrp_ref, pscale_ref, mk_ref, mv_ref, wout_ref,
                    xo_ref, pool_ref, e_sc, sa_sc, sb_sc, *, tq, n_t, n_valid, pos0):
    t = pl.program_id(1)
    rows = HALO + tq

    @pl.when(t == 0)
    def _():
        zeros = jnp.zeros((SUBLANES, D_MODEL), F32)
        e_sc[0:SUBLANES, :] = zeros
        sa_sc[0:SUBLANES, :] = zeros
        sb_sc[0:SUBLANES, :] = zeros
        e_sc[SUBLANES:E_OFF, :] = prev_ref[0]

    x = x_ref[0]
    h = _rms(x, g_ref[...]).astype(BF16)
    e_sc[E_OFF:E_OFF + tq, :] = _dot(h, win_ref[:, 0:1024])

    lo = SUBLANES
    sa_sc[lo:lo + rows, :] = e_sc[lo:lo + rows, :] + e_sc[lo - 1:lo - 1 + rows, :]
    sb_sc[lo:lo + rows, 256:1024] = sa_sc[lo:lo + rows, 256:1024] + sa_sc[lo - 2:lo - 2 + rows, 256:1024]
    sa_sc[lo:lo + rows, 512:1024] = sb_sc[lo:lo + rows, 512:1024] + sb_sc[lo - 4:lo - 4 + rows, 512:1024]
    s16 = sa_sc[E_OFF:E_OFF + tq, 768:1024] + sa_sc[E_OFF - 8:E_OFF - 8 + tq, 768:1024]
    sums = (sa_sc[E_OFF:E_OFF + tq, 0:256], sb_sc[E_OFF:E_OFF + tq, 256:512],
            sa_sc[E_OFF:E_OFF + tq, 512:768], s16)

    pos = pos0 + t * tq + lax.broadcasted_iota(jnp.int32, (tq, 1), 0)
    mixed_tok = []
    for g, w in enumerate(POOL_WINDOWS):
        c0, c1 = g * POOL_GROUP, (g + 1) * POOL_GROUP
        inv_cnt = 1.0 / jnp.minimum(pos + 1, w).astype(F32)
        pooled = sums[g] * inv_cnt - e_sc[E_OFF:E_OFF + tq, c0:c1]
        mixed = _dot(pooled.astype(BF16), wgrp_ref[g]) * pscale_ref[:, c0:c1]
        gate = _dot(h, win_ref[:, 1024 + c0:1024 + c1])
        mixed_tok.append((mixed * _silu(gate)).astype(BF16))

    qm = _dot(h, win_ref[:, 2048:2560])
    gate_m = _dot(h, win_ref[:, 2560:3072])
    mem_o = _mem_attend(qm, mk_ref, mv_ref)

    acc = x
    for g in range(4):
        acc = acc + _dot(mixed_tok[g], wout_ref[g * 256:(g + 1) * 256, :])
    for hd in range(MEM_HEADS):
        c0, c1 = hd * MEM_HEAD_DIM, (hd + 1) * MEM_HEAD_DIM
        mm = (mem_o[hd] * _silu(gate_m[:, c0:c1])).astype(BF16)
        acc = acc + _dot(mm, wout_ref[1024 + c0:1024 + c1, :])
    xo_ref[0] = acc

    @pl.when(t == n_t - 1)
    def _():
        pool_ref[0] = e_sc[E_OFF + n_valid - HALO:E_OFF + n_valid, :]

    if n_t > 1:
        @pl.when(t < n_t - 1)
        def _():
            e_sc[SUBLANES:E_OFF, :] = e_sc[SUBLANES + tq:E_OFF + tq, :]


def _a_layer(x, prev, g, w_in, w_grp, pscale, mem_k, mem_v, w_out, *, layer, tq, n_valid, pos0):
    B, T, _ = x.shape
    n_t = T // tq
    rows = E_OFF + tq
    kern = functools.partial(_a_layer_kernel, tq=tq, n_t=n_t, n_valid=n_valid, pos0=pos0)
    return pl.pallas_call(
        kern,
        grid=(B, n_t),
        in_specs=[
            pl.BlockSpec((1, tq, D_MODEL), lambda b, t: (b, t, 0)),
            pl.BlockSpec((1, HALO, D_MODEL), lambda b, t: (b, 0, 0)),
            pl.BlockSpec((1, D_MODEL), lambda b, t: (0, 0)),
            pl.BlockSpec((D_MODEL, 3072), lambda b, t: (0, 0)),
            pl.BlockSpec((4, POOL_GROUP, POOL_GROUP), lambda b, t: (0, 0, 0)),
            pl.BlockSpec((1, D_MODEL), lambda b, t: (0, 0)),
            pl.BlockSpec((1, 1, MEM_ROWS, MEM_HEAD_DIM), lambda b, t: (layer, b, 0, 0)),
            pl.BlockSpec((1, 1, MEM_ROWS, MEM_HEAD_DIM), lambda b, t: (layer, b, 0, 0)),
            pl.BlockSpec((1536, D_MODEL), lambda b, t: (0, 0)),
        ],
        out_specs=[
            pl.BlockSpec((1, tq, D_MODEL), lambda b, t: (b, t, 0)),
            pl.BlockSpec((1, HALO, D_MODEL), lambda b, t: (b, 0, 0)),
        ],
        out_shape=[jax.ShapeDtypeStruct((B, T, D_MODEL), F32),
                   jax.ShapeDtypeStruct((B, HALO, D_MODEL), F32)],
        scratch_shapes=[pltpu.VMEM((rows, D_MODEL), F32)] * 3,
        compiler_params=_cparams(2),
        name=f"a_layer{layer}_{B}",
    )(x, prev, g, w_in, w_grp, pscale, mem_k, mem_v, w_out)


def _kv_latent_kernel(x_ref, gin_ref, wkv_ref, glat_ref, cos_ref, sin_ref, ckv_ref, kr_ref, kcat_ref,
                      *maybe_kt_ref):
    h = _rms(x_ref[0], gin_ref[...]).astype(BF16)
    kv = _dot(h, wkv_ref[...])
    ckv = _rms(kv[:, 0:KV_RANK], glat_ref[...])
    kr = _rope_lanes(kv[:, KV_RANK:KCAT], cos_ref[...], sin_ref[...])
    ckv_ref[0] = ckv
    kr_ref[0] = kr[:, 0:QK_ROPE]
    kcat_ref[0, :, 0:KV_RANK] = ckv.astype(BF16)
    kcat_ref[0, :, KV_RANK:KCAT] = kr.astype(BF16)
    for kt_ref in maybe_kt_ref:
        kt_ref[0, 0, 0:KV_RANK, :] = ckv.T.astype(BF16)
        kt_ref[0, 0, KV_RANK:KCAT, :] = kr.T.astype(BF16)


def _kv_latent(x, g_in, w_kv, g_lat, cos, sin, *, tq, emit_kt):
    B, T, _ = x.shape
    out_specs = [
        pl.BlockSpec((1, tq, KV_RANK), lambda b, t: (b, t, 0)),
        pl.BlockSpec((1, tq, QK_ROPE), lambda b, t: (b, t, 0)),
        pl.BlockSpec((1, tq, KCAT), lambda b, t: (b, t, 0)),
    ]
    out_shape = [jax.ShapeDtypeStruct((B, T, KV_RANK), F32),
                 jax.ShapeDtypeStruct((B, T, QK_ROPE), F32),
                 jax.ShapeDtypeStruct((B, T, KCAT), BF16)]
    if emit_kt:
        out_specs.append(pl.BlockSpec((1, 1, KCAT, tq), lambda b, t: (b, t, 0, 0)))
        out_shape.append(jax.ShapeDtypeStruct((B, T // tq, KCAT, tq), BF16))
    return pl.pallas_call(
        _kv_latent_kernel,
        grid=(B, T // tq),
        in_specs=[
            pl.BlockSpec((1, tq, D_MODEL), lambda b, t: (b, t, 0)),
            pl.BlockSpec((1, D_MODEL), lambda b, t: (0, 0)),
            pl.BlockSpec((D_MODEL, KCAT), lambda b, t: (0, 0)),
            pl.BlockSpec((1, KV_RANK), lambda b, t: (0, 0)),
            pl.BlockSpec((tq, LANES), lambda b, t: (t, 0)),
            pl.BlockSpec((tq, LANES), lambda b, t: (t, 0)),
        ],
        out_specs=out_specs,
        out_shape=out_shape,
        compiler_params=_cparams(2),
        name=f"kv_latent_{B}",
    )(x, g_in, w_kv, g_lat, cos, sin)


def _b_pre_kernel(x_ref, g_ref, wcq_ref, wrest_ref, gq_ref, wqn_ref, wqr_ref, wkup_ref, cos_ref, sin_ref,
                  q_ref, zrest_ref):
    h = _rms(x_ref[0], g_ref[...]).astype(BF16)
    zrest_ref[0] = _dot(h, wrest_ref[...])
    cn = _rms(_dot(h, wcq_ref[...]), gq_ref[...]).astype(BF16)
    qn = _dot(cn, wqn_ref[...])
    qr = _dot(cn, wqr_ref[...])
    for hd in range(MLA_HEADS):
        ql = _dot(qn[:, hd * QK_NOPE:(hd + 1) * QK_NOPE].astype(BF16), wkup_ref[hd])
        q_ref[0, hd, :, 0:KV_RANK] = (ql * (MLA_SCALE * LOG2E)).astype(BF16)
    cos = cos_ref[...]
    sin = sin_ref[...]
    lane = lax.broadcasted_iota(jnp.int32, cos.shape, 1)
    for j in range(MLA_HEADS // 2):
        rr = _rope_lanes(qr[:, j * LANES:(j + 1) * LANES], cos, sin) * (MLA_SCALE * LOG2E)
        q_ref[0, 2 * j, :, KV_RANK:KCAT] = jnp.where(lane < QK_ROPE, rr, 0.0).astype(BF16)
        q_ref[0, 2 * j + 1, :, KV_RANK:KCAT] = jnp.where(lane < QK_ROPE, pltpu.roll(rr, 64, 1), 0.0).astype(BF16)


def _b_pre(x, g, w_cq, w_rest, g_q, w_qn, w_qr, w_kup, cos, sin, *, tq, tag):
    B, T, _ = x.shape
    const2 = lambda b, t: (0, 0)
    return pl.pallas_call(
        _b_pre_kernel,
        grid=(B, T // tq),
        in_specs=[
            pl.BlockSpec((1, tq, D_MODEL), lambda b, t: (b, t, 0)),
            pl.BlockSpec((1, D_MODEL), const2),
            pl.BlockSpec((D_MODEL, Q_RANK), const2),
            pl.BlockSpec((D_MODEL, 2048), const2),
            pl.BlockSpec((1, Q_RANK), const2),
            pl.BlockSpec((Q_RANK, MLA_HEADS * QK_NOPE), const2),
            pl.BlockSpec((Q_RANK, MLA_HEADS * QK_ROPE), const2),
            pl.BlockSpec((MLA_HEADS, QK_NOPE, KV_RANK), lambda b, t: (0, 0, 0)),
            pl.BlockSpec((tq, LANES), lambda b, t: (t, 0)),
            pl.BlockSpec((tq, LANES), lambda b, t: (t, 0)),
        ],
        out_specs=[
            pl.BlockSpec((1, MLA_HEADS, tq, KCAT), lambda b, t: (b, 0, t, 0)),
            pl.BlockSpec((1, tq, 2048), lambda b, t: (b, t, 0)),
        ],
        out_shape=[jax.ShapeDtypeStruct((B, MLA_HEADS, T, KCAT), BF16),
                   jax.ShapeDtypeStruct((B, T, 2048), F32)],
        compiler_params=_cparams(2),
        name=f"b_pre_{tag}",
    )(x, g, w_cq, w_rest, g_q, w_qn, w_qr, w_kup, cos, sin)


def _flash_kernel(q_ref, k_ref, kt_ref, wv_ref, o_ref, m_sc, l_sc, acc_sc, *, tq, tk):
    i = pl.program_id(1)
    m_sc[...] = jnp.full(m_sc.shape, NEG, F32)
    l_sc[...] = jnp.zeros(l_sc.shape, F32)
    acc_sc[...] = jnp.zeros(acc_sc.shape, F32)

    def block(j, masked):
        kt = kt_ref[0, j]
        v = k_ref[0, pl.ds(pl.multiple_of(j * tk, tk), tk), 0:KV_RANK]
        if masked:
            keep = (lax.broadcasted_iota(jnp.int32, (tq, tk), 1)
                    <= lax.broadcasted_iota(jnp.int32, (tq, tk), 0))
        s_next = _dot(q_ref[0, 0], kt)
        for hd in range(MLA_HEADS):
            s = s_next
            if hd + 1 < MLA_HEADS:
                s_next = _dot(q_ref[0, hd + 1], kt)
            if masked:
                s = jnp.where(keep, s, NEG)
            m_prev = m_sc[hd]
            m_new = jnp.maximum(m_prev, jnp.max(s, axis=-1, keepdims=True))
            alpha = jnp.exp2(m_prev - m_new)
            p = jnp.exp2(s - _rep(m_new, tk // LANES))
            l_sc[hd] = alpha * l_sc[hd] + jnp.sum(p, axis=-1, keepdims=True)
            acc_sc[hd] = _rep(alpha, KV_RANK // LANES) * acc_sc[hd] + _dot(p.astype(BF16), v)
            m_sc[hd] = m_new

    def body(j, carry):
        block(j, False)
        return carry

    lax.fori_loop(0, i, body, 0)
    block(i, True)

    for hd in range(MLA_HEADS):
        o = acc_sc[hd] * _rep(1.0 / l_sc[hd], KV_RANK // LANES)
        o_ref[0, :, hd * V_HEAD:(hd + 1) * V_HEAD] = _dot(o.astype(BF16), wv_ref[hd])


def _flash(q, kcat, kt, w_vup, *, tq):
    B, H, T, _ = q.shape
    tk = kt.shape[-1]
    assert tq == tk
    kern = functools.partial(_flash_kernel, tq=tq, tk=tk)
    return pl.pallas_call(
        kern,
        grid=(B, T // tq),
        in_specs=[
            pl.BlockSpec((1, H, tq, KCAT), lambda b, i: (b, 0, i, 0)),
            pl.BlockSpec((1, T, KCAT), lambda b, i: (b, 0, 0)),
            pl.BlockSpec((1, T // tk, KCAT, tk), lambda b, i: (b, 0, 0, 0)),
            pl.BlockSpec((H, KV_RANK, V_HEAD), lambda b, i: (0, 0, 0)),
        ],
        out_specs=pl.BlockSpec((1, tq, MLA_WIDTH), lambda b, i: (b, i, 0)),
        out_shape=jax.ShapeDtypeStruct((B, T, MLA_WIDTH), F32),
        scratch_shapes=[pltpu.VMEM((H, tq, LANES), F32), pltpu.VMEM((H, tq, LANES), F32),
                        pltpu.VMEM((H, tq, KV_RANK), F32)],
        compiler_params=_cparams(2),
        name="flash_prompt",
    )(q, kcat, kt, w_vup)


def _paged_kernel(pt_ref, q_ref, knew_ref, ckv_hbm, krt_hbm, o_ref,
                  ckv_buf, krt_buf, sem, m_sc, l_sc, acc_sc, *, n_b, n_ch, cp, page, tp, n_new, sub):
    total = n_b * n_ch
    n_sub = (cp * page) // sub

    def copies(bc, slot):
        b = bc // n_ch
        c = bc % n_ch
        out = []
        for p in range(cp):
            pg = pt_ref[b, c * cp + p]
            out.append(pltpu.make_async_copy(ckv_hbm.at[pg], ckv_buf.at[slot, pl.ds(p * page, page)],
                                             sem.at[0, slot]))
            out.append(pltpu.make_async_copy(krt_hbm.at[pg], krt_buf.at[slot, :, pl.ds(p * page, page)],
                                             sem.at[1, slot]))
        return out

    def stats(s, v_bf16):
        m = jnp.max(s, axis=-1, keepdims=True)
        p = jnp.exp2(s - m)
        return m, jnp.sum(p, axis=-1, keepdims=True), _dot(p.astype(BF16), v_bf16)

    def merge(parts):
        m_prev = m_sc[...]
        m_new = m_prev
        for m, _, _ in parts:
            m_new = jnp.maximum(m_new, m)
        alpha = jnp.exp2(m_prev - m_new)
        l_new = alpha * l_sc[...]
        acc = _rep(alpha, KV_RANK // LANES) * acc_sc[...]
        for m, l, o in parts:
            w = jnp.exp2(m - m_new)
            l_new = l_new + w * l
            acc = acc + _rep(w, KV_RANK // LANES) * o
        m_sc[...] = m_new
        l_sc[...] = l_new
        acc_sc[...] = acc

    def step(bc, slot):
        b = bc // n_ch
        c = bc % n_ch

        @pl.when(bc + 1 < total)
        def _():
            for cpy in copies(bc + 1, 1 - slot):
                cpy.start()

        @pl.when(c == 0)
        def _():
            m_sc[...] = jnp.full(m_sc.shape, NEG, F32)
            l_sc[...] = jnp.zeros(l_sc.shape, F32)
            acc_sc[...] = jnp.zeros(acc_sc.shape, F32)

        for cpy in copies(bc, slot):
            cpy.wait()

        q_lat = q_ref[b, :, 0:KV_RANK]
        q_rope = q_ref[b, :, KV_RANK:KV_RANK + QK_ROPE]
        kcs = [ckv_buf[slot, j * sub:(j + 1) * sub, :].astype(BF16) for j in range(n_sub)]
        krts = [krt_buf[slot, :, j * sub:(j + 1) * sub].astype(BF16) for j in range(n_sub)]
        scores = [_dot_nt(q_lat, kcs[j]) + _dot(q_rope, krts[j]) for j in range(n_sub)]
        ms = [jnp.max(s, axis=-1, keepdims=True) for s in scores]
        ps = [jnp.exp2(s - m) for s, m in zip(scores, ms)]
        ls = [jnp.sum(p, axis=-1, keepdims=True) for p in ps]
        os_ = [_dot(p.astype(BF16), kc) for p, kc in zip(ps, kcs)]
        merge(list(zip(ms, ls, os_)))

        @pl.when(c == n_ch - 1)
        def _():
            kn = knew_ref[b]
            sn = _dot_nt(q_ref[b], kn)
            tpos = lax.broadcasted_iota(jnp.int32, sn.shape, 0) & (tp - 1)
            col = lax.broadcasted_iota(jnp.int32, sn.shape, 1)
            sn = jnp.where((col <= tpos) & (col < n_new), sn, NEG)
            merge([stats(sn, kn[:, 0:KV_RANK])])
            o_ref[b] = acc_sc[...] * _rep(1.0 / l_sc[...], KV_RANK // LANES)

    for cpy in copies(0, 0):
        cpy.start()

    def pair(i, carry):
        step(2 * i, 0)
        step(2 * i + 1, 1)
        return carry

    lax.fori_loop(0, total // 2, pair, 0)


def _paged_attention(page_table, q, knew, cache_ckv, cache_krt, *, cp, n_new, sub):
    n_b, n_pages = page_table.shape
    page = cache_ckv.shape[1]
    rows = q.shape[1]
    tp = rows // MLA_HEADS
    n_ch = n_pages // cp
    assert (n_b * n_ch) % 2 == 0 and (cp * page) % sub == 0
    kern = functools.partial(_paged_kernel, n_b=n_b, n_ch=n_ch, cp=cp, page=page, tp=tp, n_new=n_new, sub=sub)
    grid_spec = pltpu.PrefetchScalarGridSpec(
        num_scalar_prefetch=1,
        grid=(1,),
        in_specs=[
            pl.BlockSpec(q.shape, lambda i, pt: (0, 0, 0)),
            pl.BlockSpec(knew.shape, lambda i, pt: (0, 0, 0)),
            pl.BlockSpec(memory_space=pl.ANY),
            pl.BlockSpec(memory_space=pl.ANY),
        ],
        out_specs=pl.BlockSpec((n_b, rows, KV_RANK), lambda i, pt: (0, 0, 0)),
        scratch_shapes=[
            pltpu.VMEM((2, cp * page, KV_RANK), F32),
            pltpu.VMEM((2, QK_ROPE, cp * page), F32),
            pltpu.SemaphoreType.DMA((2, 2)),
            pltpu.VMEM((rows, LANES), F32),
            pltpu.VMEM((rows, LANES), F32),
            pltpu.VMEM((rows, KV_RANK), F32),
        ],
    )
    return pl.pallas_call(
        kern,
        grid_spec=grid_spec,
        out_shape=jax.ShapeDtypeStruct((n_b, rows, KV_RANK), F32),
        compiler_params=_cparams(1),
        name="paged_attention",
    )(page_table, q, knew, cache_ckv, cache_krt)


def _v_up_kernel(o_ref, wv_ref, tok_ref):
    n_b, _, tp, _ = o_ref.shape
    for hd in range(MLA_HEADS):
        oh = o_ref[:, hd].reshape(n_b * tp, KV_RANK).astype(BF16)
        tok_ref[:, :, hd * V_HEAD:(hd + 1) * V_HEAD] = _dot(oh, wv_ref[hd]).reshape(n_b, tp, V_HEAD)


def _v_up(o, w_vup):
    n_b, H, tp, _ = o.shape
    return pl.pallas_call(
        _v_up_kernel,
        out_shape=jax.ShapeDtypeStruct((n_b, tp, MLA_WIDTH), F32),
        compiler_params=pltpu.CompilerParams(vmem_limit_bytes=VMEM_LIMIT),
        name="v_up_sample",
    )(o, w_vup)


def _b_post_kernel(x_ref, tok_ref, z_ref, mk_ref, mv_ref, wout_ref, gf_ref, xo_ref, *, final):
    x = x_ref[0]
    acc = x
    for g in range(4):
        c0, c1 = g * 256, (g + 1) * 256
        mt = (tok_ref[0, :, c0:c1] * _silu(z_ref[0, :, c0:c1])).astype(BF16)
        acc = acc + _dot(mt, wout_ref[c0:c1, :])
    mem_o = _mem_attend(z_ref[0, :, 1024:1536], mk_ref, mv_ref)
    for hd in range(MEM_HEADS):
        c0, c1 = hd * MEM_HEAD_DIM, (hd + 1) * MEM_HEAD_DIM
        mm = (mem_o[hd] * _silu(z_ref[0, :, 1536 + c0:1536 + c1])).astype(BF16)
        acc = acc + _dot(mm, wout_ref[1024 + c0:1024 + c1, :])
    if final:
        acc = _rms(acc, gf_ref[...])
    xo_ref[0] = acc


def _b_post(x, tok, zrest, mem_k, mem_v, w_out, g_final, *, layer, tq, final, tag):
    B, T, _ = x.shape
    kern = functools.partial(_b_post_kernel, final=final)
    return pl.pallas_call(
        kern,
        grid=(B, T // tq),
        in_specs=[
            pl.BlockSpec((1, tq, D_MODEL), lambda b, t: (b, t, 0)),
            pl.BlockSpec((1, tq, MLA_WIDTH), lambda b, t: (b, t, 0)),
            pl.BlockSpec((1, tq, 2048), lambda b, t: (b, t, 0)),
            pl.BlockSpec((1, 1, MEM_ROWS, MEM_HEAD_DIM), lambda b, t: (layer, b, 0, 0)),
            pl.BlockSpec((1, 1, MEM_ROWS, MEM_HEAD_DIM), lambda b, t: (layer, b, 0, 0)),
            pl.BlockSpec((1536, D_MODEL), lambda b, t: (0, 0)),
            pl.BlockSpec((1, D_MODEL), lambda b, t: (0, 0)),
        ],
        out_specs=pl.BlockSpec((1, tq, D_MODEL), lambda b, t: (b, t, 0)),
        out_shape=jax.ShapeDtypeStruct((B, T, D_MODEL), F32),
        compiler_params=_cparams(2),
        name=f"b_post{layer}_{tag}",
    )(x, tok, zrest, mem_k, mem_v, w_out, g_final)


def _rope_tables(pos):
    half = QK_ROPE // 2
    inv = ROPE_THETA ** (-jnp.arange(half, dtype=F32) / half)
    ang = pos.astype(F32)[:, None] * inv[None, :]
    return jnp.tile(jnp.cos(ang), (1, LANES // half)), jnp.tile(jnp.sin(ang), (1, LANES // half))


def kernel(x_prompt, x_sample, state_pool, cache_ckv, cache_krope, cache_mem_k, cache_mem_v, page_table, mem_prompt, g_norm, w_in_a, w_pool_grp, pool_scale, w_in_b, g_q_latent, w_q_up, g_kv_in, w_kv_down, g_kv_latent, w_k_up, w_v_up, g_mem, w_mem_k, w_mem_v, w_out, g_final):
    B, T, _ = x_prompt.shape
    SB, ST, _ = x_sample.shape
    TP = SUBLANES
    past = page_table.shape[1] * cache_ckv.shape[1]

    w_in_a16 = w_in_a.astype(BF16)
    w_grp16 = w_pool_grp.astype(BF16)
    w_out16 = w_out.astype(BF16)
    w_cq16 = w_in_b[:, :, :Q_RANK].astype(BF16)
    w_rest16 = w_in_b[:, :, Q_RANK:].astype(BF16)
    wq = w_q_up.reshape(N_A, Q_RANK, MLA_HEADS, QK_NOPE + QK_ROPE)
    w_qn16 = wq[..., :QK_NOPE].reshape(N_A, Q_RANK, MLA_HEADS * QK_NOPE).astype(BF16)
    w_qr16 = wq[..., QK_NOPE:].reshape(N_A, Q_RANK, MLA_HEADS * QK_ROPE).astype(BF16)
    w_kup16 = jnp.transpose(w_k_up, (1, 2, 0)).astype(BF16)
    w_vup16 = jnp.transpose(w_v_up, (1, 0, 2)).astype(BF16)
    w_kv16 = jnp.pad(w_kv_down, ((0, 0), (0, KCAT - KV_RANK - QK_ROPE))).astype(BF16)
    w_mk16 = w_mem_k.astype(BF16)
    w_mv16 = w_mem_v.astype(BF16)
    g_kv_in2 = g_kv_in[None]
    g_lat2 = g_kv_latent[None]
    g_final2 = g_final[None]

    def trunk(x, prev, mem_k, mem_v, pos, *, tag, tq, tq_kv, emit_kt, n_valid, attend):
        pools = []
        for l in range(N_A):
            x, pool = _a_layer(x, prev[l], g_norm[l][None], w_in_a16[l], w_grp16[l], pool_scale[l][None],
                               mem_k, mem_v, w_out16[l], layer=l, tq=tq, n_valid=n_valid, pos0=pos)
            pools.append(pool[:, 1:])
        cos, sin = _rope_tables(pos + jnp.arange(x.shape[1], dtype=jnp.int32))
        ckv, krope, *kcat = _kv_latent(x, g_kv_in2, w_kv16, g_lat2, cos, sin, tq=tq_kv, emit_kt=emit_kt)
        for j in range(DEPTH - N_A):
            l = N_A + j
            q, zrest = _b_pre(x, g_norm[l][None], w_cq16[j], w_rest16[j], g_q_latent[j][None],
                              w_qn16[j], w_qr16[j], w_kup16, cos, sin, tq=tq, tag=f"{tag}{l}")
            tok = attend(q, *kcat)
            x = _b_post(x, tok, zrest, mem_k, mem_v, w_out16[l], g_final2, layer=l, tq=tq,
                        final=(l == DEPTH - 1), tag=tag)
        return x, jnp.stack(pools, axis=0), ckv, krope

    mem_k_p, mem_v_p = _mem_project(mem_prompt, g_mem, w_mk16, w_mv16)
    prev_p = jnp.zeros((N_A, B, HALO, D_MODEL), F32)
    y_p, pool_p, ckv_p, krope_p = trunk(
        x_prompt, prev_p, mem_k_p, mem_v_p, 0, tag="p", tq=256, tq_kv=256, emit_kt=True, n_valid=256,
        attend=lambda q, kcat, kt: _flash(q, kcat, kt, w_vup16, tq=256))

    xs = jnp.pad(x_sample, ((0, 0), (0, TP - ST), (0, 0)))
    prev_s = jnp.pad(state_pool, ((0, 0), (0, 0), (HALO - POOL_BUF, 0), (0, 0)))
    mem_k_s = cache_mem_k.reshape(DEPTH, SB, MEM_ROWS, MEM_HEAD_DIM)
    mem_v_s = cache_mem_v.reshape(DEPTH, SB, MEM_ROWS, MEM_HEAD_DIM)
    cache_krt = jnp.transpose(cache_krope, (0, 2, 1))

    def attend_sample(q, kcat):
        qf = q.reshape(SB, MLA_HEADS * TP, KCAT)
        knew = jnp.pad(kcat, ((0, 0), (0, LANES - TP), (0, 0)))
        o = _paged_attention(page_table, qf, knew, cache_ckv, cache_krt, cp=8, n_new=ST, sub=256)
        return _v_up(o.reshape(SB, MLA_HEADS, TP, KV_RANK), w_vup16)

    y_s, pool_s, ckv_s, krope_s = trunk(
        xs, prev_s, mem_k_s, mem_v_s, past, tag="s", tq=TP, tq_kv=TP, emit_kt=False, n_valid=ST,
        attend=attend_sample)

    mem_shape = (DEPTH, B, MEM_TOKENS, MEM_HEADS, MEM_HEAD_DIM)
    return (y_p, y_s[:, :ST], pool_p, pool_s, ckv_p, krope_p, ckv_s[:, :ST], krope_s[:, :ST],
            mem_k_p.reshape(mem_shape), mem_v_p.reshape(mem_shape))
```

```python
import functools

import jax
import jax.numpy as jnp
from jax import lax
from jax.experimental import pallas as pl
from jax.experimental.pallas import tpu as pltpu

F32 = jnp.float32
BF16 = jnp.bfloat16

D_MODEL = 1024
DEPTH = 4
N_A = 2
POOL_WINDOWS = (2, 4, 8, 16)
POOL_GROUP = 256
POOL_BUF = 15
MLA_HEADS = 8
QK_NOPE = 128
QK_ROPE = 64
V_HEAD = 128
KV_RANK = 256
Q_RANK = 384
MLA_WIDTH = MLA_HEADS * V_HEAD
MLA_SCALE = (QK_NOPE + QK_ROPE) ** -0.5
ROPE_THETA = 10000.0
MEM_TOKENS = 256
MEM_HEADS = 4
MEM_HEAD_DIM = 128
MEM_WIDTH = MEM_HEADS * MEM_HEAD_DIM
MEM_SCALE = MEM_HEAD_DIM ** -0.5
EPS = 1e-6
NEG = -1e30
LOG2E = 1.4426950408889634

KCAT = 384
SUBLANES = 8
LANES = 128
HALO = 16
E_OFF = SUBLANES + HALO
VMEM_LIMIT = 56 * 1024 * 1024
MEM_ROWS = MEM_TOKENS * MEM_HEADS
N_RAW = 3


def _cparams(n_axes):
    return pltpu.CompilerParams(dimension_semantics=("arbitrary",) * n_axes,
                                vmem_limit_bytes=VMEM_LIMIT)


def _rms(x, g):
    return x * lax.rsqrt(jnp.mean(x * x, axis=-1, keepdims=True) + EPS) * g


def _dot(a, b):
    return jnp.dot(a, b, preferred_element_type=F32)


def _dot_nt(a, b):
    return lax.dot_general(a, b, (((1,), (1,)), ((), ())), preferred_element_type=F32)


def _silu(x):
    return x * jax.nn.sigmoid(x)


def _rope_lanes(a, cos, sin):
    lane = lax.broadcasted_iota(jnp.int32, a.shape, 1)
    up = pltpu.roll(a, 32, 1)
    dn = pltpu.roll(a, 96, 1)
    rot = jnp.where((lane & 63) < 32, -dn, up)
    return a * cos + rot * sin


def _rep(x, n):
    return x if n == 1 else jnp.concatenate([x] * n, axis=1)


def _mem_attend(qm, mk_ref, mv_ref):
    outs = []
    for h in range(MEM_HEADS):
        c0, c1 = h * MEM_HEAD_DIM, (h + 1) * MEM_HEAD_DIM
        q = (qm[:, c0:c1] * (MEM_SCALE * LOG2E)).astype(BF16)
        k = mk_ref[0, 0, pl.ds(h, MEM_TOKENS, stride=MEM_HEADS), :].astype(BF16)
        v = mv_ref[0, 0, pl.ds(h, MEM_TOKENS, stride=MEM_HEADS), :].astype(BF16)
        s = _dot_nt(q, k)
        m = jnp.max(s, axis=-1, keepdims=True)
        p = jnp.exp2(s - m)
        l = jnp.sum(p, axis=-1, keepdims=True)
        outs.append(_dot(p.astype(BF16), v) * (1.0 / l))
    return outs


def _mem_proj_kernel(mem_ref, g_ref, wk_ref, wv_ref, mk_ref, mv_ref):
    x = mem_ref[0]
    xn = x * lax.rsqrt(jnp.mean(x * x, axis=-1, keepdims=True) + EPS)
    for l in range(DEPTH):
        h = (xn * g_ref[l:l + 1, :]).astype(BF16)
        mk = _dot(h, wk_ref[l])
        mv = _dot(h, wv_ref[l])
        for hd in range(MEM_HEADS):
            c0, c1 = hd * MEM_HEAD_DIM, (hd + 1) * MEM_HEAD_DIM
            mk_ref[l, 0, pl.ds(hd, MEM_TOKENS, stride=MEM_HEADS), :] = mk[:, c0:c1]
            mv_ref[l, 0, pl.ds(hd, MEM_TOKENS, stride=MEM_HEADS), :] = mv[:, c0:c1]


def _mem_project(mem, g_mem, wk, wv):
    B, M, _ = mem.shape
    out = jax.ShapeDtypeStruct((DEPTH, B, MEM_ROWS, MEM_HEAD_DIM), F32)
    return pl.pallas_call(
        _mem_proj_kernel,
        grid=(B,),
        in_specs=[
            pl.BlockSpec((1, M, D_MODEL), lambda b: (b, 0, 0)),
            pl.BlockSpec((DEPTH, D_MODEL), lambda b: (0, 0)),
            pl.BlockSpec((DEPTH, D_MODEL, MEM_WIDTH), lambda b: (0, 0, 0)),
            pl.BlockSpec((DEPTH, D_MODEL, MEM_WIDTH), lambda b: (0, 0, 0)),
        ],
        out_specs=[
            pl.BlockSpec((DEPTH, 1, MEM_ROWS, MEM_HEAD_DIM), lambda b: (0, b, 0, 0)),
            pl.BlockSpec((DEPTH, 1, MEM_ROWS, MEM_HEAD_DIM), lambda b: (0, b, 0, 0)),
        ],
        out_shape=[out, out],
        compiler_params=_cparams(1),
        name="mem_project",
    )(mem, g_mem, wk, wv)


def _a_layer_kernel(x_ref, prev_ref, g_ref, win_ref, wgrp_ref, pscale_ref, mk_ref, mv_ref, wout_ref,
                    xo_ref, pool_ref, e_sc, sa_sc, sb_sc, *, tq, n_t, n_valid, pos0):
    t = pl.program_id(1)
    rows = HALO + tq

    @pl.when(t == 0)
    def _():
        zeros = jnp.zeros((SUBLANES, D_MODEL), F32)
        e_sc[0:SUBLANES, :] = zeros
        sa_sc[0:SUBLANES, :] = zeros
        sb_sc[0:SUBLANES, :] = zeros
        e_sc[SUBLANES:E_OFF, :] = prev_ref[0]

    x = x_ref[0]
    h = _rms(x, g_ref[...]).astype(BF16)
    e_sc[E_OFF:E_OFF + tq, :] = _dot(h, win_ref[:, 0:1024])

    lo = SUBLANES
    sa_sc[lo:lo + rows, :] = e_sc[lo:lo + rows, :] + e_sc[lo - 1:lo - 1 + rows, :]
    sb_sc[lo:lo + rows, 256:1024] = sa_sc[lo:lo + rows, 256:1024] + sa_sc[lo - 2:lo - 2 + rows, 256:1024]
    sa_sc[lo:lo + rows, 512:1024] = sb_sc[lo:lo + rows, 512:1024] + sb_sc[lo - 4:lo - 4 + rows, 512:1024]
    s16 = sa_sc[E_OFF:E_OFF + tq, 768:1024] + sa_sc[E_OFF - 8:E_OFF - 8 + tq, 768:1024]
    sums = (sa_sc[E_OFF:E_OFF + tq, 0:256], sb_sc[E_OFF:E_OFF + tq, 256:512],
            sa_sc[E_OFF:E_OFF + tq, 512:768], s16)

    pos = pos0 + t * tq + lax.broadcasted_iota(jnp.int32, (tq, 1), 0)
    mixed_tok = []
    for g, w in enumerate(POOL_WINDOWS):
        c0, c1 = g * POOL_GROUP, (g + 1) * POOL_GROUP
        inv_cnt = 1.0 / jnp.minimum(pos + 1, w).astype(F32)
        pooled = sums[g] * inv_cnt - e_sc[E_OFF:E_OFF + tq, c0:c1]
        mixed = _dot(pooled.astype(BF16), wgrp_ref[g]) * pscale_ref[:, c0:c1]
        gate = _dot(h, win_ref[:, 1024 + c0:1024 + c1])
        mixed_tok.append((mixed * _silu(gate)).astype(BF16))

    qm = _dot(h, win_ref[:, 2048:2560])
    gate_m = _dot(h, win_ref[:, 2560:3072])
    mem_o = _mem_attend(qm, mk_ref, mv_ref)

    acc = x
    for g in range(4):
        acc = acc + _dot(mixed_tok[g], wout_ref[g * 256:(g + 1) * 256, :])
    for hd in range(MEM_HEADS):
        c0, c1 = hd * MEM_HEAD_DIM, (hd + 1) * MEM_HEAD_DIM
        mm = (mem_o[hd] * _silu(gate_m[:, c0:c1])).astype(BF16)
        acc = acc + _dot(mm, wout_ref[1024 + c0:1024 + c1, :])
    xo_ref[0] = acc

    @pl.when(t == n_t - 1)
    def _():
        pool_ref[0] = e_sc[E_OFF + n_valid - HALO:E_OFF + n_valid, :]

    if n_t > 1:
        @pl.when(t < n_t - 1)
        def _():
            e_sc[SUBLANES:E_OFF, :] = e_sc[SUBLANES + tq:E_OFF + tq, :]


def _a_layer(x, prev, g, w_in, w_grp, pscale, mem_k, mem_v, w_out, *, layer, tq, n_valid, pos0):
    B, T, _ = x.shape
    n_t = T // tq
    rows = E_OFF + tq
    kern = functools.partial(_a_layer_kernel, tq=tq, n_t=n_t, n_valid=n_valid, pos0=pos0)
    return pl.pallas_call(
        kern,
        grid=(B, n_t),
        in_specs=[
            pl.BlockSpec((1, tq, D_MODEL), lambda b, t: (b, t, 0)),
            pl.BlockSpec((1, HALO, D_MODEL), lambda b, t: (b, 0, 0)),
            pl.BlockSpec((1, D_MODEL), lambda b, t: (0, 0)),
            pl.BlockSpec((D_MODEL, 3072), lambda b, t: (0, 0)),
            pl.BlockSpec((4, POOL_GROUP, POOL_GROUP), lambda b, t: (0, 0, 0)),
            pl.BlockSpec((1, D_MODEL), lambda b, t: (0, 0)),
            pl.BlockSpec((1, 1, MEM_ROWS, MEM_HEAD_DIM), lambda b, t: (layer, b, 0, 0)),
            pl.BlockSpec((1, 1, MEM_ROWS, MEM_HEAD_DIM), lambda b, t: (layer, b, 0, 0)),
            pl.BlockSpec((1536, D_MODEL), lambda b, t: (0, 0)),
        ],
        out_specs=[
            pl.BlockSpec((1, tq, D_MODEL), lambda b, t: (b, t, 0)),
            pl.BlockSpec((1, HALO, D_MODEL), lambda b, t: (b, 0, 0)),
        ],
        out_shape=[jax.ShapeDtypeStruct((B, T, D_MODEL), F32),
                   jax.ShapeDtypeStruct((B, HALO, D_MODEL), F32)],
        scratch_shapes=[pltpu.VMEM((rows, D_MODEL), F32)] * 3,
        compiler_params=_cparams(2),
        name=f"a_layer{layer}_{B}",
    )(x, prev, g, w_in, w_grp, pscale, mem_k, mem_v, w_out)


def _kv_latent_kernel(x_ref, gin_ref, wkv_ref, glat_ref, cos_ref, sin_ref, ckv_ref, kr_ref, kcat_ref,
                      *maybe_kt_ref):
    h = _rms(x_ref[0], gin_ref[...]).astype(BF16)
    kv = _dot(h, wkv_ref[...])
    ckv = _rms(kv[:, 0:KV_RANK], glat_ref[...])
    kr = _rope_lanes(kv[:, KV_RANK:KCAT], cos_ref[...], sin_ref[...])
    ckv_ref[0] = ckv
    kr_ref[0] = kr[:, 0:QK_ROPE]
    kcat_ref[0, :, 0:KV_RANK] = ckv.astype(BF16)
    kcat_ref[0, :, KV_RANK:KCAT] = kr.astype(BF16)
    for kt_ref in maybe_kt_ref:
        kt_ref[0, 0, 0:KV_RANK, :] = ckv.T.astype(BF16)
        kt_ref[0, 0, KV_RANK:KCAT, :] = kr.T.astype(BF16)


def _kv_latent(x, g_in, w_kv, g_lat, cos, sin, *, tq, emit_kt):
    B, T, _ = x.shape
    out_specs = [
        pl.BlockSpec((1, tq, KV_RANK), lambda b, t: (b, t, 0)),
        pl.BlockSpec((1, tq, QK_ROPE), lambda b, t: (b, t, 0)),
        pl.BlockSpec((1, tq, KCAT), lambda b, t: (b, t, 0)),
    ]
    out_shape = [jax.ShapeDtypeStruct((B, T, KV_RANK), F32),
                 jax.ShapeDtypeStruct((B, T, QK_ROPE), F32),
                 jax.ShapeDtypeStruct((B, T, KCAT), BF16)]
    if emit_kt:
        out_specs.append(pl.BlockSpec((1, 1, KCAT, tq), lambda b, t: (b, t, 0, 0)))
        out_shape.append(jax.ShapeDtypeStruct((B, T // tq, KCAT, tq), BF16))
    return pl.pallas_call(
        _kv_latent_kernel,
        grid=(B, T // tq),
        in_specs=[
            pl.BlockSpec((1, tq, D_MODEL), lambda b, t: (b, t, 0)),
            pl.BlockSpec((1, D_MODEL), lambda b, t: (0, 0)),
            pl.BlockSpec((D_MODEL, KCAT), lambda b, t: (0, 0)),
            pl.BlockSpec((1, KV_RANK), lambda b, t: (0, 0)),
            pl.BlockSpec((tq, LANES), lambda b, t: (t, 0)),
            pl.BlockSpec((tq, LANES), lambda b, t: (t, 0)),
        ],
        out_specs=out_specs,
        out_shape=out_shape,
        compiler_params=_cparams(2),
        name=f"kv_latent_{B}",
    )(x, g_in, w_kv, g_lat, cos, sin)


def _b_pre_kernel(x_ref, g_ref, wcq_ref, wrest_ref, gq_ref, wqn_ref, wqr_ref, wkup_ref, cos_ref, sin_ref,
                  q_ref, zrest_ref):
    h = _rms(x_ref[0], g_ref[...]).astype(BF16)
    zrest_ref[0] = _dot(h, wrest_ref[...])
    cn = _rms(_dot(h, wcq_ref[...]), gq_ref[...]).astype(BF16)
    qn = _dot(cn, wqn_ref[...])
    qr = _dot(cn, wqr_ref[...])
    for hd in range(MLA_HEADS):
        ql = _dot(qn[:, hd * QK_NOPE:(hd + 1) * QK_NOPE].astype(BF16), wkup_ref[hd])
        q_ref[0, hd, :, 0:KV_RANK] = (ql * (MLA_SCALE * LOG2E)).astype(BF16)
    cos = cos_ref[...]
    sin = sin_ref[...]
    lane = lax.broadcasted_iota(jnp.int32, cos.shape, 1)
    for j in range(MLA_HEADS // 2):
        rr = _rope_lanes(qr[:, j * LANES:(j + 1) * LANES], cos, sin) * (MLA_SCALE * LOG2E)
        q_ref[0, 2 * j, :, KV_RANK:KCAT] = jnp.where(lane < QK_ROPE, rr, 0.0).astype(BF16)
        q_ref[0, 2 * j + 1, :, KV_RANK:KCAT] = jnp.where(lane < QK_ROPE, pltpu.roll(rr, 64, 1), 0.0).astype(BF16)


def _b_pre(x, g, w_cq, w_rest, g_q, w_qn, w_qr, w_kup, cos, sin, *, tq, tag):
    B, T, _ = x.shape
    const2 = lambda b, t: (0, 0)
    return pl.pallas_call(
        _b_pre_kernel,
        grid=(B, T // tq),
        in_specs=[
            pl.BlockSpec((1, tq, D_MODEL), lambda b, t: (b, t, 0)),
            pl.BlockSpec((1, D_MODEL), const2),
            pl.BlockSpec((D_MODEL, Q_RANK), const2),
            pl.BlockSpec((D_MODEL, 2048), const2),
            pl.BlockSpec((1, Q_RANK), const2),
            pl.BlockSpec((Q_RANK, MLA_HEADS * QK_NOPE), const2),
            pl.BlockSpec((Q_RANK, MLA_HEADS * QK_ROPE), const2),
            pl.BlockSpec((MLA_HEADS, QK_NOPE, KV_RANK), lambda b, t: (0, 0, 0)),
            pl.BlockSpec((tq, LANES), lambda b, t: (t, 0)),
            pl.BlockSpec((tq, LANES), lambda b, t: (t, 0)),
        ],
        out_specs=[
            pl.BlockSpec((1, MLA_HEADS, tq, KCAT), lambda b, t: (b, 0, t, 0)),
            pl.BlockSpec((1, tq, 2048), lambda b, t: (b, t, 0)),
        ],
        out_shape=[jax.ShapeDtypeStruct((B, MLA_HEADS, T, KCAT), BF16),
                   jax.ShapeDtypeStruct((B, T, 2048), F32)],
        compiler_params=_cparams(2),
        name=f"b_pre_{tag}",
    )(x, g, w_cq, w_rest, g_q, w_qn, w_qr, w_kup, cos, sin)


def _flash_kernel(q_ref, k_ref, kt_ref, wv_ref, o_ref, m_sc, l_sc, acc_sc, *, tq, tk):
    i = pl.program_id(1)
    m_sc[...] = jnp.full(m_sc.shape, NEG, F32)
    l_sc[...] = jnp.zeros(l_sc.shape, F32)
    acc_sc[...] = jnp.zeros(acc_sc.shape, F32)

    def block(j, masked):
        kt = kt_ref[0, j]
        v = k_ref[0, pl.ds(pl.multiple_of(j * tk, tk), tk), 0:KV_RANK]
        if masked:
            keep = (lax.broadcasted_iota(jnp.int32, (tq, tk), 1)
                    <= lax.broadcasted_iota(jnp.int32, (tq, tk), 0))
        s_next = _dot(q_ref[0, 0], kt)
        for hd in range(MLA_HEADS):
            s = s_next
            if hd + 1 < MLA_HEADS:
                s_next = _dot(q_ref[0, hd + 1], kt)
            if masked:
                s = jnp.where(keep, s, NEG)
            m_prev = m_sc[hd]
            m_new = jnp.maximum(m_prev, jnp.max(s, axis=-1, keepdims=True))
            alpha = jnp.exp2(m_prev - m_new)
            p = jnp.exp2(s - _rep(m_new, tk // LANES))
            p_lanes = p[:, 0:LANES]
            for c in range(1, tk // LANES):
                p_lanes = p_lanes + p[:, c * LANES:(c + 1) * LANES]
            l_sc[hd] = alpha * l_sc[hd] + p_lanes
            acc_sc[hd] = _rep(alpha, KV_RANK // LANES) * acc_sc[hd] + _dot(p.astype(BF16), v)
            m_sc[hd] = m_new

    def body(j, carry):
        block(j, False)
        return carry

    lax.fori_loop(0, i, body, 0)
    block(i, True)

    for hd in range(MLA_HEADS):
        o = acc_sc[hd] * (1.0 / jnp.sum(l_sc[hd], axis=-1, keepdims=True))
        o_ref[0, :, hd * V_HEAD:(hd + 1) * V_HEAD] = _dot(o.astype(BF16), wv_ref[hd])


def _flash(q, kcat, kt, w_vup, *, tq):
    B, H, T, _ = q.shape
    tk = kt.shape[-1]
    assert tq == tk
    kern = functools.partial(_flash_kernel, tq=tq, tk=tk)
    return pl.pallas_call(
        kern,
        grid=(B, T // tq),
        in_specs=[
            pl.BlockSpec((1, H, tq, KCAT), lambda b, i: (b, 0, i, 0)),
            pl.BlockSpec((1, T, KCAT), lambda b, i: (b, 0, 0)),
            pl.BlockSpec((1, T // tk, KCAT, tk), lambda b, i: (b, 0, 0, 0)),
            pl.BlockSpec((H, KV_RANK, V_HEAD), lambda b, i: (0, 0, 0)),
        ],
        out_specs=pl.BlockSpec((1, tq, MLA_WIDTH), lambda b, i: (b, i, 0)),
        out_shape=jax.ShapeDtypeStruct((B, T, MLA_WIDTH), F32),
        scratch_shapes=[pltpu.VMEM((H, tq, LANES), F32), pltpu.VMEM((H, tq, LANES), F32),
                        pltpu.VMEM((H, tq, KV_RANK), F32)],
        compiler_params=_cparams(2),
        name="flash_prompt",
    )(q, kcat, kt, w_vup)


def _paged_kernel(pt_ref, q_ref, knew_ref, ckv_hbm, krt_hbm, o_ref,
                  ckv_buf, krt_buf, kbf_sc, s_sc, sem, m_sc, l_sc, acc_sc,
                  *, n_b, n_ch, cp, page, tp, n_new, sub):
    total = n_b * n_ch
    n_sub = (cp * page) // sub

    def copies(bc):
        slot = bc % N_RAW
        b = bc // n_ch
        c = bc % n_ch
        out = []
        for p in range(cp):
            pg = pt_ref[b, c * cp + p]
            out.append(pltpu.make_async_copy(ckv_hbm.at[pg], ckv_buf.at[slot, pl.ds(p * page, page)],
                                             sem.at[0, slot]))
            out.append(pltpu.make_async_copy(krt_hbm.at[pg], krt_buf.at[slot, :, pl.ds(p * page, page)],
                                             sem.at[1, slot]))
        return out

    def score(bc, w):
        slot = bc % N_RAW
        b = bc // n_ch
        for cpy in copies(bc):
            cpy.wait()
        q_lat = q_ref[b, :, 0:KV_RANK]
        q_rope = q_ref[b, :, KV_RANK:KV_RANK + QK_ROPE]
        for j in range(n_sub):
            cols = slice(j * sub, (j + 1) * sub)
            kc = ckv_buf[slot, cols, :].astype(BF16)
            krt = krt_buf[slot, :, cols].astype(BF16)
            kbf_sc[w, cols, :] = kc
            s_sc[w, :, cols] = _dot_nt(q_lat, kc) + _dot(q_rope, krt)

    def update(s, v_bf16, first):
        m_prev = jnp.where(first, NEG, m_sc[...])
        l_prev = jnp.where(first, 0.0, l_sc[...])
        acc_prev = jnp.where(first, 0.0, acc_sc[...])
        m_new = jnp.maximum(m_prev, jnp.max(s, axis=-1, keepdims=True))
        alpha = jnp.exp2(m_prev - m_new)
        p = jnp.exp2(s - _rep(m_new, s.shape[1] // LANES))
        l_sc[...] = alpha * l_prev + jnp.sum(p, axis=-1, keepdims=True)
        acc_sc[...] = _rep(alpha, KV_RANK // LANES) * acc_prev + _dot(p.astype(BF16), v_bf16)
        m_sc[...] = m_new

    def attend(bc, w):
        b = bc // n_ch
        c = bc % n_ch
        update(s_sc[w], kbf_sc[w], c == 0)

        def finish():
            kn = knew_ref[b]
            sn = _dot_nt(q_ref[b], kn)
            tpos = lax.broadcasted_iota(jnp.int32, sn.shape, 0) & (tp - 1)
            col = lax.broadcasted_iota(jnp.int32, sn.shape, 1)
            sn = jnp.where((col <= tpos) & (col < n_new), sn, NEG)
            update(sn, kn[:, 0:KV_RANK], False)
            o_ref[b] = acc_sc[...] * _rep(1.0 / l_sc[...], KV_RANK // LANES)

        last = c == n_ch - 1
        if isinstance(last, bool):
            if last:
                finish()
        else:
            pl.when(last)(finish)

    m_sc[...] = jnp.full(m_sc.shape, NEG, F32)
    l_sc[...] = jnp.zeros(l_sc.shape, F32)
    acc_sc[...] = jnp.zeros(acc_sc.shape, F32)
    for bc in range(min(N_RAW, total)):
        for cpy in copies(bc):
            cpy.start()
    score(0, 0)

    def step(i, w):
        def prefetch():
            for cpy in copies(i + N_RAW):
                cpy.start()

        more = i + N_RAW < total
        if isinstance(more, bool):
            if more:
                prefetch()
        else:
            pl.when(more)(prefetch)
        if isinstance(i, int) and i + 1 >= total:
            attend(i, w)
            return
        score(i + 1, 1 - w)
        attend(i, w)

    def body(k, carry):
        step(2 * k, 0)
        step(2 * k + 1, 1)
        return carry

    n_pairs = (total - 2) // 2
    lax.fori_loop(0, n_pairs, body, 0)
    for i in range(2 * n_pairs, total):
        step(i, i % 2)


def _paged_attention(page_table, q, knew, cache_ckv, cache_krt, *, cp, tp, n_new, sub):
    n_b, n_pages = page_table.shape
    page = cache_ckv.shape[1]
    rows = q.shape[1]
    n_ch = n_pages // cp
    assert (cp * page) % sub == 0
    kern = functools.partial(_paged_kernel, n_b=n_b, n_ch=n_ch, cp=cp, page=page, tp=tp, n_new=n_new, sub=sub)
    grid_spec = pltpu.PrefetchScalarGridSpec(
        num_scalar_prefetch=1,
        grid=(1,),
        in_specs=[
            pl.BlockSpec(q.shape, lambda i, pt: (0, 0, 0)),
            pl.BlockSpec(knew.shape, lambda i, pt: (0, 0, 0)),
            pl.BlockSpec(memory_space=pl.ANY),
            pl.BlockSpec(memory_space=pl.ANY),
        ],
        out_specs=pl.BlockSpec((n_b, rows, KV_RANK), lambda i, pt: (0, 0, 0)),
        scratch_shapes=[
            pltpu.VMEM((N_RAW, cp * page, KV_RANK), F32),
            pltpu.VMEM((N_RAW, QK_ROPE, cp * page), F32),
            pltpu.VMEM((2, cp * page, KV_RANK), BF16),
            pltpu.VMEM((2, rows, cp * page), F32),
            pltpu.SemaphoreType.DMA((2, N_RAW)),
            pltpu.VMEM((rows, LANES), F32),
            pltpu.VMEM((rows, LANES), F32),
            pltpu.VMEM((rows, KV_RANK), F32),
        ],
    )
    return pl.pallas_call(
        kern,
        grid_spec=grid_spec,
        out_shape=jax.ShapeDtypeStruct((n_b, rows, KV_RANK), F32),
        compiler_params=_cparams(1),
        name="paged_attention",
    )(page_table, q, knew, cache_ckv, cache_krt)


def _v_up_kernel(o_ref, wv_ref, tok_ref):
    n_b, _, tp, _ = o_ref.shape
    for hd in range(MLA_HEADS):
        oh = o_ref[:, hd].reshape(n_b * tp, KV_RANK).astype(BF16)
        tok_ref[:, :, hd * V_HEAD:(hd + 1) * V_HEAD] = _dot(oh, wv_ref[hd]).reshape(n_b, tp, V_HEAD)


def _v_up(o, w_vup):
    n_b, H, tp, _ = o.shape
    return pl.pallas_call(
        _v_up_kernel,
        out_shape=jax.ShapeDtypeStruct((n_b, tp, MLA_WIDTH), F32),
        compiler_params=pltpu.CompilerParams(vmem_limit_bytes=VMEM_LIMIT),
        name="v_up_sample",
    )(o, w_vup)


def _b_post_kernel(x_ref, tok_ref, z_ref, mk_ref, mv_ref, wout_ref, gf_ref, xo_ref, *, final):
    x = x_ref[0]
    acc = x
    for g in range(4):
        c0, c1 = g * 256, (g + 1) * 256
        mt = (tok_ref[0, :, c0:c1] * _silu(z_ref[0, :, c0:c1])).astype(BF16)
        acc = acc + _dot(mt, wout_ref[c0:c1, :])
    mem_o = _mem_attend(z_ref[0, :, 1024:1536], mk_ref, mv_ref)
    for hd in range(MEM_HEADS):
        c0, c1 = hd * MEM_HEAD_DIM, (hd + 1) * MEM_HEAD_DIM
        mm = (mem_o[hd] * _silu(z_ref[0, :, 1536 + c0:1536 + c1])).astype(BF16)
        acc = acc + _dot(mm, wout_ref[1024 + c0:1024 + c1, :])
    if final:
        acc = _rms(acc, gf_ref[...])
    xo_ref[0] = acc


def _b_post(x, tok, zrest, mem_k, mem_v, w_out, g_final, *, layer, tq, final, tag):
    B, T, _ = x.shape
    kern = functools.partial(_b_post_kernel, final=final)
    return pl.pallas_call(
        kern,
        grid=(B, T // tq),
        in_specs=[
            pl.BlockSpec((1, tq, D_MODEL), lambda b, t: (b, t, 0)),
            pl.BlockSpec((1, tq, MLA_WIDTH), lambda b, t: (b, t, 0)),
            pl.BlockSpec((1, tq, 2048), lambda b, t: (b, t, 0)),
            pl.BlockSpec((1, 1, MEM_ROWS, MEM_HEAD_DIM), lambda b, t: (layer, b, 0, 0)),
            pl.BlockSpec((1, 1, MEM_ROWS, MEM_HEAD_DIM), lambda b, t: (layer, b, 0, 0)),
            pl.BlockSpec((1536, D_MODEL), lambda b, t: (0, 0)),
            pl.BlockSpec((1, D_MODEL), lambda b, t: (0, 0)),
        ],
        out_specs=pl.BlockSpec((1, tq, D_MODEL), lambda b, t: (b, t, 0)),
        out_shape=jax.ShapeDtypeStruct((B, T, D_MODEL), F32),
        compiler_params=_cparams(2),
        name=f"b_post{layer}_{tag}",
    )(x, tok, zrest, mem_k, mem_v, w_out, g_final)


def _rope_tables(pos):
    half = QK_ROPE // 2
    inv = ROPE_THETA ** (-jnp.arange(half, dtype=F32) / half)
    ang = pos.astype(F32)[:, None] * inv[None, :]
    return jnp.tile(jnp.cos(ang), (1, LANES // half)), jnp.tile(jnp.sin(ang), (1, LANES // half))


def kernel(x_prompt, x_sample, state_pool, cache_ckv, cache_krope, cache_mem_k, cache_mem_v, page_table, mem_prompt, g_norm, w_in_a, w_pool_grp, pool_scale, w_in_b, g_q_latent, w_q_up, g_kv_in, w_kv_down, g_kv_latent, w_k_up, w_v_up, g_mem, w_mem_k, w_mem_v, w_out, g_final):
    B, T, _ = x_prompt.shape
    SB, ST, _ = x_sample.shape
    TP = SUBLANES
    past = page_table.shape[1] * cache_ckv.shape[1]

    w_in_a16 = w_in_a.astype(BF16)
    w_grp16 = w_pool_grp.astype(BF16)
    w_out16 = w_out.astype(BF16)
    w_cq16 = w_in_b[:, :, :Q_RANK].astype(BF16)
    w_rest16 = w_in_b[:, :, Q_RANK:].astype(BF16)
    wq = w_q_up.reshape(N_A, Q_RANK, MLA_HEADS, QK_NOPE + QK_ROPE)
    w_qn16 = wq[..., :QK_NOPE].reshape(N_A, Q_RANK, MLA_HEADS * QK_NOPE).astype(BF16)
    w_qr16 = wq[..., QK_NOPE:].reshape(N_A, Q_RANK, MLA_HEADS * QK_ROPE).astype(BF16)
    w_kup16 = jnp.transpose(w_k_up, (1, 2, 0)).astype(BF16)
    w_vup16 = jnp.transpose(w_v_up, (1, 0, 2)).astype(BF16)
    w_kv16 = jnp.pad(w_kv_down, ((0, 0), (0, KCAT - KV_RANK - QK_ROPE))).astype(BF16)
    w_mk16 = w_mem_k.astype(BF16)
    w_mv16 = w_mem_v.astype(BF16)
    g_kv_in2 = g_kv_in[None]
    g_lat2 = g_kv_latent[None]
    g_final2 = g_final[None]

    def trunk(x, prev, mem_k, mem_v, pos, *, tag, tq, tq_kv, emit_kt, n_valid, attend):
        pools = []
        for l in range(N_A):
            x, pool = _a_layer(x, prev[l], g_norm[l][None], w_in_a16[l], w_grp16[l], pool_scale[l][None],
                               mem_k, mem_v, w_out16[l], layer=l, tq=tq, n_valid=n_valid, pos0=pos)
            pools.append(pool[:, 1:])
        cos, sin = _rope_tables(pos + jnp.arange(x.shape[1], dtype=jnp.int32))
        ckv, krope, *kcat = _kv_latent(x, g_kv_in2, w_kv16, g_lat2, cos, sin, tq=tq_kv, emit_kt=emit_kt)
        for j in range(DEPTH - N_A):
            l = N_A + j
            q, zrest = _b_pre(x, g_norm[l][None], w_cq16[j], w_rest16[j], g_q_latent[j][None],
                              w_qn16[j], w_qr16[j], w_kup16, cos, sin, tq=tq, tag=f"{tag}{l}")
            tok = attend(q, *kcat)
            x = _b_post(x, tok, zrest, mem_k, mem_v, w_out16[l], g_final2, layer=l, tq=tq,
                        final=(l == DEPTH - 1), tag=tag)
        return x, jnp.stack(pools, axis=0), ckv, krope

    mem_k_p, mem_v_p = _mem_project(mem_prompt, g_mem, w_mk16, w_mv16)
    prev_p = jnp.zeros((N_A, B, HALO, D_MODEL), F32)
    y_p, pool_p, ckv_p, krope_p = trunk(
        x_prompt, prev_p, mem_k_p, mem_v_p, 0, tag="p", tq=256, tq_kv=256, emit_kt=True, n_valid=256,
        attend=lambda q, kcat, kt: _flash(q, kcat, kt, w_vup16, tq=256))

    xs = jnp.pad(x_sample, ((0, 0), (0, TP - ST), (0, 0)))
    prev_s = jnp.pad(state_pool, ((0, 0), (0, 0), (HALO - POOL_BUF, 0), (0, 0)))
    mem_k_s = cache_mem_k.reshape(DEPTH, SB, MEM_ROWS, MEM_HEAD_DIM)
    mem_v_s = cache_mem_v.reshape(DEPTH, SB, MEM_ROWS, MEM_HEAD_DIM)
    cache_krt = jnp.transpose(cache_krope, (0, 2, 1))

    def attend_sample(q, kcat):
        qf = q.reshape(SB, MLA_HEADS * TP, KCAT)
        knew = jnp.pad(kcat, ((0, 0), (0, LANES - TP), (0, 0)))
        o = _paged_attention(page_table, qf, knew, cache_ckv, cache_krt, cp=8, tp=TP, n_new=ST, sub=1024)
        return _v_up(o.reshape(SB, MLA_HEADS, TP, KV_RANK), w_vup16)

    y_s, pool_s, ckv_s, krope_s = trunk(
        xs, prev_s, mem_k_s, mem_v_s, past, tag="s", tq=TP, tq_kv=TP, emit_kt=False, n_valid=ST,
        attend=attend_sample)

    mem_shape = (DEPTH, B, MEM_TOKENS, MEM_HEADS, MEM_HEAD_DIM)
    return (y_p, y_s[:, :ST], pool_p, pool_s, ckv_p, krope_p, ckv_s[:, :ST], krope_s[:, :ST],
            mem_k_p.reshape(mem_shape), mem_v_p.reshape(mem_shape))
```

```python
import functools

import jax
import jax.numpy as jnp
from jax import lax
from jax.experimental import pallas as pl
from jax.experimental.pallas import tpu as pltpu

F32 = jnp.float32
BF16 = jnp.bfloat16

D_MODEL = 1024
DEPTH = 4
N_A = 2
POOL_WINDOWS = (2, 4, 8, 16)
POOL_GROUP = 256
POOL_BUF = 15
MLA_HEADS = 8
QK_NOPE = 128
QK_ROPE = 64
V_HEAD = 128
KV_RANK = 256
Q_RANK = 384
MLA_WIDTH = MLA_HEADS * V_HEAD
MLA_SCALE = (QK_NOPE + QK_ROPE) ** -0.5
ROPE_THETA = 10000.0
MEM_TOKENS = 256
MEM_HEADS = 4
MEM_HEAD_DIM = 128
MEM_WIDTH = MEM_HEADS * MEM_HEAD_DIM
MEM_SCALE = MEM_HEAD_DIM ** -0.5
EPS = 1e-6
NEG = -1e30
LOG2E = 1.4426950408889634

KCAT = 384
SUBLANES = 8
LANES = 128
HALO = 16
E_OFF = SUBLANES + HALO
VMEM_LIMIT = 56 * 1024 * 1024
MEM_ROWS = MEM_TOKENS * MEM_HEADS
N_RAW = 3


def _cparams(n_axes):
    return pltpu.CompilerParams(dimension_semantics=("arbitrary",) * n_axes,
                                vmem_limit_bytes=VMEM_LIMIT)


def _rms(x, g):
    return x * lax.rsqrt(jnp.mean(x * x, axis=-1, keepdims=True) + EPS) * g


def _dot(a, b):
    return jnp.dot(a, b, preferred_element_type=F32)


def _dot_nt(a, b):
    return lax.dot_general(a, b, (((1,), (1,)), ((), ())), preferred_element_type=F32)


def _silu(x):
    return x * jax.nn.sigmoid(x)


def _rope_lanes(a, cos, sin):
    lane = lax.broadcasted_iota(jnp.int32, a.shape, 1)
    up = pltpu.roll(a, 32, 1)
    dn = pltpu.roll(a, 96, 1)
    rot = jnp.where((lane & 63) < 32, -dn, up)
    return a * cos + rot * sin


def _rep(x, n):
    return x if n == 1 else jnp.concatenate([x] * n, axis=1)


def _tile_rows(x, n):
    return x if n == 1 else jnp.concatenate([x] * n, axis=0)


def _mem_attend(qm, mk_ref, mv_ref, nb, tq):
    rows, keys = nb * tq, nb * MEM_TOKENS
    if nb > 1:
        row_seq = lax.broadcasted_iota(jnp.int32, (rows, keys), 0) // tq
        col_seq = lax.broadcasted_iota(jnp.int32, (rows, keys), 1) // MEM_TOKENS
        keep = row_seq == col_seq
    outs = []
    for h in range(MEM_HEADS):
        c0, c1 = h * MEM_HEAD_DIM, (h + 1) * MEM_HEAD_DIM
        q = (qm[:, c0:c1] * (MEM_SCALE * LOG2E)).astype(BF16)
        k = mk_ref[0, :, pl.ds(h, MEM_TOKENS, stride=MEM_HEADS), :].reshape(keys, MEM_HEAD_DIM).astype(BF16)
        v = mv_ref[0, :, pl.ds(h, MEM_TOKENS, stride=MEM_HEADS), :].reshape(keys, MEM_HEAD_DIM).astype(BF16)
        s = _dot_nt(q, k)
        if nb > 1:
            s = jnp.where(keep, s, NEG)
        m = jnp.max(s, axis=-1, keepdims=True)
        p = jnp.exp2(s - m)
        l = jnp.sum(p, axis=-1, keepdims=True)
        outs.append(_dot(p.astype(BF16), v) * (1.0 / l))
    return outs


def _mem_proj_kernel(mem_ref, g_ref, wk_ref, wv_ref, mk_ref, mv_ref):
    x = mem_ref[0]
    xn = x * lax.rsqrt(jnp.mean(x * x, axis=-1, keepdims=True) + EPS)
    for l in range(DEPTH):
        h = (xn * g_ref[l:l + 1, :]).astype(BF16)
        mk = _dot(h, wk_ref[l])
        mv = _dot(h, wv_ref[l])
        for hd in range(MEM_HEADS):
            c0, c1 = hd * MEM_HEAD_DIM, (hd + 1) * MEM_HEAD_DIM
            mk_ref[l, 0, pl.ds(hd, MEM_TOKENS, stride=MEM_HEADS), :] = mk[:, c0:c1]
            mv_ref[l, 0, pl.ds(hd, MEM_TOKENS, stride=MEM_HEADS), :] = mv[:, c0:c1]


def _mem_project(mem, g_mem, wk, wv):
    B, M, _ = mem.shape
    out = jax.ShapeDtypeStruct((DEPTH, B, MEM_ROWS, MEM_HEAD_DIM), F32)
    return pl.pallas_call(
        _mem_proj_kernel,
        grid=(B,),
        in_specs=[
            pl.BlockSpec((1, M, D_MODEL), lambda b: (b, 0, 0)),
            pl.BlockSpec((DEPTH, D_MODEL), lambda b: (0, 0)),
            pl.BlockSpec((DEPTH, D_MODEL, MEM_WIDTH), lambda b: (0, 0, 0)),
            pl.BlockSpec((DEPTH, D_MODEL, MEM_WIDTH), lambda b: (0, 0, 0)),
        ],
        out_specs=[
            pl.BlockSpec((DEPTH, 1, MEM_ROWS, MEM_HEAD_DIM), lambda b: (0, b, 0, 0)),
            pl.BlockSpec((DEPTH, 1, MEM_ROWS, MEM_HEAD_DIM), lambda b: (0, b, 0, 0)),
        ],
        out_shape=[out, out],
        compiler_params=_cparams(1),
        name="mem_project",
    )(mem, g_mem, wk, wv)


def _a_layer_kernel(x_ref, prev_ref, g_ref, win_ref, wgrp_ref, pscale_ref, mk_ref, mv_ref, wout_ref,
                    xo_ref, pool_ref, e_sc, sa_sc, sb_sc, *, nb, tq, n_t, n_valid, pos0):
    t = pl.program_id(1)
    rows = HALO + tq
    m = nb * tq

    @pl.when(t == 0)
    def _():
        zeros = jnp.zeros((nb, SUBLANES, D_MODEL), F32)
        e_sc[:, 0:SUBLANES, :] = zeros
        sa_sc[:, 0:SUBLANES, :] = zeros
        sb_sc[:, 0:SUBLANES, :] = zeros
        e_sc[:, SUBLANES:E_OFF, :] = prev_ref[...]

    x = x_ref[...].reshape(m, D_MODEL)
    h = _rms(x, g_ref[...]).astype(BF16)
    e_sc[:, E_OFF:E_OFF + tq, :] = _dot(h, win_ref[:, 0:1024]).reshape(nb, tq, D_MODEL)

    lo = SUBLANES
    sa_sc[:, lo:lo + rows, :] = e_sc[:, lo:lo + rows, :] + e_sc[:, lo - 1:lo - 1 + rows, :]
    sb_sc[:, lo:lo + rows, 256:1024] = (sa_sc[:, lo:lo + rows, 256:1024]
                                        + sa_sc[:, lo - 2:lo - 2 + rows, 256:1024])
    sa_sc[:, lo:lo + rows, 512:1024] = (sb_sc[:, lo:lo + rows, 512:1024]
                                        + sb_sc[:, lo - 4:lo - 4 + rows, 512:1024])
    s16 = sa_sc[:, E_OFF:E_OFF + tq, 768:1024] + sa_sc[:, E_OFF - 8:E_OFF - 8 + tq, 768:1024]
    sums = (sa_sc[:, E_OFF:E_OFF + tq, 0:256], sb_sc[:, E_OFF:E_OFF + tq, 256:512],
            sa_sc[:, E_OFF:E_OFF + tq, 512:768], s16)

    pos = pos0 + t * tq + lax.broadcasted_iota(jnp.int32, (1, tq, 1), 1)
    mixed_tok = []
    for g, w in enumerate(POOL_WINDOWS):
        c0, c1 = g * POOL_GROUP, (g + 1) * POOL_GROUP
        inv_cnt = 1.0 / jnp.minimum(pos + 1, w).astype(F32)
        pooled = (sums[g] * inv_cnt - e_sc[:, E_OFF:E_OFF + tq, c0:c1]).reshape(m, POOL_GROUP)
        mixed = _dot(pooled.astype(BF16), wgrp_ref[g]) * pscale_ref[:, c0:c1]
        gate = _dot(h, win_ref[:, 1024 + c0:1024 + c1])
        mixed_tok.append((mixed * _silu(gate)).astype(BF16))

    qm = _dot(h, win_ref[:, 2048:2560])
    gate_m = _dot(h, win_ref[:, 2560:3072])
    mem_o = _mem_attend(qm, mk_ref, mv_ref, nb, tq)

    acc = x
    for g in range(4):
        acc = acc + _dot(mixed_tok[g], wout_ref[g * 256:(g + 1) * 256, :])
    for hd in range(MEM_HEADS):
        c0, c1 = hd * MEM_HEAD_DIM, (hd + 1) * MEM_HEAD_DIM
        mm = (mem_o[hd] * _silu(gate_m[:, c0:c1])).astype(BF16)
        acc = acc + _dot(mm, wout_ref[1024 + c0:1024 + c1, :])
    xo_ref[...] = acc.reshape(nb, tq, D_MODEL)

    @pl.when(t == n_t - 1)
    def _():
        pool_ref[...] = e_sc[:, E_OFF + n_valid - HALO:E_OFF + n_valid, :]

    if n_t > 1:
        @pl.when(t < n_t - 1)
        def _():
            e_sc[:, SUBLANES:E_OFF, :] = e_sc[:, SUBLANES + tq:E_OFF + tq, :]


def _a_layer(x, prev, g, w_in, w_grp, pscale, mem_k, mem_v, w_out, *, layer, nb, tq, n_valid, pos0):
    B, T, _ = x.shape
    n_t = T // tq
    rows = E_OFF + tq
    kern = functools.partial(_a_layer_kernel, nb=nb, tq=tq, n_t=n_t, n_valid=n_valid, pos0=pos0)
    return pl.pallas_call(
        kern,
        grid=(B // nb, n_t),
        in_specs=[
            pl.BlockSpec((nb, tq, D_MODEL), lambda b, t: (b, t, 0)),
            pl.BlockSpec((nb, HALO, D_MODEL), lambda b, t: (b, 0, 0)),
            pl.BlockSpec((1, D_MODEL), lambda b, t: (0, 0)),
            pl.BlockSpec((D_MODEL, 3072), lambda b, t: (0, 0)),
            pl.BlockSpec((4, POOL_GROUP, POOL_GROUP), lambda b, t: (0, 0, 0)),
            pl.BlockSpec((1, D_MODEL), lambda b, t: (0, 0)),
            pl.BlockSpec((1, nb, MEM_ROWS, MEM_HEAD_DIM), lambda b, t: (layer, b, 0, 0)),
            pl.BlockSpec((1, nb, MEM_ROWS, MEM_HEAD_DIM), lambda b, t: (layer, b, 0, 0)),
            pl.BlockSpec((1536, D_MODEL), lambda b, t: (0, 0)),
        ],
        out_specs=[
            pl.BlockSpec((nb, tq, D_MODEL), lambda b, t: (b, t, 0)),
            pl.BlockSpec((nb, HALO, D_MODEL), lambda b, t: (b, 0, 0)),
        ],
        out_shape=[jax.ShapeDtypeStruct((B, T, D_MODEL), F32),
                   jax.ShapeDtypeStruct((B, HALO, D_MODEL), F32)],
        scratch_shapes=[pltpu.VMEM((nb, rows, D_MODEL), F32)] * 3,
        compiler_params=_cparams(2),
        name=f"a_layer{layer}_{B}",
    )(x, prev, g, w_in, w_grp, pscale, mem_k, mem_v, w_out)


def _kv_latent_kernel(x_ref, gin_ref, wkv_ref, glat_ref, cos_ref, sin_ref, ckv_ref, kr_ref, kcat_ref,
                      *maybe_kt_ref):
    nb, tq, _ = x_ref.shape
    m = nb * tq
    h = _rms(x_ref[...].reshape(m, D_MODEL), gin_ref[...]).astype(BF16)
    kv = _dot(h, wkv_ref[...])
    ckv = _rms(kv[:, 0:KV_RANK], glat_ref[...])
    kr = _rope_lanes(kv[:, KV_RANK:KCAT], _tile_rows(cos_ref[...], nb), _tile_rows(sin_ref[...], nb))
    ckv_ref[...] = ckv.reshape(nb, tq, KV_RANK)
    kr_ref[...] = kr[:, 0:QK_ROPE].reshape(nb, tq, QK_ROPE)
    kcat_ref[:, :, 0:KV_RANK] = ckv.reshape(nb, tq, KV_RANK).astype(BF16)
    kcat_ref[:, :, KV_RANK:KCAT] = kr.reshape(nb, tq, LANES).astype(BF16)
    for kt_ref in maybe_kt_ref:
        kt_ref[0, 0, 0:KV_RANK, :] = ckv.T.astype(BF16)
        kt_ref[0, 0, KV_RANK:KCAT, :] = kr.T.astype(BF16)


def _kv_latent(x, g_in, w_kv, g_lat, cos, sin, *, nb, tq, emit_kt):
    B, T, _ = x.shape
    assert nb == 1 or not emit_kt
    out_specs = [
        pl.BlockSpec((nb, tq, KV_RANK), lambda b, t: (b, t, 0)),
        pl.BlockSpec((nb, tq, QK_ROPE), lambda b, t: (b, t, 0)),
        pl.BlockSpec((nb, tq, KCAT), lambda b, t: (b, t, 0)),
    ]
    out_shape = [jax.ShapeDtypeStruct((B, T, KV_RANK), F32),
                 jax.ShapeDtypeStruct((B, T, QK_ROPE), F32),
                 jax.ShapeDtypeStruct((B, T, KCAT), BF16)]
    if emit_kt:
        out_specs.append(pl.BlockSpec((1, 1, KCAT, tq), lambda b, t: (b, t, 0, 0)))
        out_shape.append(jax.ShapeDtypeStruct((B, T // tq, KCAT, tq), BF16))
    return pl.pallas_call(
        _kv_latent_kernel,
        grid=(B // nb, T // tq),
        in_specs=[
            pl.BlockSpec((nb, tq, D_MODEL), lambda b, t: (b, t, 0)),
            pl.BlockSpec((1, D_MODEL), lambda b, t: (0, 0)),
            pl.BlockSpec((D_MODEL, KCAT), lambda b, t: (0, 0)),
            pl.BlockSpec((1, KV_RANK), lambda b, t: (0, 0)),
            pl.BlockSpec((tq, LANES), lambda b, t: (t, 0)),
            pl.BlockSpec((tq, LANES), lambda b, t: (t, 0)),
        ],
        out_specs=out_specs,
        out_shape=out_shape,
        compiler_params=_cparams(2),
        name=f"kv_latent_{B}",
    )(x, g_in, w_kv, g_lat, cos, sin)


def _b_pre_kernel(x_ref, g_ref, wcq_ref, wrest_ref, gq_ref, wqn_ref, wqr_ref, wkup_ref, cos_ref, sin_ref,
                  q_ref, zrest_ref):
    nb, tq, _ = x_ref.shape
    m = nb * tq
    h = _rms(x_ref[...].reshape(m, D_MODEL), g_ref[...]).astype(BF16)
    zrest_ref[...] = _dot(h, wrest_ref[...]).reshape(nb, tq, 2048)
    cn = _rms(_dot(h, wcq_ref[...]), gq_ref[...]).astype(BF16)
    qn = _dot(cn, wqn_ref[...])
    qr = _dot(cn, wqr_ref[...])
    for hd in range(MLA_HEADS):
        ql = _dot(qn[:, hd * QK_NOPE:(hd + 1) * QK_NOPE].astype(BF16), wkup_ref[hd])
        q_ref[:, hd, :, 0:KV_RANK] = (ql * (MLA_SCALE * LOG2E)).reshape(nb, tq, KV_RANK).astype(BF16)
    cos = _tile_rows(cos_ref[...], nb)
    sin = _tile_rows(sin_ref[...], nb)
    lane = lax.broadcasted_iota(jnp.int32, cos.shape, 1)
    for j in range(MLA_HEADS // 2):
        rr = _rope_lanes(qr[:, j * LANES:(j + 1) * LANES], cos, sin) * (MLA_SCALE * LOG2E)
        even = jnp.where(lane < QK_ROPE, rr, 0.0)
        odd = jnp.where(lane < QK_ROPE, pltpu.roll(rr, 64, 1), 0.0)
        q_ref[:, 2 * j, :, KV_RANK:KCAT] = even.reshape(nb, tq, LANES).astype(BF16)
        q_ref[:, 2 * j + 1, :, KV_RANK:KCAT] = odd.reshape(nb, tq, LANES).astype(BF16)


def _b_pre(x, g, w_cq, w_rest, g_q, w_qn, w_qr, w_kup, cos, sin, *, nb, tq, tag):
    B, T, _ = x.shape
    const2 = lambda b, t: (0, 0)
    return pl.pallas_call(
        _b_pre_kernel,
        grid=(B // nb, T // tq),
        in_specs=[
            pl.BlockSpec((nb, tq, D_MODEL), lambda b, t: (b, t, 0)),
            pl.BlockSpec((1, D_MODEL), const2),
            pl.BlockSpec((D_MODEL, Q_RANK), const2),
            pl.BlockSpec((D_MODEL, 2048), const2),
            pl.BlockSpec((1, Q_RANK), const2),
            pl.BlockSpec((Q_RANK, MLA_HEADS * QK_NOPE), const2),
            pl.BlockSpec((Q_RANK, MLA_HEADS * QK_ROPE), const2),
            pl.BlockSpec((MLA_HEADS, QK_NOPE, KV_RANK), lambda b, t: (0, 0, 0)),
            pl.BlockSpec((tq, LANES), lambda b, t: (t, 0)),
            pl.BlockSpec((tq, LANES), lambda b, t: (t, 0)),
        ],
        out_specs=[
            pl.BlockSpec((nb, MLA_HEADS, tq, KCAT), lambda b, t: (b, 0, t, 0)),
            pl.BlockSpec((nb, tq, 2048), lambda b, t: (b, t, 0)),
        ],
        out_shape=[jax.ShapeDtypeStruct((B, MLA_HEADS, T, KCAT), BF16),
                   jax.ShapeDtypeStruct((B, T, 2048), F32)],
        compiler_params=_cparams(2),
        name=f"b_pre_{tag}",
    )(x, g, w_cq, w_rest, g_q, w_qn, w_qr, w_kup, cos, sin)


def _flash_kernel(q_ref, k_ref, kt_ref, wv_ref, o_ref, m_sc, l_sc, acc_sc, *, tq, tk, hg):
    i = pl.program_id(1)
    n_grp = MLA_HEADS // hg
    rows = hg * tq
    m_sc[...] = jnp.full(m_sc.shape, NEG, F32)
    l_sc[...] = jnp.zeros(l_sc.shape, F32)
    acc_sc[...] = jnp.zeros(acc_sc.shape, F32)

    def q_group(g):
        return q_ref[0, g * hg:(g + 1) * hg].reshape(rows, KCAT)

    def block(j, masked):
        kt = kt_ref[0, j]
        v = k_ref[0, pl.ds(pl.multiple_of(j * tk, tk), tk), 0:KV_RANK]
        if masked:
            keep = (lax.broadcasted_iota(jnp.int32, (rows, tk), 1)
                    <= (lax.broadcasted_iota(jnp.int32, (rows, tk), 0) & (tq - 1)))
        s_next = _dot(q_group(0), kt)
        for g in range(n_grp):
            s = s_next
            if g + 1 < n_grp:
                s_next = _dot(q_group(g + 1), kt)
            if masked:
                s = jnp.where(keep, s, NEG)
            m_prev = m_sc[g]
            m_new = jnp.maximum(m_prev, jnp.max(s, axis=-1, keepdims=True))
            alpha = jnp.exp2(m_prev - m_new)
            p = jnp.exp2(s - _rep(m_new, tk // LANES))
            p_lanes = p[:, 0:LANES]
            for c in range(1, tk // LANES):
                p_lanes = p_lanes + p[:, c * LANES:(c + 1) * LANES]
            l_sc[g] = alpha * l_sc[g] + p_lanes
            acc_sc[g] = _rep(alpha, KV_RANK // LANES) * acc_sc[g] + _dot(p.astype(BF16), v)
            m_sc[g] = m_new

    def body(j, carry):
        block(j, False)
        return carry

    lax.fori_loop(0, i, body, 0)
    block(i, True)

    for hd in range(MLA_HEADS):
        g, r0 = hd // hg, (hd % hg) * tq
        l = jnp.sum(l_sc[g, r0:r0 + tq, :], axis=-1, keepdims=True)
        o = acc_sc[g, r0:r0 + tq, :] * (1.0 / l)
        o_ref[0, :, hd * V_HEAD:(hd + 1) * V_HEAD] = _dot(o.astype(BF16), wv_ref[hd])


def _flash(q, kcat, kt, w_vup, *, tq, hg):
    B, H, T, _ = q.shape
    tk = kt.shape[-1]
    assert tq == tk and H % hg == 0
    kern = functools.partial(_flash_kernel, tq=tq, tk=tk, hg=hg)
    return pl.pallas_call(
        kern,
        grid=(B, T // tq),
        in_specs=[
            pl.BlockSpec((1, H, tq, KCAT), lambda b, i: (b, 0, i, 0)),
            pl.BlockSpec((1, T, KCAT), lambda b, i: (b, 0, 0)),
            pl.BlockSpec((1, T // tk, KCAT, tk), lambda b, i: (b, 0, 0, 0)),
            pl.BlockSpec((H, KV_RANK, V_HEAD), lambda b, i: (0, 0, 0)),
        ],
        out_specs=pl.BlockSpec((1, tq, MLA_WIDTH), lambda b, i: (b, i, 0)),
        out_shape=jax.ShapeDtypeStruct((B, T, MLA_WIDTH), F32),
        scratch_shapes=[pltpu.VMEM((H // hg, hg * tq, LANES), F32), pltpu.VMEM((H // hg, hg * tq, LANES), F32),
                        pltpu.VMEM((H // hg, hg * tq, KV_RANK), F32)],
        compiler_params=_cparams(2),
        name="flash_prompt",
    )(q, kcat, kt, w_vup)


def _paged_kernel(pt_ref, q_ref, knew_ref, ckv_hbm, krt_hbm, o_ref,
                  ckv_buf, krt_buf, kbf_sc, s_sc, sem, m_sc, l_sc, acc_sc,
                  *, n_b, n_ch, cp, page, tp, n_new, sub):
    total = n_b * n_ch
    n_sub = (cp * page) // sub

    def copies(bc):
        slot = bc % N_RAW
        b = bc // n_ch
        c = bc % n_ch
        out = []
        for p in range(cp):
            pg = pt_ref[b, c * cp + p]
            out.append(pltpu.make_async_copy(ckv_hbm.at[pg], ckv_buf.at[slot, pl.ds(p * page, page)],
                                             sem.at[0, slot]))
            out.append(pltpu.make_async_copy(krt_hbm.at[pg], krt_buf.at[slot, :, pl.ds(p * page, page)],
                                             sem.at[1, slot]))
        return out

    def score(bc, w):
        slot = bc % N_RAW
        b = bc // n_ch
        for cpy in copies(bc):
            cpy.wait()
        q_lat = q_ref[b, :, 0:KV_RANK]
        q_rope = q_ref[b, :, KV_RANK:KV_RANK + QK_ROPE]
        for j in range(n_sub):
            cols = slice(j * sub, (j + 1) * sub)
            kc = ckv_buf[slot, cols, :].astype(BF16)
            krt = krt_buf[slot, :, cols].astype(BF16)
            kbf_sc[w, cols, :] = kc
            s_sc[w, :, cols] = _dot_nt(q_lat, kc) + _dot(q_rope, krt)

    def update(s, v_bf16, first):
        m_prev = jnp.where(first, NEG, m_sc[...])
        l_prev = jnp.where(first, 0.0, l_sc[...])
        acc_prev = jnp.where(first, 0.0, acc_sc[...])
        m_new = jnp.maximum(m_prev, jnp.max(s, axis=-1, keepdims=True))
        alpha = jnp.exp2(m_prev - m_new)
        p = jnp.exp2(s - _rep(m_new, s.shape[1] // LANES))
        l_sc[...] = alpha * l_prev + jnp.sum(p, axis=-1, keepdims=True)
        acc_sc[...] = _rep(alpha, KV_RANK // LANES) * acc_prev + _dot(p.astype(BF16), v_bf16)
        m_sc[...] = m_new

    def attend(bc, w):
        b = bc // n_ch
        c = bc % n_ch
        update(s_sc[w], kbf_sc[w], c == 0)

        def finish():
            kn = knew_ref[b]
            sn = _dot_nt(q_ref[b], kn)
            tpos = lax.broadcasted_iota(jnp.int32, sn.shape, 0) & (tp - 1)
            col = lax.broadcasted_iota(jnp.int32, sn.shape, 1)
            sn = jnp.where((col <= tpos) & (col < n_new), sn, NEG)
            update(sn, kn[:, 0:KV_RANK], False)
            o_ref[b] = acc_sc[...] * _rep(1.0 / l_sc[...], KV_RANK // LANES)

        last = c == n_ch - 1
        if isinstance(last, bool):
            if last:
                finish()
        else:
            pl.when(last)(finish)

    m_sc[...] = jnp.full(m_sc.shape, NEG, F32)
    l_sc[...] = jnp.zeros(l_sc.shape, F32)
    acc_sc[...] = jnp.zeros(acc_sc.shape, F32)
    for bc in range(min(N_RAW, total)):
        for cpy in copies(bc):
            cpy.start()
    score(0, 0)

    def step(i, w):
        def prefetch():
            for cpy in copies(i + N_RAW):
                cpy.start()

        more = i + N_RAW < total
        if isinstance(more, bool):
            if more:
                prefetch()
        else:
            pl.when(more)(prefetch)
        if isinstance(i, int) and i + 1 >= total:
            attend(i, w)
            return
        score(i + 1, 1 - w)
        attend(i, w)

    def body(k, carry):
        step(2 * k, 0)
        step(2 * k + 1, 1)
        return carry

    n_pairs = (total - 2) // 2
    lax.fori_loop(0, n_pairs, body, 0)
    for i in range(2 * n_pairs, total):
        step(i, i % 2)


def _paged_attention(page_table, q, knew, cache_ckv, cache_krt, *, cp, tp, n_new, sub):
    n_b, n_pages = page_table.shape
    page = cache_ckv.shape[1]
    rows = q.shape[1]
    n_ch = n_pages // cp
    assert (cp * page) % sub == 0
    kern = functools.partial(_paged_kernel, n_b=n_b, n_ch=n_ch, cp=cp, page=page, tp=tp, n_new=n_new, sub=sub)
    grid_spec = pltpu.PrefetchScalarGridSpec(
        num_scalar_prefetch=1,
        grid=(1,),
        in_specs=[
            pl.BlockSpec(q.shape, lambda i, pt: (0, 0, 0)),
            pl.BlockSpec(knew.shape, lambda i, pt: (0, 0, 0)),
            pl.BlockSpec(memory_space=pl.ANY),
            pl.BlockSpec(memory_space=pl.ANY),
        ],
        out_specs=pl.BlockSpec((n_b, rows, KV_RANK), lambda i, pt: (0, 0, 0)),
        scratch_shapes=[
            pltpu.VMEM((N_RAW, cp * page, KV_RANK), F32),
            pltpu.VMEM((N_RAW, QK_ROPE, cp * page), F32),
            pltpu.VMEM((2, cp * page, KV_RANK), BF16),
            pltpu.VMEM((2, rows, cp * page), F32),
            pltpu.SemaphoreType.DMA((2, N_RAW)),
            pltpu.VMEM((rows, LANES), F32),
            pltpu.VMEM((rows, LANES), F32),
            pltpu.VMEM((rows, KV_RANK), F32),
        ],
    )
    return pl.pallas_call(
        kern,
        grid_spec=grid_spec,
        out_shape=jax.ShapeDtypeStruct((n_b, rows, KV_RANK), F32),
        compiler_params=_cparams(1),
        name="paged_attention",
    )(page_table, q, knew, cache_ckv, cache_krt)


def _v_up_kernel(o_ref, wv_ref, tok_ref):
    n_b, _, tp, _ = o_ref.shape
    for hd in range(MLA_HEADS):
        oh = o_ref[:, hd].reshape(n_b * tp, KV_RANK).astype(BF16)
        tok_ref[:, :, hd * V_HEAD:(hd + 1) * V_HEAD] = _dot(oh, wv_ref[hd]).reshape(n_b, tp, V_HEAD)


def _v_up(o, w_vup):
    n_b, H, tp, _ = o.shape
    return pl.pallas_call(
        _v_up_kernel,
        out_shape=jax.ShapeDtypeStruct((n_b, tp, MLA_WIDTH), F32),
        compiler_params=pltpu.CompilerParams(vmem_limit_bytes=VMEM_LIMIT),
        name="v_up_sample",
    )(o, w_vup)


def _b_post_kernel(x_ref, tok_ref, z_ref, mk_ref, mv_ref, wout_ref, gf_ref, xo_ref, *, final):
    nb, tq, _ = x_ref.shape
    m = nb * tq
    x = x_ref[...].reshape(m, D_MODEL)
    acc = x
    for g in range(4):
        c0, c1 = g * 256, (g + 1) * 256
        mt = (tok_ref[:, :, c0:c1] * _silu(z_ref[:, :, c0:c1])).reshape(m, 256).astype(BF16)
        acc = acc + _dot(mt, wout_ref[c0:c1, :])
    mem_o = _mem_attend(z_ref[:, :, 1024:1536].reshape(m, MEM_WIDTH), mk_ref, mv_ref, nb, tq)
    for hd in range(MEM_HEADS):
        c0, c1 = hd * MEM_HEAD_DIM, (hd + 1) * MEM_HEAD_DIM
        gate = z_ref[:, :, 1536 + c0:1536 + c1].reshape(m, MEM_HEAD_DIM)
        mm = (mem_o[hd] * _silu(gate)).astype(BF16)
        acc = acc + _dot(mm, wout_ref[1024 + c0:1024 + c1, :])
    if final:
        acc = _rms(acc, gf_ref[...])
    xo_ref[...] = acc.reshape(nb, tq, D_MODEL)


def _b_post(x, tok, zrest, mem_k, mem_v, w_out, g_final, *, layer, nb, tq, final, tag):
    B, T, _ = x.shape
    kern = functools.partial(_b_post_kernel, final=final)
    return pl.pallas_call(
        kern,
        grid=(B // nb, T // tq),
        in_specs=[
            pl.BlockSpec((nb, tq, D_MODEL), lambda b, t: (b, t, 0)),
            pl.BlockSpec((nb, tq, MLA_WIDTH), lambda b, t: (b, t, 0)),
            pl.BlockSpec((nb, tq, 2048), lambda b, t: (b, t, 0)),
            pl.BlockSpec((1, nb, MEM_ROWS, MEM_HEAD_DIM), lambda b, t: (layer, b, 0, 0)),
            pl.BlockSpec((1, nb, MEM_ROWS, MEM_HEAD_DIM), lambda b, t: (layer, b, 0, 0)),
            pl.BlockSpec((1536, D_MODEL), lambda b, t: (0, 0)),
            pl.BlockSpec((1, D_MODEL), lambda b, t: (0, 0)),
        ],
        out_specs=pl.BlockSpec((nb, tq, D_MODEL), lambda b, t: (b, t, 0)),
        out_shape=jax.ShapeDtypeStruct((B, T, D_MODEL), F32),
        compiler_params=_cparams(2),
        name=f"b_post{layer}_{tag}",
    )(x, tok, zrest, mem_k, mem_v, w_out, g_final)


def _rope_tables(pos):
    half = QK_ROPE // 2
    inv = ROPE_THETA ** (-jnp.arange(half, dtype=F32) / half)
    ang = pos.astype(F32)[:, None] * inv[None, :]
    return jnp.tile(jnp.cos(ang), (1, LANES // half)), jnp.tile(jnp.sin(ang), (1, LANES // half))


def kernel(x_prompt, x_sample, state_pool, cache_ckv, cache_krope, cache_mem_k, cache_mem_v, page_table, mem_prompt, g_norm, w_in_a, w_pool_grp, pool_scale, w_in_b, g_q_latent, w_q_up, g_kv_in, w_kv_down, g_kv_latent, w_k_up, w_v_up, g_mem, w_mem_k, w_mem_v, w_out, g_final):
    B, T, _ = x_prompt.shape
    SB, ST, _ = x_sample.shape
    TP = SUBLANES
    past = page_table.shape[1] * cache_ckv.shape[1]

    w_in_a16 = w_in_a.astype(BF16)
    w_grp16 = w_pool_grp.astype(BF16)
    w_out16 = w_out.astype(BF16)
    w_cq16 = w_in_b[:, :, :Q_RANK].astype(BF16)
    w_rest16 = w_in_b[:, :, Q_RANK:].astype(BF16)
    wq = w_q_up.reshape(N_A, Q_RANK, MLA_HEADS, QK_NOPE + QK_ROPE)
    w_qn16 = wq[..., :QK_NOPE].reshape(N_A, Q_RANK, MLA_HEADS * QK_NOPE).astype(BF16)
    w_qr16 = wq[..., QK_NOPE:].reshape(N_A, Q_RANK, MLA_HEADS * QK_ROPE).astype(BF16)
    w_kup16 = jnp.transpose(w_k_up, (1, 2, 0)).astype(BF16)
    w_vup16 = jnp.transpose(w_v_up, (1, 0, 2)).astype(BF16)
    w_kv16 = jnp.pad(w_kv_down, ((0, 0), (0, KCAT - KV_RANK - QK_ROPE))).astype(BF16)
    w_mk16 = w_mem_k.astype(BF16)
    w_mv16 = w_mem_v.astype(BF16)
    g_kv_in2 = g_kv_in[None]
    g_lat2 = g_kv_latent[None]
    g_final2 = g_final[None]

    def trunk(x, prev, mem_k, mem_v, pos, *, tag, nb, tq, emit_kt, n_valid, attend):
        pools = []
        for l in range(N_A):
            x, pool = _a_layer(x, prev[l], g_norm[l][None], w_in_a16[l], w_grp16[l], pool_scale[l][None],
                               mem_k, mem_v, w_out16[l], layer=l, nb=nb, tq=tq, n_valid=n_valid, pos0=pos)
            pools.append(pool[:, 1:])
        cos, sin = _rope_tables(pos + jnp.arange(x.shape[1], dtype=jnp.int32))
        ckv, krope, *kcat = _kv_latent(x, g_kv_in2, w_kv16, g_lat2, cos, sin, nb=nb, tq=tq, emit_kt=emit_kt)
        for j in range(DEPTH - N_A):
            l = N_A + j
            q, zrest = _b_pre(x, g_norm[l][None], w_cq16[j], w_rest16[j], g_q_latent[j][None],
                              w_qn16[j], w_qr16[j], w_kup16, cos, sin, nb=nb, tq=tq, tag=f"{tag}{l}")
            tok = attend(q, *kcat)
            x = _b_post(x, tok, zrest, mem_k, mem_v, w_out16[l], g_final2, layer=l, nb=nb, tq=tq,
                        final=(l == DEPTH - 1), tag=tag)
        return x, jnp.stack(pools, axis=0), ckv, krope

    mem_k_p, mem_v_p = _mem_project(mem_prompt, g_mem, w_mk16, w_mv16)
    prev_p = jnp.zeros((N_A, B, HALO, D_MODEL), F32)
    y_p, pool_p, ckv_p, krope_p = trunk(
        x_prompt, prev_p, mem_k_p, mem_v_p, 0, tag="p", nb=1, tq=256, emit_kt=True, n_valid=256,
        attend=lambda q, kcat, kt: _flash(q, kcat, kt, w_vup16, tq=256, hg=2))

    xs = jnp.pad(x_sample, ((0, 0), (0, TP - ST), (0, 0)))
    prev_s = jnp.pad(state_pool, ((0, 0), (0, 0), (HALO - POOL_BUF, 0), (0, 0)))
    mem_k_s = cache_mem_k.reshape(DEPTH, SB, MEM_ROWS, MEM_HEAD_DIM)
    mem_v_s = cache_mem_v.reshape(DEPTH, SB, MEM_ROWS, MEM_HEAD_DIM)
    cache_krt = jnp.transpose(cache_krope, (0, 2, 1))

    def attend_sample(q, kcat):
        qf = q.reshape(SB, MLA_HEADS * TP, KCAT)
        knew = jnp.pad(kcat, ((0, 0), (0, LANES - TP), (0, 0)))
        o = _paged_attention(page_table, qf, knew, cache_ckv, cache_krt, cp=8, tp=TP, n_new=ST, sub=1024)
        return _v_up(o.reshape(SB, MLA_HEADS, TP, KV_RANK), w_vup16)

    y_s, pool_s, ckv_s, krope_s = trunk(
        xs, prev_s, mem_k_s, mem_v_s, past, tag="s", nb=8, tq=TP, emit_kt=False, n_valid=ST,
        attend=attend_sample)

    mem_shape = (DEPTH, B, MEM_TOKENS, MEM_HEADS, MEM_HEAD_DIM)
    return (y_p, y_s[:, :ST], pool_p, pool_s, ckv_p, krope_p, ckv_s[:, :ST], krope_s[:, :ST],
            mem_k_p.reshape(mem_shape), mem_v_p.reshape(mem_shape))
```

```python
import functools

import jax
import jax.numpy as jnp
from jax import lax
from jax.experimental import pallas as pl
from jax.experimental.pallas import tpu as pltpu

F32 = jnp.float32
BF16 = jnp.bfloat16

D_MODEL = 1024
DEPTH = 4
N_A = 2
POOL_WINDOWS = (2, 4, 8, 16)
POOL_GROUP = 256
POOL_BUF = 15
MLA_HEADS = 8
QK_NOPE = 128
QK_ROPE = 64
V_HEAD = 128
KV_RANK = 256
Q_RANK = 384
MLA_WIDTH = MLA_HEADS * V_HEAD
MLA_SCALE = (QK_NOPE + QK_ROPE) ** -0.5
ROPE_THETA = 10000.0
MEM_TOKENS = 256
MEM_HEADS = 4
MEM_HEAD_DIM = 128
MEM_WIDTH = MEM_HEADS * MEM_HEAD_DIM
MEM_SCALE = MEM_HEAD_DIM ** -0.5
EPS = 1e-6
NEG = -1e30
LOG2E = 1.4426950408889634

KCAT = 384
SUBLANES = 8
LANES = 128
HALO = 16
E_OFF = SUBLANES + HALO
VMEM_LIMIT = 56 * 1024 * 1024
MEM_ROWS = MEM_TOKENS * MEM_HEADS
N_RAW = 4


def _cparams(n_axes):
    return pltpu.CompilerParams(dimension_semantics=("arbitrary",) * n_axes,
                                vmem_limit_bytes=VMEM_LIMIT)


def _rms(x, g):
    return x * lax.rsqrt(jnp.mean(x * x, axis=-1, keepdims=True) + EPS) * g


def _dot(a, b):
    return jnp.dot(a, b, preferred_element_type=F32)


def _dot_nt(a, b):
    return lax.dot_general(a, b, (((1,), (1,)), ((), ())), preferred_element_type=F32)


def _silu(x):
    return x * jax.nn.sigmoid(x)


def _rope_lanes(a, cos, sin):
    lane = lax.broadcasted_iota(jnp.int32, a.shape, 1)
    up = pltpu.roll(a, 32, 1)
    dn = pltpu.roll(a, 96, 1)
    rot = jnp.where((lane & 63) < 32, -dn, up)
    return a * cos + rot * sin


def _rep(x, n):
    return x if n == 1 else jnp.concatenate([x] * n, axis=1)


def _tile_rows(x, n):
    return x if n == 1 else jnp.concatenate([x] * n, axis=0)


def _mem_attend(qm, mk_ref, mv_ref, nb, tq):
    rows, keys = nb * tq, nb * MEM_TOKENS
    if nb > 1:
        row_seq = lax.broadcasted_iota(jnp.int32, (rows, keys), 0) // tq
        col_seq = lax.broadcasted_iota(jnp.int32, (rows, keys), 1) // MEM_TOKENS
        keep = row_seq == col_seq
    outs = []
    for h in range(MEM_HEADS):
        c0, c1 = h * MEM_HEAD_DIM, (h + 1) * MEM_HEAD_DIM
        q = (qm[:, c0:c1] * (MEM_SCALE * LOG2E)).astype(BF16)
        k = mk_ref[0, :, pl.ds(h, MEM_TOKENS, stride=MEM_HEADS), :].reshape(keys, MEM_HEAD_DIM).astype(BF16)
        v = mv_ref[0, :, pl.ds(h, MEM_TOKENS, stride=MEM_HEADS), :].reshape(keys, MEM_HEAD_DIM).astype(BF16)
        s = _dot_nt(q, k)
        if nb > 1:
            s = jnp.where(keep, s, NEG)
        m = jnp.max(s, axis=-1, keepdims=True)
        p = jnp.exp2(s - m)
        l = jnp.sum(p, axis=-1, keepdims=True)
        outs.append(_dot(p.astype(BF16), v) * (1.0 / l))
    return outs


def _mem_proj_kernel(mem_ref, g_ref, wk_ref, wv_ref, mk_ref, mv_ref):
    x = mem_ref[0]
    xn = x * lax.rsqrt(jnp.mean(x * x, axis=-1, keepdims=True) + EPS)
    for l in range(DEPTH):
        h = (xn * g_ref[l:l + 1, :]).astype(BF16)
        mk = _dot(h, wk_ref[l])
        mv = _dot(h, wv_ref[l])
        for hd in range(MEM_HEADS):
            c0, c1 = hd * MEM_HEAD_DIM, (hd + 1) * MEM_HEAD_DIM
            mk_ref[l, 0, pl.ds(hd, MEM_TOKENS, stride=MEM_HEADS), :] = mk[:, c0:c1]
            mv_ref[l, 0, pl.ds(hd, MEM_TOKENS, stride=MEM_HEADS), :] = mv[:, c0:c1]


def _mem_project(mem, g_mem, wk, wv):
    B, M, _ = mem.shape
    out = jax.ShapeDtypeStruct((DEPTH, B, MEM_ROWS, MEM_HEAD_DIM), F32)
    return pl.pallas_call(
        _mem_proj_kernel,
        grid=(B,),
        in_specs=[
            pl.BlockSpec((1, M, D_MODEL), lambda b: (b, 0, 0)),
            pl.BlockSpec((DEPTH, D_MODEL), lambda b: (0, 0)),
            pl.BlockSpec((DEPTH, D_MODEL, MEM_WIDTH), lambda b: (0, 0, 0)),
            pl.BlockSpec((DEPTH, D_MODEL, MEM_WIDTH), lambda b: (0, 0, 0)),
        ],
        out_specs=[
            pl.BlockSpec((DEPTH, 1, MEM_ROWS, MEM_HEAD_DIM), lambda b: (0, b, 0, 0)),
            pl.BlockSpec((DEPTH, 1, MEM_ROWS, MEM_HEAD_DIM), lambda b: (0, b, 0, 0)),
        ],
        out_shape=[out, out],
        compiler_params=_cparams(1),
        name="mem_project",
    )(mem, g_mem, wk, wv)


def _a_layer_kernel(x_ref, prev_ref, g_ref, win_ref, wgrp_ref, pscale_ref, mk_ref, mv_ref, wout_ref,
                    xo_ref, pool_ref, e_sc, sa_sc, sb_sc, *, nb, tq, n_t, n_valid, pos0):
    t = pl.program_id(1)
    rows = HALO + tq
    m = nb * tq

    @pl.when(t == 0)
    def _():
        zeros = jnp.zeros((nb, SUBLANES, D_MODEL), F32)
        e_sc[:, 0:SUBLANES, :] = zeros
        sa_sc[:, 0:SUBLANES, :] = zeros
        sb_sc[:, 0:SUBLANES, :] = zeros
        e_sc[:, SUBLANES:E_OFF, :] = prev_ref[...]

    x = x_ref[...].reshape(m, D_MODEL)
    h = _rms(x, g_ref[...]).astype(BF16)
    e_sc[:, E_OFF:E_OFF + tq, :] = _dot(h, win_ref[:, 0:1024]).reshape(nb, tq, D_MODEL)

    lo = SUBLANES
    sa_sc[:, lo:lo + rows, :] = e_sc[:, lo:lo + rows, :] + e_sc[:, lo - 1:lo - 1 + rows, :]
    sb_sc[:, lo:lo + rows, 256:1024] = (sa_sc[:, lo:lo + rows, 256:1024]
                                        + sa_sc[:, lo - 2:lo - 2 + rows, 256:1024])
    sa_sc[:, lo:lo + rows, 512:1024] = (sb_sc[:, lo:lo + rows, 512:1024]
                                        + sb_sc[:, lo - 4:lo - 4 + rows, 512:1024])
    s16 = sa_sc[:, E_OFF:E_OFF + tq, 768:1024] + sa_sc[:, E_OFF - 8:E_OFF - 8 + tq, 768:1024]
    sums = (sa_sc[:, E_OFF:E_OFF + tq, 0:256], sb_sc[:, E_OFF:E_OFF + tq, 256:512],
            sa_sc[:, E_OFF:E_OFF + tq, 512:768], s16)

    pos = pos0 + t * tq + lax.broadcasted_iota(jnp.int32, (1, tq, 1), 1)
    mixed_tok = []
    for g, w in enumerate(POOL_WINDOWS):
        c0, c1 = g * POOL_GROUP, (g + 1) * POOL_GROUP
        inv_cnt = 1.0 / jnp.minimum(pos + 1, w).astype(F32)
        pooled = (sums[g] * inv_cnt - e_sc[:, E_OFF:E_OFF + tq, c0:c1]).reshape(m, POOL_GROUP)
        mixed = _dot(pooled.astype(BF16), wgrp_ref[g]) * pscale_ref[:, c0:c1]
        gate = _dot(h, win_ref[:, 1024 + c0:1024 + c1])
        mixed_tok.append((mixed * _silu(gate)).astype(BF16))

    qm = _dot(h, win_ref[:, 2048:2560])
    gate_m = _dot(h, win_ref[:, 2560:3072])
    mem_o = _mem_attend(qm, mk_ref, mv_ref, nb, tq)

    acc = x
    for g in range(4):
        acc = acc + _dot(mixed_tok[g], wout_ref[g * 256:(g + 1) * 256, :])
    for hd in range(MEM_HEADS):
        c0, c1 = hd * MEM_HEAD_DIM, (hd + 1) * MEM_HEAD_DIM
        mm = (mem_o[hd] * _silu(gate_m[:, c0:c1])).astype(BF16)
        acc = acc + _dot(mm, wout_ref[1024 + c0:1024 + c1, :])
    xo_ref[...] = acc.reshape(nb, tq, D_MODEL)

    @pl.when(t == n_t - 1)
    def _():
        pool_ref[...] = e_sc[:, E_OFF + n_valid - HALO:E_OFF + n_valid, :]

    if n_t > 1:
        @pl.when(t < n_t - 1)
        def _():
            e_sc[:, SUBLANES:E_OFF, :] = e_sc[:, SUBLANES + tq:E_OFF + tq, :]


def _a_layer(x, prev, g, w_in, w_grp, pscale, mem_k, mem_v, w_out, *, layer, nb, tq, n_valid, pos0):
    B, T, _ = x.shape
    n_t = T // tq
    rows = E_OFF + tq
    kern = functools.partial(_a_layer_kernel, nb=nb, tq=tq, n_t=n_t, n_valid=n_valid, pos0=pos0)
    return pl.pallas_call(
        kern,
        grid=(B // nb, n_t),
        in_specs=[
            pl.BlockSpec((nb, tq, D_MODEL), lambda b, t: (b, t, 0)),
            pl.BlockSpec((nb, HALO, D_MODEL), lambda b, t: (b, 0, 0)),
            pl.BlockSpec((1, D_MODEL), lambda b, t: (0, 0)),
            pl.BlockSpec((D_MODEL, 3072), lambda b, t: (0, 0)),
            pl.BlockSpec((4, POOL_GROUP, POOL_GROUP), lambda b, t: (0, 0, 0)),
            pl.BlockSpec((1, D_MODEL), lambda b, t: (0, 0)),
            pl.BlockSpec((1, nb, MEM_ROWS, MEM_HEAD_DIM), lambda b, t: (layer, b, 0, 0)),
            pl.BlockSpec((1, nb, MEM_ROWS, MEM_HEAD_DIM), lambda b, t: (layer, b, 0, 0)),
            pl.BlockSpec((1536, D_MODEL), lambda b, t: (0, 0)),
        ],
        out_specs=[
            pl.BlockSpec((nb, tq, D_MODEL), lambda b, t: (b, t, 0)),
            pl.BlockSpec((nb, HALO, D_MODEL), lambda b, t: (b, 0, 0)),
        ],
        out_shape=[jax.ShapeDtypeStruct((B, T, D_MODEL), F32),
                   jax.ShapeDtypeStruct((B, HALO, D_MODEL), F32)],
        scratch_shapes=[pltpu.VMEM((nb, rows, D_MODEL), F32)] * 3,
        compiler_params=_cparams(2),
        name=f"a_layer{layer}_{B}",
    )(x, prev, g, w_in, w_grp, pscale, mem_k, mem_v, w_out)


def _kv_latent_kernel(x_ref, gin_ref, wkv_ref, glat_ref, cos_ref, sin_ref, ckv_ref, kr_ref, kcat_ref,
                      *maybe_kt_ref):
    nb, tq, _ = x_ref.shape
    m = nb * tq
    h = _rms(x_ref[...].reshape(m, D_MODEL), gin_ref[...]).astype(BF16)
    kv = _dot(h, wkv_ref[...])
    ckv = _rms(kv[:, 0:KV_RANK], glat_ref[...])
    kr = _rope_lanes(kv[:, KV_RANK:KCAT], _tile_rows(cos_ref[...], nb), _tile_rows(sin_ref[...], nb))
    ckv_ref[...] = ckv.reshape(nb, tq, KV_RANK)
    kr_ref[...] = kr[:, 0:QK_ROPE].reshape(nb, tq, QK_ROPE)
    kcat_ref[:, :, 0:KV_RANK] = ckv.reshape(nb, tq, KV_RANK).astype(BF16)
    kcat_ref[:, :, KV_RANK:KCAT] = kr.reshape(nb, tq, LANES).astype(BF16)
    for kt_ref in maybe_kt_ref:
        kt_ref[0, 0, 0:KV_RANK, :] = ckv.T.astype(BF16)
        kt_ref[0, 0, KV_RANK:KCAT, :] = kr.T.astype(BF16)


def _kv_latent(x, g_in, w_kv, g_lat, cos, sin, *, nb, tq, emit_kt):
    B, T, _ = x.shape
    assert nb == 1 or not emit_kt
    out_specs = [
        pl.BlockSpec((nb, tq, KV_RANK), lambda b, t: (b, t, 0)),
        pl.BlockSpec((nb, tq, QK_ROPE), lambda b, t: (b, t, 0)),
        pl.BlockSpec((nb, tq, KCAT), lambda b, t: (b, t, 0)),
    ]
    out_shape = [jax.ShapeDtypeStruct((B, T, KV_RANK), F32),
                 jax.ShapeDtypeStruct((B, T, QK_ROPE), F32),
                 jax.ShapeDtypeStruct((B, T, KCAT), BF16)]
    if emit_kt:
        out_specs.append(pl.BlockSpec((1, 1, KCAT, tq), lambda b, t: (b, t, 0, 0)))
        out_shape.append(jax.ShapeDtypeStruct((B, T // tq, KCAT, tq), BF16))
    return pl.pallas_call(
        _kv_latent_kernel,
        grid=(B // nb, T // tq),
        in_specs=[
            pl.BlockSpec((nb, tq, D_MODEL), lambda b, t: (b, t, 0)),
            pl.BlockSpec((1, D_MODEL), lambda b, t: (0, 0)),
            pl.BlockSpec((D_MODEL, KCAT), lambda b, t: (0, 0)),
            pl.BlockSpec((1, KV_RANK), lambda b, t: (0, 0)),
            pl.BlockSpec((tq, LANES), lambda b, t: (t, 0)),
            pl.BlockSpec((tq, LANES), lambda b, t: (t, 0)),
        ],
        out_specs=out_specs,
        out_shape=out_shape,
        compiler_params=_cparams(2),
        name=f"kv_latent_{B}",
    )(x, g_in, w_kv, g_lat, cos, sin)


def _b_pre_kernel(x_ref, g_ref, wcq_ref, wrest_ref, gq_ref, wqn_ref, wqr_ref, wkup_ref, cos_ref, sin_ref,
                  q_ref, zrest_ref):
    nb, tq, _ = x_ref.shape
    m = nb * tq
    h = _rms(x_ref[...].reshape(m, D_MODEL), g_ref[...]).astype(BF16)
    zrest_ref[...] = _dot(h, wrest_ref[...]).reshape(nb, tq, 2048)
    cn = _rms(_dot(h, wcq_ref[...]), gq_ref[...]).astype(BF16)
    qn = _dot(cn, wqn_ref[...])
    qr = _dot(cn, wqr_ref[...])
    for hd in range(MLA_HEADS):
        ql = _dot(qn[:, hd * QK_NOPE:(hd + 1) * QK_NOPE].astype(BF16), wkup_ref[hd])
        q_ref[:, hd, :, 0:KV_RANK] = (ql * (MLA_SCALE * LOG2E)).reshape(nb, tq, KV_RANK).astype(BF16)
    cos = _tile_rows(cos_ref[...], nb)
    sin = _tile_rows(sin_ref[...], nb)
    lane = lax.broadcasted_iota(jnp.int32, cos.shape, 1)
    for j in range(MLA_HEADS // 2):
        rr = _rope_lanes(qr[:, j * LANES:(j + 1) * LANES], cos, sin) * (MLA_SCALE * LOG2E)
        even = jnp.where(lane < QK_ROPE, rr, 0.0)
        odd = jnp.where(lane < QK_ROPE, pltpu.roll(rr, 64, 1), 0.0)
        q_ref[:, 2 * j, :, KV_RANK:KCAT] = even.reshape(nb, tq, LANES).astype(BF16)
        q_ref[:, 2 * j + 1, :, KV_RANK:KCAT] = odd.reshape(nb, tq, LANES).astype(BF16)


def _b_pre(x, g, w_cq, w_rest, g_q, w_qn, w_qr, w_kup, cos, sin, *, nb, tq, tag):
    B, T, _ = x.shape
    const2 = lambda b, t: (0, 0)
    return pl.pallas_call(
        _b_pre_kernel,
        grid=(B // nb, T // tq),
        in_specs=[
            pl.BlockSpec((nb, tq, D_MODEL), lambda b, t: (b, t, 0)),
            pl.BlockSpec((1, D_MODEL), const2),
            pl.BlockSpec((D_MODEL, Q_RANK), const2),
            pl.BlockSpec((D_MODEL, 2048), const2),
            pl.BlockSpec((1, Q_RANK), const2),
            pl.BlockSpec((Q_RANK, MLA_HEADS * QK_NOPE), const2),
            pl.BlockSpec((Q_RANK, MLA_HEADS * QK_ROPE), const2),
            pl.BlockSpec((MLA_HEADS, QK_NOPE, KV_RANK), lambda b, t: (0, 0, 0)),
            pl.BlockSpec((tq, LANES), lambda b, t: (t, 0)),
            pl.BlockSpec((tq, LANES), lambda b, t: (t, 0)),
        ],
        out_specs=[
            pl.BlockSpec((nb, MLA_HEADS, tq, KCAT), lambda b, t: (b, 0, t, 0)),
            pl.BlockSpec((nb, tq, 2048), lambda b, t: (b, t, 0)),
        ],
        out_shape=[jax.ShapeDtypeStruct((B, MLA_HEADS, T, KCAT), BF16),
                   jax.ShapeDtypeStruct((B, T, 2048), F32)],
        compiler_params=_cparams(2),
        name=f"b_pre_{tag}",
    )(x, g, w_cq, w_rest, g_q, w_qn, w_qr, w_kup, cos, sin)


def _flash_kernel(q_ref, k_ref, kt_ref, wv_ref, x_ref, z_ref, mk_ref, mv_ref, wout_ref, gf_ref, xo_ref,
                  m_sc, l_sc, acc_sc, tok_sc, *, tq, tk, hg, final):
    i = pl.program_id(1)
    n_grp = MLA_HEADS // hg
    rows = hg * tq
    m_sc[...] = jnp.full(m_sc.shape, NEG, F32)
    l_sc[...] = jnp.zeros(l_sc.shape, F32)
    acc_sc[...] = jnp.zeros(acc_sc.shape, F32)

    def q_group(g):
        return q_ref[0, g * hg:(g + 1) * hg].reshape(rows, KCAT)

    def block(j, masked):
        kt = kt_ref[0, j]
        v = k_ref[0, pl.ds(pl.multiple_of(j * tk, tk), tk), 0:KV_RANK]
        if masked:
            keep = (lax.broadcasted_iota(jnp.int32, (rows, tk), 1)
                    <= (lax.broadcasted_iota(jnp.int32, (rows, tk), 0) & (tq - 1)))
        s_next = _dot(q_group(0), kt)
        for g in range(n_grp):
            s = s_next
            if g + 1 < n_grp:
                s_next = _dot(q_group(g + 1), kt)
            if masked:
                s = jnp.where(keep, s, NEG)
            m_prev = m_sc[g]
            m_new = jnp.maximum(m_prev, jnp.max(s, axis=-1, keepdims=True))
            alpha = jnp.exp2(m_prev - m_new)
            p = jnp.exp2(s - _rep(m_new, tk // LANES))
            p_lanes = p[:, 0:LANES]
            for c in range(1, tk // LANES):
                p_lanes = p_lanes + p[:, c * LANES:(c + 1) * LANES]
            l_sc[g] = alpha * l_sc[g] + p_lanes
            acc_sc[g] = _rep(alpha, KV_RANK // LANES) * acc_sc[g] + _dot(p.astype(BF16), v)
            m_sc[g] = m_new

    def body(j, carry):
        block(j, False)
        return carry

    lax.fori_loop(0, i, body, 0)
    block(i, True)

    for hd in range(MLA_HEADS):
        g, r0 = hd // hg, (hd % hg) * tq
        l = jnp.sum(l_sc[g, r0:r0 + tq, :], axis=-1, keepdims=True)
        o = acc_sc[g, r0:r0 + tq, :] * (1.0 / l)
        tok_sc[0, :, hd * V_HEAD:(hd + 1) * V_HEAD] = _dot(o.astype(BF16), wv_ref[hd])

    _mix_out(x_ref, tok_sc, z_ref, mk_ref, mv_ref, wout_ref, gf_ref, xo_ref, final)


def _flash_post(q, kcat, kt, w_vup, x, zrest, mem_k, mem_v, w_out, g_final, *, layer, tq, hg, final):
    B, H, T, _ = q.shape
    tk = kt.shape[-1]
    assert tq == tk and H % hg == 0
    kern = functools.partial(_flash_kernel, tq=tq, tk=tk, hg=hg, final=final)
    return pl.pallas_call(
        kern,
        grid=(B, T // tq),
        in_specs=[
            pl.BlockSpec((1, H, tq, KCAT), lambda b, i: (b, 0, i, 0)),
            pl.BlockSpec((1, T, KCAT), lambda b, i: (b, 0, 0)),
            pl.BlockSpec((1, T // tk, KCAT, tk), lambda b, i: (b, 0, 0, 0)),
            pl.BlockSpec((H, KV_RANK, V_HEAD), lambda b, i: (0, 0, 0)),
            pl.BlockSpec((1, tq, D_MODEL), lambda b, i: (b, i, 0)),
            pl.BlockSpec((1, tq, 2048), lambda b, i: (b, i, 0)),
            pl.BlockSpec((1, 1, MEM_ROWS, MEM_HEAD_DIM), lambda b, i: (layer, b, 0, 0)),
            pl.BlockSpec((1, 1, MEM_ROWS, MEM_HEAD_DIM), lambda b, i: (layer, b, 0, 0)),
            pl.BlockSpec((1536, D_MODEL), lambda b, i: (0, 0)),
            pl.BlockSpec((1, D_MODEL), lambda b, i: (0, 0)),
        ],
        out_specs=pl.BlockSpec((1, tq, D_MODEL), lambda b, i: (b, i, 0)),
        out_shape=jax.ShapeDtypeStruct((B, T, D_MODEL), F32),
        scratch_shapes=[pltpu.VMEM((H // hg, hg * tq, LANES), F32), pltpu.VMEM((H // hg, hg * tq, LANES), F32),
                        pltpu.VMEM((H // hg, hg * tq, KV_RANK), F32), pltpu.VMEM((1, tq, MLA_WIDTH), F32)],
        compiler_params=_cparams(2),
        name=f"flash_post{layer}",
    )(q, kcat, kt, w_vup, x, zrest, mem_k, mem_v, w_out, g_final)


def _paged_kernel(pt_ref, q_ref, knew_ref, ckv_hbm, krt_hbm, o_ref,
                  ckv_buf, krt_buf, kbf_sc, s_sc, sem, m_sc, l_sc, acc_sc,
                  *, n_b, n_ch, cp, page, tp, n_new):
    total = n_b * n_ch

    def copies(bc):
        slot = bc % N_RAW
        b = bc // n_ch
        c = bc % n_ch
        out = []
        for p in range(cp):
            pg = pt_ref[b, c * cp + p]
            out.append(pltpu.make_async_copy(ckv_hbm.at[pg], ckv_buf.at[slot, pl.ds(p * page, page)],
                                             sem.at[0, slot]))
            out.append(pltpu.make_async_copy(krt_hbm.at[pg], krt_buf.at[slot, p], sem.at[1, slot]))
        return out

    def score(bc, w):
        slot = bc % N_RAW
        b = bc // n_ch
        for cpy in copies(bc):
            cpy.wait()
        q_lat = q_ref[b, :, 0:KV_RANK]
        q_rope = q_ref[b, :, KV_RANK:KV_RANK + QK_ROPE]
        kc = ckv_buf[slot].astype(BF16)
        krt = jnp.concatenate([krt_buf[slot, p] for p in range(cp)], axis=1).astype(BF16)
        kbf_sc[w] = kc
        s_sc[w] = _dot_nt(q_lat, kc) + _dot(q_rope, krt)

    def update(s, v_bf16, first):
        m_prev = jnp.where(first, NEG, m_sc[...])
        l_prev = jnp.where(first, 0.0, l_sc[...])
        acc_prev = jnp.where(first, 0.0, acc_sc[...])
        m_new = jnp.maximum(m_prev, jnp.max(s, axis=-1, keepdims=True))
        alpha = jnp.exp2(m_prev - m_new)
        p = jnp.exp2(s - _rep(m_new, s.shape[1] // LANES))
        l_sc[...] = alpha * l_prev + jnp.sum(p, axis=-1, keepdims=True)
        acc_sc[...] = _rep(alpha, KV_RANK // LANES) * acc_prev + _dot(p.astype(BF16), v_bf16)
        m_sc[...] = m_new

    def attend(bc, w):
        b = bc // n_ch
        c = bc % n_ch
        update(s_sc[w], kbf_sc[w], c == 0)

        def finish():
            kn = knew_ref[b]
            sn = _dot_nt(q_ref[b], kn)
            tpos = lax.broadcasted_iota(jnp.int32, sn.shape, 0) & (tp - 1)
            col = lax.broadcasted_iota(jnp.int32, sn.shape, 1)
            sn = jnp.where((col <= tpos) & (col < n_new), sn, NEG)
            update(sn, kn[:, 0:KV_RANK], False)
            o_ref[b] = acc_sc[...] * _rep(1.0 / l_sc[...], KV_RANK // LANES)

        last = c == n_ch - 1
        if isinstance(last, bool):
            if last:
                finish()
        else:
            pl.when(last)(finish)

    m_sc[...] = jnp.full(m_sc.shape, NEG, F32)
    l_sc[...] = jnp.zeros(l_sc.shape, F32)
    acc_sc[...] = jnp.zeros(acc_sc.shape, F32)
    for bc in range(min(N_RAW, total)):
        for cpy in copies(bc):
            cpy.start()
    score(0, 0)

    def step(i, w):
        def prefetch():
            for cpy in copies(i + N_RAW):
                cpy.start()

        more = i + N_RAW < total
        if isinstance(more, bool):
            if more:
                prefetch()
        else:
            pl.when(more)(prefetch)
        if isinstance(i, int) and i + 1 >= total:
            attend(i, w)
            return
        score(i + 1, 1 - w)
        attend(i, w)

    def body(k, carry):
        step(2 * k, 0)
        step(2 * k + 1, 1)
        return carry

    n_pairs = (total - 2) // 2
    lax.fori_loop(0, n_pairs, body, 0)
    for i in range(2 * n_pairs, total):
        step(i, i % 2)


def _paged_attention(page_table, q, knew, cache_ckv, cache_krt, *, cp, tp, n_new):
    n_b, n_pages = page_table.shape
    page = cache_ckv.shape[1]
    rows = q.shape[1]
    n_ch = n_pages // cp
    kern = functools.partial(_paged_kernel, n_b=n_b, n_ch=n_ch, cp=cp, page=page, tp=tp, n_new=n_new)
    grid_spec = pltpu.PrefetchScalarGridSpec(
        num_scalar_prefetch=1,
        grid=(1,),
        in_specs=[
            pl.BlockSpec(q.shape, lambda i, pt: (0, 0, 0)),
            pl.BlockSpec(knew.shape, lambda i, pt: (0, 0, 0)),
            pl.BlockSpec(memory_space=pl.ANY),
            pl.BlockSpec(memory_space=pl.ANY),
        ],
        out_specs=pl.BlockSpec((n_b, rows, KV_RANK), lambda i, pt: (0, 0, 0)),
        scratch_shapes=[
            pltpu.VMEM((N_RAW, cp * page, KV_RANK), F32),
            pltpu.VMEM((N_RAW, cp, QK_ROPE, page), F32),
            pltpu.VMEM((2, cp * page, KV_RANK), BF16),
            pltpu.VMEM((2, rows, cp * page), F32),
            pltpu.SemaphoreType.DMA((2, N_RAW)),
            pltpu.VMEM((rows, LANES), F32),
            pltpu.VMEM((rows, LANES), F32),
            pltpu.VMEM((rows, KV_RANK), F32),
        ],
    )
    return pl.pallas_call(
        kern,
        grid_spec=grid_spec,
        out_shape=jax.ShapeDtypeStruct((n_b, rows, KV_RANK), F32),
        compiler_params=_cparams(1),
        name="paged_attention",
    )(page_table, q, knew, cache_ckv, cache_krt)


def _v_up_kernel(o_ref, wv_ref, tok_ref):
    n_b, _, tp, _ = o_ref.shape
    for hd in range(MLA_HEADS):
        oh = o_ref[:, hd].reshape(n_b * tp, KV_RANK).astype(BF16)
        tok_ref[:, :, hd * V_HEAD:(hd + 1) * V_HEAD] = _dot(oh, wv_ref[hd]).reshape(n_b, tp, V_HEAD)


def _v_up(o, w_vup):
    n_b, H, tp, _ = o.shape
    return pl.pallas_call(
        _v_up_kernel,
        out_shape=jax.ShapeDtypeStruct((n_b, tp, MLA_WIDTH), F32),
        compiler_params=pltpu.CompilerParams(vmem_limit_bytes=VMEM_LIMIT),
        name="v_up_sample",
    )(o, w_vup)


def _b_post_kernel(x_ref, tok_ref, z_ref, mk_ref, mv_ref, wout_ref, gf_ref, xo_ref, *, final):
    _mix_out(x_ref, tok_ref, z_ref, mk_ref, mv_ref, wout_ref, gf_ref, xo_ref, final)


def _mix_out(x_ref, tok_ref, z_ref, mk_ref, mv_ref, wout_ref, gf_ref, xo_ref, final):
    nb, tq, _ = x_ref.shape
    m = nb * tq
    x = x_ref[...].reshape(m, D_MODEL)
    acc = x
    for g in range(4):
        c0, c1 = g * 256, (g + 1) * 256
        mt = (tok_ref[:, :, c0:c1] * _silu(z_ref[:, :, c0:c1])).reshape(m, 256).astype(BF16)
        acc = acc + _dot(mt, wout_ref[c0:c1, :])
    mem_o = _mem_attend(z_ref[:, :, 1024:1536].reshape(m, MEM_WIDTH), mk_ref, mv_ref, nb, tq)
    for hd in range(MEM_HEADS):
        c0, c1 = hd * MEM_HEAD_DIM, (hd + 1) * MEM_HEAD_DIM
        gate = z_ref[:, :, 1536 + c0:1536 + c1].reshape(m, MEM_HEAD_DIM)
        mm = (mem_o[hd] * _silu(gate)).astype(BF16)
        acc = acc + _dot(mm, wout_ref[1024 + c0:1024 + c1, :])
    if final:
        acc = _rms(acc, gf_ref[...])
    xo_ref[...] = acc.reshape(nb, tq, D_MODEL)


def _b_post(x, tok, zrest, mem_k, mem_v, w_out, g_final, *, layer, nb, tq, final, tag):
    B, T, _ = x.shape
    kern = functools.partial(_b_post_kernel, final=final)
    return pl.pallas_call(
        kern,
        grid=(B // nb, T // tq),
        in_specs=[
            pl.BlockSpec((nb, tq, D_MODEL), lambda b, t: (b, t, 0)),
            pl.BlockSpec((nb, tq, MLA_WIDTH), lambda b, t: (b, t, 0)),
            pl.BlockSpec((nb, tq, 2048), lambda b, t: (b, t, 0)),
            pl.BlockSpec((1, nb, MEM_ROWS, MEM_HEAD_DIM), lambda b, t: (layer, b, 0, 0)),
            pl.BlockSpec((1, nb, MEM_ROWS, MEM_HEAD_DIM), lambda b, t: (layer, b, 0, 0)),
            pl.BlockSpec((1536, D_MODEL), lambda b, t: (0, 0)),
            pl.BlockSpec((1, D_MODEL), lambda b, t: (0, 0)),
        ],
        out_specs=pl.BlockSpec((nb, tq, D_MODEL), lambda b, t: (b, t, 0)),
        out_shape=jax.ShapeDtypeStruct((B, T, D_MODEL), F32),
        compiler_params=_cparams(2),
        name=f"b_post{layer}_{tag}",
    )(x, tok, zrest, mem_k, mem_v, w_out, g_final)


def _rope_tables(pos):
    half = QK_ROPE // 2
    inv = ROPE_THETA ** (-jnp.arange(half, dtype=F32) / half)
    ang = pos.astype(F32)[:, None] * inv[None, :]
    return jnp.tile(jnp.cos(ang), (1, LANES // half)), jnp.tile(jnp.sin(ang), (1, LANES // half))


def kernel(x_prompt, x_sample, state_pool, cache_ckv, cache_krope, cache_mem_k, cache_mem_v, page_table, mem_prompt, g_norm, w_in_a, w_pool_grp, pool_scale, w_in_b, g_q_latent, w_q_up, g_kv_in, w_kv_down, g_kv_latent, w_k_up, w_v_up, g_mem, w_mem_k, w_mem_v, w_out, g_final):
    B, T, _ = x_prompt.shape
    SB, ST, _ = x_sample.shape
    TP = SUBLANES
    past = page_table.shape[1] * cache_ckv.shape[1]

    w_in_a16 = w_in_a.astype(BF16)
    w_grp16 = w_pool_grp.astype(BF16)
    w_out16 = w_out.astype(BF16)
    w_cq16 = w_in_b[:, :, :Q_RANK].astype(BF16)
    w_rest16 = w_in_b[:, :, Q_RANK:].astype(BF16)
    wq = w_q_up.reshape(N_A, Q_RANK, MLA_HEADS, QK_NOPE + QK_ROPE)
    w_qn16 = wq[..., :QK_NOPE].reshape(N_A, Q_RANK, MLA_HEADS * QK_NOPE).astype(BF16)
    w_qr16 = wq[..., QK_NOPE:].reshape(N_A, Q_RANK, MLA_HEADS * QK_ROPE).astype(BF16)
    w_kup16 = jnp.transpose(w_k_up, (1, 2, 0)).astype(BF16)
    w_vup16 = jnp.transpose(w_v_up, (1, 0, 2)).astype(BF16)
    w_kv16 = jnp.pad(w_kv_down, ((0, 0), (0, KCAT - KV_RANK - QK_ROPE))).astype(BF16)
    w_mk16 = w_mem_k.astype(BF16)
    w_mv16 = w_mem_v.astype(BF16)
    g_kv_in2 = g_kv_in[None]
    g_lat2 = g_kv_latent[None]
    g_final2 = g_final[None]

    def trunk(x, prev, mem_k, mem_v, pos, *, tag, nb, tq, emit_kt, n_valid, attend):
        pools = []
        for l in range(N_A):
            x, pool = _a_layer(x, prev[l], g_norm[l][None], w_in_a16[l], w_grp16[l], pool_scale[l][None],
                               mem_k, mem_v, w_out16[l], layer=l, nb=nb, tq=tq, n_valid=n_valid, pos0=pos)
            pools.append(pool[:, 1:])
        cos, sin = _rope_tables(pos + jnp.arange(x.shape[1], dtype=jnp.int32))
        ckv, krope, *kcat = _kv_latent(x, g_kv_in2, w_kv16, g_lat2, cos, sin, nb=nb, tq=tq, emit_kt=emit_kt)
        for j in range(DEPTH - N_A):
            l = N_A + j
            q, zrest = _b_pre(x, g_norm[l][None], w_cq16[j], w_rest16[j], g_q_latent[j][None],
                              w_qn16[j], w_qr16[j], w_kup16, cos, sin, nb=nb, tq=tq, tag=f"{tag}{l}")
            x = attend(x, q, zrest, *kcat, layer=l, final=(l == DEPTH - 1))
        return x, jnp.stack(pools, axis=0), ckv, krope

    mem_k_p, mem_v_p = _mem_project(mem_prompt, g_mem, w_mk16, w_mv16)
    prev_p = jnp.zeros((N_A, B, HALO, D_MODEL), F32)
    def attend_prompt(x, q, zrest, kcat, kt, *, layer, final):
        return _flash_post(q, kcat, kt, w_vup16, x, zrest, mem_k_p, mem_v_p, w_out16[layer], g_final2,
                           layer=layer, tq=256, hg=2, final=final)

    y_p, pool_p, ckv_p, krope_p = trunk(
        x_prompt, prev_p, mem_k_p, mem_v_p, 0, tag="p", nb=1, tq=256, emit_kt=True, n_valid=256,
        attend=attend_prompt)

    xs = jnp.pad(x_sample, ((0, 0), (0, TP - ST), (0, 0)))
    prev_s = jnp.pad(state_pool, ((0, 0), (0, 0), (HALO - POOL_BUF, 0), (0, 0)))
    mem_k_s = cache_mem_k.reshape(DEPTH, SB, MEM_ROWS, MEM_HEAD_DIM)
    mem_v_s = cache_mem_v.reshape(DEPTH, SB, MEM_ROWS, MEM_HEAD_DIM)
    cache_krt = jnp.transpose(cache_krope, (0, 2, 1))

    def attend_sample(x, q, zrest, kcat, *, layer, final):
        qf = q.reshape(SB, MLA_HEADS * TP, KCAT)
        knew = jnp.pad(kcat, ((0, 0), (0, LANES - TP), (0, 0)))
        o = _paged_attention(page_table, qf, knew, cache_ckv, cache_krt, cp=8, tp=TP, n_new=ST)
        tok = _v_up(o.reshape(SB, MLA_HEADS, TP, KV_RANK), w_vup16)
        return _b_post(x, tok, zrest, mem_k_s, mem_v_s, w_out16[layer], g_final2, layer=layer, nb=8, tq=TP,
                       final=final, tag="s")

    y_s, pool_s, ckv_s, krope_s = trunk(
        xs, prev_s, mem_k_s, mem_v_s, past, tag="s", nb=8, tq=TP, emit_kt=False, n_valid=ST,
        attend=attend_sample)

    mem_shape = (DEPTH, B, MEM_TOKENS, MEM_HEADS, MEM_HEAD_DIM)
    return (y_p, y_s[:, :ST], pool_p, pool_s, ckv_p, krope_p, ckv_s[:, :ST], krope_s[:, :ST],
            mem_k_p.reshape(mem_shape), mem_v_p.reshape(mem_shape))
```

```python
import functools

import jax
import jax.numpy as jnp
from jax import lax
from jax.experimental import pallas as pl
from jax.experimental.pallas import tpu as pltpu

F32 = jnp.float32
BF16 = jnp.bfloat16

D_MODEL = 1024
DEPTH = 4
N_A = 2
POOL_WINDOWS = (2, 4, 8, 16)
POOL_GROUP = 256
POOL_BUF = 15
MLA_HEADS = 8
QK_NOPE = 128
QK_ROPE = 64
V_HEAD = 128
KV_RANK = 256
Q_RANK = 384
MLA_WIDTH = MLA_HEADS * V_HEAD
MLA_SCALE = (QK_NOPE + QK_ROPE) ** -0.5
ROPE_THETA = 10000.0
MEM_TOKENS = 256
MEM_HEADS = 4
MEM_HEAD_DIM = 128
MEM_WIDTH = MEM_HEADS * MEM_HEAD_DIM
MEM_SCALE = MEM_HEAD_DIM ** -0.5
EPS = 1e-6
NEG = -1e30
LOG2E = 1.4426950408889634

KCAT = 384
SUBLANES = 8
LANES = 128
HALO = 16
E_OFF = SUBLANES + HALO
VMEM_LIMIT = 56 * 1024 * 1024
MEM_ROWS = MEM_TOKENS * MEM_HEADS
N_RAW = 4


def _cparams(n_axes):
    return pltpu.CompilerParams(dimension_semantics=("arbitrary",) * n_axes,
                                vmem_limit_bytes=VMEM_LIMIT)


def _rms(x, g):
    return x * lax.rsqrt(jnp.mean(x * x, axis=-1, keepdims=True) + EPS) * g


def _dot(a, b):
    return jnp.dot(a, b, preferred_element_type=F32)


def _dot_nt(a, b):
    return lax.dot_general(a, b, (((1,), (1,)), ((), ())), preferred_element_type=F32)


def _silu(x):
    return x * jax.nn.sigmoid(x)


def _rope_lanes(a, cos, sin):
    lane = lax.broadcasted_iota(jnp.int32, a.shape, 1)
    up = pltpu.roll(a, 32, 1)
    dn = pltpu.roll(a, 96, 1)
    rot = jnp.where((lane & 63) < 32, -dn, up)
    return a * cos + rot * sin


def _rep(x, n):
    return x if n == 1 else jnp.concatenate([x] * n, axis=1)


def _tile_rows(x, n):
    return x if n == 1 else jnp.concatenate([x] * n, axis=0)


def _mem_attend(qm, mk_ref, mv_ref, nb, tq):
    rows, keys = nb * tq, nb * MEM_TOKENS
    if nb > 1:
        row_seq = lax.broadcasted_iota(jnp.int32, (rows, keys), 0) // tq
        col_seq = lax.broadcasted_iota(jnp.int32, (rows, keys), 1) // MEM_TOKENS
        keep = row_seq == col_seq
    outs = []
    for h in range(MEM_HEADS):
        c0, c1 = h * MEM_HEAD_DIM, (h + 1) * MEM_HEAD_DIM
        q = (qm[:, c0:c1] * (MEM_SCALE * LOG2E)).astype(BF16)
        k = mk_ref[0, :, pl.ds(h, MEM_TOKENS, stride=MEM_HEADS), :].reshape(keys, MEM_HEAD_DIM).astype(BF16)
        v = mv_ref[0, :, pl.ds(h, MEM_TOKENS, stride=MEM_HEADS), :].reshape(keys, MEM_HEAD_DIM).astype(BF16)
        s = _dot_nt(q, k)
        if nb > 1:
            s = jnp.where(keep, s, NEG)
        m = jnp.max(s, axis=-1, keepdims=True)
        p = jnp.exp2(s - m)
        l = jnp.sum(p, axis=-1, keepdims=True)
        outs.append(_dot(p.astype(BF16), v) * (1.0 / l))
    return outs


def _mem_proj_kernel(mem_ref, g_ref, wk_ref, wv_ref, mk_ref, mv_ref):
    x = mem_ref[0]
    xn = x * lax.rsqrt(jnp.mean(x * x, axis=-1, keepdims=True) + EPS)
    for l in range(DEPTH):
        h = (xn * g_ref[l:l + 1, :]).astype(BF16)
        mk = _dot(h, wk_ref[l])
        mv = _dot(h, wv_ref[l])
        for hd in range(MEM_HEADS):
            c0, c1 = hd * MEM_HEAD_DIM, (hd + 1) * MEM_HEAD_DIM
            mk_ref[l, 0, pl.ds(hd, MEM_TOKENS, stride=MEM_HEADS), :] = mk[:, c0:c1]
            mv_ref[l, 0, pl.ds(hd, MEM_TOKENS, stride=MEM_HEADS), :] = mv[:, c0:c1]


def _mem_project(mem, g_mem, wk, wv):
    B, M, _ = mem.shape
    out = jax.ShapeDtypeStruct((DEPTH, B, MEM_ROWS, MEM_HEAD_DIM), F32)
    return pl.pallas_call(
        _mem_proj_kernel,
        grid=(B,),
        in_specs=[
            pl.BlockSpec((1, M, D_MODEL), lambda b: (b, 0, 0)),
            pl.BlockSpec((DEPTH, D_MODEL), lambda b: (0, 0)),
            pl.BlockSpec((DEPTH, D_MODEL, MEM_WIDTH), lambda b: (0, 0, 0)),
            pl.BlockSpec((DEPTH, D_MODEL, MEM_WIDTH), lambda b: (0, 0, 0)),
        ],
        out_specs=[
            pl.BlockSpec((DEPTH, 1, MEM_ROWS, MEM_HEAD_DIM), lambda b: (0, b, 0, 0)),
            pl.BlockSpec((DEPTH, 1, MEM_ROWS, MEM_HEAD_DIM), lambda b: (0, b, 0, 0)),
        ],
        out_shape=[out, out],
        compiler_params=_cparams(1),
        name="mem_project",
    )(mem, g_mem, wk, wv)


def _a_layer_kernel(x_ref, prev_ref, g_ref, win_ref, wgrp_ref, pscale_ref, mk_ref, mv_ref, wout_ref,
                    xo_ref, pool_ref, e_sc, sa_sc, sb_sc, *, nb, tq, n_t, n_valid, pos0):
    t = pl.program_id(1)
    rows = HALO + tq
    m = nb * tq

    @pl.when(t == 0)
    def _():
        zeros = jnp.zeros((nb, SUBLANES, D_MODEL), F32)
        e_sc[:, 0:SUBLANES, :] = zeros
        sa_sc[:, 0:SUBLANES, :] = zeros
        sb_sc[:, 0:SUBLANES, :] = zeros
        e_sc[:, SUBLANES:E_OFF, :] = prev_ref[...]

    x = x_ref[...].reshape(m, D_MODEL)
    h = _rms(x, g_ref[...]).astype(BF16)
    e_sc[:, E_OFF:E_OFF + tq, :] = _dot(h, win_ref[:, 0:1024]).reshape(nb, tq, D_MODEL)

    lo = SUBLANES
    sa_sc[:, lo:lo + rows, :] = e_sc[:, lo:lo + rows, :] + e_sc[:, lo - 1:lo - 1 + rows, :]
    sb_sc[:, lo:lo + rows, 256:1024] = (sa_sc[:, lo:lo + rows, 256:1024]
                                        + sa_sc[:, lo - 2:lo - 2 + rows, 256:1024])
    sa_sc[:, lo:lo + rows, 512:1024] = (sb_sc[:, lo:lo + rows, 512:1024]
                                        + sb_sc[:, lo - 4:lo - 4 + rows, 512:1024])
    s16 = sa_sc[:, E_OFF:E_OFF + tq, 768:1024] + sa_sc[:, E_OFF - 8:E_OFF - 8 + tq, 768:1024]
    sums = (sa_sc[:, E_OFF:E_OFF + tq, 0:256], sb_sc[:, E_OFF:E_OFF + tq, 256:512],
            sa_sc[:, E_OFF:E_OFF + tq, 512:768], s16)

    pos = pos0 + t * tq + lax.broadcasted_iota(jnp.int32, (1, tq, 1), 1)
    mixed_tok = []
    for g, w in enumerate(POOL_WINDOWS):
        c0, c1 = g * POOL_GROUP, (g + 1) * POOL_GROUP
        inv_cnt = 1.0 / jnp.minimum(pos + 1, w).astype(F32)
        pooled = (sums[g] * inv_cnt - e_sc[:, E_OFF:E_OFF + tq, c0:c1]).reshape(m, POOL_GROUP)
        mixed = _dot(pooled.astype(BF16), wgrp_ref[g]) * pscale_ref[:, c0:c1]
        gate = _dot(h, win_ref[:, 1024 + c0:1024 + c1])
        mixed_tok.append((mixed * _silu(gate)).astype(BF16))

    qm = _dot(h, win_ref[:, 2048:2560])
    gate_m = _dot(h, win_ref[:, 2560:3072])
    mem_o = _mem_attend(qm, mk_ref, mv_ref, nb, tq)

    acc = x
    for g in range(4):
        acc = acc + _dot(mixed_tok[g], wout_ref[g * 256:(g + 1) * 256, :])
    for hd in range(MEM_HEADS):
        c0, c1 = hd * MEM_HEAD_DIM, (hd + 1) * MEM_HEAD_DIM
        mm = (mem_o[hd] * _silu(gate_m[:, c0:c1])).astype(BF16)
        acc = acc + _dot(mm, wout_ref[1024 + c0:1024 + c1, :])
    xo_ref[...] = acc.reshape(nb, tq, D_MODEL)

    @pl.when(t == n_t - 1)
    def _():
        pool_ref[...] = e_sc[:, E_OFF + n_valid - HALO:E_OFF + n_valid, :]

    if n_t > 1:
        @pl.when(t < n_t - 1)
        def _():
            e_sc[:, SUBLANES:E_OFF, :] = e_sc[:, SUBLANES + tq:E_OFF + tq, :]


def _a_layer(x, prev, g, w_in, w_grp, pscale, mem_k, mem_v, w_out, *, layer, nb, tq, n_valid, pos0):
    B, T, _ = x.shape
    n_t = T // tq
    rows = E_OFF + tq
    kern = functools.partial(_a_layer_kernel, nb=nb, tq=tq, n_t=n_t, n_valid=n_valid, pos0=pos0)
    return pl.pallas_call(
        kern,
        grid=(B // nb, n_t),
        in_specs=[
            pl.BlockSpec((nb, tq, D_MODEL), lambda b, t: (b, t, 0)),
            pl.BlockSpec((nb, HALO, D_MODEL), lambda b, t: (b, 0, 0)),
            pl.BlockSpec((1, D_MODEL), lambda b, t: (0, 0)),
            pl.BlockSpec((D_MODEL, 3072), lambda b, t: (0, 0)),
            pl.BlockSpec((4, POOL_GROUP, POOL_GROUP), lambda b, t: (0, 0, 0)),
            pl.BlockSpec((1, D_MODEL), lambda b, t: (0, 0)),
            pl.BlockSpec((1, nb, MEM_ROWS, MEM_HEAD_DIM), lambda b, t: (layer, b, 0, 0)),
            pl.BlockSpec((1, nb, MEM_ROWS, MEM_HEAD_DIM), lambda b, t: (layer, b, 0, 0)),
            pl.BlockSpec((1536, D_MODEL), lambda b, t: (0, 0)),
        ],
        out_specs=[
            pl.BlockSpec((nb, tq, D_MODEL), lambda b, t: (b, t, 0)),
            pl.BlockSpec((nb, HALO, D_MODEL), lambda b, t: (b, 0, 0)),
        ],
        out_shape=[jax.ShapeDtypeStruct((B, T, D_MODEL), F32),
                   jax.ShapeDtypeStruct((B, HALO, D_MODEL), F32)],
        scratch_shapes=[pltpu.VMEM((nb, rows, D_MODEL), F32)] * 3,
        compiler_params=_cparams(2),
        name=f"a_layer{layer}_{B}",
    )(x, prev, g, w_in, w_grp, pscale, mem_k, mem_v, w_out)


def _kv_latent_kernel(x_ref, gin_ref, wkv_ref, glat_ref, cos_ref, sin_ref, wkupt_ref, wvup_ref,
                      ckv_ref, kr_ref, *out_refs, heads):
    nb, tq, _ = x_ref.shape
    m = nb * tq
    h = _rms(x_ref[...].reshape(m, D_MODEL), gin_ref[...]).astype(BF16)
    kv = _dot(h, wkv_ref[...])
    ckv = _rms(kv[:, 0:KV_RANK], glat_ref[...])
    kr = _rope_lanes(kv[:, KV_RANK:KCAT], _tile_rows(cos_ref[...], nb), _tile_rows(sin_ref[...], nb))
    ckv_ref[...] = ckv.reshape(nb, tq, KV_RANK)
    kr_ref[...] = kr[:, 0:QK_ROPE].reshape(nb, tq, QK_ROPE)
    if not heads:
        kcat_ref, = out_refs
        kcat_ref[:, :, 0:KV_RANK] = ckv.reshape(nb, tq, KV_RANK).astype(BF16)
        kcat_ref[:, :, KV_RANK:KCAT] = kr.reshape(nb, tq, LANES).astype(BF16)
    else:
        knt_ref, krt_ref, v_ref = out_refs
        ckv16 = ckv.astype(BF16)
        knt_ref[0, 0] = _dot(wkupt_ref[...], ckv.T.astype(BF16)).astype(BF16)
        krt_ref[0, 0] = kr.T.astype(BF16)
        v_ref[0] = _dot(ckv16, wvup_ref[...]).astype(BF16)


def _kv_latent(x, g_in, w_kv, g_lat, cos, sin, w_kupt, w_vup, *, nb, tq, heads):
    B, T, _ = x.shape
    assert nb == 1 or not heads
    out_specs = [
        pl.BlockSpec((nb, tq, KV_RANK), lambda b, t: (b, t, 0)),
        pl.BlockSpec((nb, tq, QK_ROPE), lambda b, t: (b, t, 0)),
    ]
    out_shape = [jax.ShapeDtypeStruct((B, T, KV_RANK), F32),
                 jax.ShapeDtypeStruct((B, T, QK_ROPE), F32)]
    if heads:
        out_specs += [pl.BlockSpec((1, 1, MLA_HEADS * QK_NOPE, tq), lambda b, t: (b, t, 0, 0)),
                      pl.BlockSpec((1, 1, LANES, tq), lambda b, t: (b, t, 0, 0)),
                      pl.BlockSpec((1, tq, MLA_WIDTH), lambda b, t: (b, t, 0))]
        out_shape += [jax.ShapeDtypeStruct((B, T // tq, MLA_HEADS * QK_NOPE, tq), BF16),
                      jax.ShapeDtypeStruct((B, T // tq, LANES, tq), BF16),
                      jax.ShapeDtypeStruct((B, T, MLA_WIDTH), BF16)]
    else:
        out_specs.append(pl.BlockSpec((nb, tq, KCAT), lambda b, t: (b, t, 0)))
        out_shape.append(jax.ShapeDtypeStruct((B, T, KCAT), BF16))
    return pl.pallas_call(
        functools.partial(_kv_latent_kernel, heads=heads),
        grid=(B // nb, T // tq),
        in_specs=[
            pl.BlockSpec((nb, tq, D_MODEL), lambda b, t: (b, t, 0)),
            pl.BlockSpec((1, D_MODEL), lambda b, t: (0, 0)),
            pl.BlockSpec((D_MODEL, KCAT), lambda b, t: (0, 0)),
            pl.BlockSpec((1, KV_RANK), lambda b, t: (0, 0)),
            pl.BlockSpec((tq, LANES), lambda b, t: (t, 0)),
            pl.BlockSpec((tq, LANES), lambda b, t: (t, 0)),
            pl.BlockSpec((MLA_HEADS * QK_NOPE, KV_RANK), lambda b, t: (0, 0)),
            pl.BlockSpec((KV_RANK, MLA_WIDTH), lambda b, t: (0, 0)),
        ],
        out_specs=out_specs,
        out_shape=out_shape,
        compiler_params=_cparams(2),
        name=f"kv_latent_{B}",
    )(x, g_in, w_kv, g_lat, cos, sin, w_kupt, w_vup)


def _b_pre_kernel(x_ref, g_ref, wcq_ref, wrest_ref, gq_ref, wqn_ref, wqr_ref, wkup_ref, cos_ref, sin_ref,
                  q_ref, zrest_ref, *, absorb):
    nb, tq, _ = x_ref.shape
    m = nb * tq
    n0 = KV_RANK if absorb else QK_NOPE
    h = _rms(x_ref[...].reshape(m, D_MODEL), g_ref[...]).astype(BF16)
    zrest_ref[...] = _dot(h, wrest_ref[...]).reshape(nb, tq, 2048)
    cn = _rms(_dot(h, wcq_ref[...]), gq_ref[...]).astype(BF16)
    qn = _dot(cn, wqn_ref[...])
    qr = _dot(cn, wqr_ref[...])
    for hd in range(MLA_HEADS):
        ql = qn[:, hd * QK_NOPE:(hd + 1) * QK_NOPE]
        if absorb:
            ql = _dot(ql.astype(BF16), wkup_ref[hd])
        q_ref[:, hd, :, 0:n0] = (ql * (MLA_SCALE * LOG2E)).reshape(nb, tq, n0).astype(BF16)
    cos = _tile_rows(cos_ref[...], nb)
    sin = _tile_rows(sin_ref[...], nb)
    lane = lax.broadcasted_iota(jnp.int32, cos.shape, 1)
    for j in range(MLA_HEADS // 2):
        rr = _rope_lanes(qr[:, j * LANES:(j + 1) * LANES], cos, sin) * (MLA_SCALE * LOG2E)
        even = jnp.where(lane < QK_ROPE, rr, 0.0)
        odd = jnp.where(lane < QK_ROPE, pltpu.roll(rr, 64, 1), 0.0)
        q_ref[:, 2 * j, :, n0:n0 + LANES] = even.reshape(nb, tq, LANES).astype(BF16)
        q_ref[:, 2 * j + 1, :, n0:n0 + LANES] = odd.reshape(nb, tq, LANES).astype(BF16)


def _b_pre(x, g, w_cq, w_rest, g_q, w_qn, w_qr, w_kup, cos, sin, *, nb, tq, absorb, tag):
    B, T, _ = x.shape
    const2 = lambda b, t: (0, 0)
    qw = (KV_RANK if absorb else QK_NOPE) + LANES
    return pl.pallas_call(
        functools.partial(_b_pre_kernel, absorb=absorb),
        grid=(B // nb, T // tq),
        in_specs=[
            pl.BlockSpec((nb, tq, D_MODEL), lambda b, t: (b, t, 0)),
            pl.BlockSpec((1, D_MODEL), const2),
            pl.BlockSpec((D_MODEL, Q_RANK), const2),
            pl.BlockSpec((D_MODEL, 2048), const2),
            pl.BlockSpec((1, Q_RANK), const2),
            pl.BlockSpec((Q_RANK, MLA_HEADS * QK_NOPE), const2),
            pl.BlockSpec((Q_RANK, MLA_HEADS * QK_ROPE), const2),
            pl.BlockSpec((MLA_HEADS, QK_NOPE, KV_RANK), lambda b, t: (0, 0, 0)),
            pl.BlockSpec((tq, LANES), lambda b, t: (t, 0)),
            pl.BlockSpec((tq, LANES), lambda b, t: (t, 0)),
        ],
        out_specs=[
            pl.BlockSpec((nb, MLA_HEADS, tq, qw), lambda b, t: (b, 0, t, 0)),
            pl.BlockSpec((nb, tq, 2048), lambda b, t: (b, t, 0)),
        ],
        out_shape=[jax.ShapeDtypeStruct((B, MLA_HEADS, T, qw), BF16),
                   jax.ShapeDtypeStruct((B, T, 2048), F32)],
        compiler_params=_cparams(2),
        name=f"b_pre_{tag}",
    )(x, g, w_cq, w_rest, g_q, w_qn, w_qr, w_kup, cos, sin)


def _flash_kernel(q_ref, knt_ref, krt_ref, v_ref, x_ref, z_ref, mk_ref, mv_ref, wout_ref, gf_ref, xo_ref,
                  m_sc, l_sc, acc_sc, tok_sc, *, tq, tk, final):
    i = pl.program_id(1)
    m_sc[...] = jnp.full(m_sc.shape, NEG, F32)
    l_sc[...] = jnp.zeros(l_sc.shape, F32)
    acc_sc[...] = jnp.zeros(acc_sc.shape, F32)

    def block(j, masked):
        krt = krt_ref[0, j]
        rows = pl.ds(pl.multiple_of(j * tk, tk), tk)
        if masked:
            keep = (lax.broadcasted_iota(jnp.int32, (tq, tk), 1)
                    <= lax.broadcasted_iota(jnp.int32, (tq, tk), 0))

        def scores(hd):
            kt = jnp.concatenate([knt_ref[0, j, hd * QK_NOPE:(hd + 1) * QK_NOPE, :], krt], axis=0)
            return _dot(q_ref[0, hd], kt)

        s_next = scores(0)
        for hd in range(MLA_HEADS):
            s = s_next
            if hd + 1 < MLA_HEADS:
                s_next = scores(hd + 1)
            if masked:
                s = jnp.where(keep, s, NEG)
            m_prev = m_sc[hd]
            m_new = jnp.maximum(m_prev, jnp.max(s, axis=-1, keepdims=True))
            alpha = jnp.exp2(m_prev - m_new)
            p = jnp.exp2(s - _rep(m_new, tk // LANES))
            p_lanes = p[:, 0:LANES]
            for c in range(1, tk // LANES):
                p_lanes = p_lanes + p[:, c * LANES:(c + 1) * LANES]
            l_sc[hd] = alpha * l_sc[hd] + p_lanes
            v = v_ref[0, rows, hd * V_HEAD:(hd + 1) * V_HEAD]
            acc_sc[hd] = alpha * acc_sc[hd] + _dot(p.astype(BF16), v)
            m_sc[hd] = m_new

    def body(j, carry):
        block(j, False)
        return carry

    lax.fori_loop(0, i, body, 0)
    block(i, True)

    for hd in range(MLA_HEADS):
        l = jnp.sum(l_sc[hd], axis=-1, keepdims=True)
        tok_sc[0, :, hd * V_HEAD:(hd + 1) * V_HEAD] = acc_sc[hd] * (1.0 / l)

    _mix_out(x_ref, tok_sc, z_ref, mk_ref, mv_ref, wout_ref, gf_ref, xo_ref, final)


def _flash_post(q, knt, krt, v, x, zrest, mem_k, mem_v, w_out, g_final, *, layer, tq, final):
    B, H, T, qw = q.shape
    tk = knt.shape[-1]
    assert tq == tk and qw == QK_NOPE + LANES
    kern = functools.partial(_flash_kernel, tq=tq, tk=tk, final=final)
    return pl.pallas_call(
        kern,
        grid=(B, T // tq),
        in_specs=[
            pl.BlockSpec((1, H, tq, qw), lambda b, i: (b, 0, i, 0)),
            pl.BlockSpec((1, T // tk, H * QK_NOPE, tk), lambda b, i: (b, 0, 0, 0)),
            pl.BlockSpec((1, T // tk, LANES, tk), lambda b, i: (b, 0, 0, 0)),
            pl.BlockSpec((1, T, MLA_WIDTH), lambda b, i: (b, 0, 0)),
            pl.BlockSpec((1, tq, D_MODEL), lambda b, i: (b, i, 0)),
            pl.BlockSpec((1, tq, 2048), lambda b, i: (b, i, 0)),
            pl.BlockSpec((1, 1, MEM_ROWS, MEM_HEAD_DIM), lambda b, i: (layer, b, 0, 0)),
            pl.BlockSpec((1, 1, MEM_ROWS, MEM_HEAD_DIM), lambda b, i: (layer, b, 0, 0)),
            pl.BlockSpec((1536, D_MODEL), lambda b, i: (0, 0)),
            pl.BlockSpec((1, D_MODEL), lambda b, i: (0, 0)),
        ],
        out_specs=pl.BlockSpec((1, tq, D_MODEL), lambda b, i: (b, i, 0)),
        out_shape=jax.ShapeDtypeStruct((B, T, D_MODEL), F32),
        scratch_shapes=[pltpu.VMEM((H, tq, LANES), F32), pltpu.VMEM((H, tq, LANES), F32),
                        pltpu.VMEM((H, tq, V_HEAD), F32), pltpu.VMEM((1, tq, MLA_WIDTH), F32)],
        compiler_params=_cparams(2),
        name=f"flash_post{layer}",
    )(q, knt, krt, v, x, zrest, mem_k, mem_v, w_out, g_final)


def _paged_kernel(pt_ref, q_ref, knew_ref, ckv_hbm, krt_hbm, o_ref,
                  ckv_buf, krt_buf, kbf_sc, s_sc, sem, m_sc, l_sc, acc_sc,
                  *, n_b, n_ch, cp, page, tp, n_new):
    total = n_b * n_ch

    def copies(bc):
        slot = bc % N_RAW
        b = bc // n_ch
        c = bc % n_ch
        out = []
        for p in range(cp):
            pg = pt_ref[b, c * cp + p]
            out.append(pltpu.make_async_copy(ckv_hbm.at[pg], ckv_buf.at[slot, pl.ds(p * page, page)],
                                             sem.at[0, slot]))
            out.append(pltpu.make_async_copy(krt_hbm.at[pg], krt_buf.at[slot, p], sem.at[1, slot]))
        return out

    def score(bc, w):
        slot = bc % N_RAW
        b = bc // n_ch
        for cpy in copies(bc):
            cpy.wait()
        q_lat = q_ref[b, :, 0:KV_RANK]
        q_rope = q_ref[b, :, KV_RANK:KV_RANK + QK_ROPE]
        kc = ckv_buf[slot].astype(BF16)
        krt = jnp.concatenate([krt_buf[slot, p] for p in range(cp)], axis=1).astype(BF16)
        kbf_sc[w] = kc
        s_sc[w] = _dot_nt(q_lat, kc) + _dot(q_rope, krt)

    def update(s, v_bf16, first):
        m_prev = jnp.where(first, NEG, m_sc[...])
        l_prev = jnp.where(first, 0.0, l_sc[...])
        acc_prev = jnp.where(first, 0.0, acc_sc[...])
        m_new = jnp.maximum(m_prev, jnp.max(s, axis=-1, keepdims=True))
        alpha = jnp.exp2(m_prev - m_new)
        p = jnp.exp2(s - _rep(m_new, s.shape[1] // LANES))
        l_sc[...] = alpha * l_prev + jnp.sum(p, axis=-1, keepdims=True)
        acc_sc[...] = _rep(alpha, KV_RANK // LANES) * acc_prev + _dot(p.astype(BF16), v_bf16)
        m_sc[...] = m_new

    def attend(bc, w):
        b = bc // n_ch
        c = bc % n_ch
        update(s_sc[w], kbf_sc[w], c == 0)

        def finish():
            kn = knew_ref[b]
            sn = _dot_nt(q_ref[b], kn)
            tpos = lax.broadcasted_iota(jnp.int32, sn.shape, 0) & (tp - 1)
            col = lax.broadcasted_iota(jnp.int32, sn.shape, 1)
            sn = jnp.where((col <= tpos) & (col < n_new), sn, NEG)
            update(sn, kn[:, 0:KV_RANK], False)
            o_ref[b] = acc_sc[...] * _rep(1.0 / l_sc[...], KV_RANK // LANES)

        last = c == n_ch - 1
        if isinstance(last, bool):
            if last:
                finish()
        else:
            pl.when(last)(finish)

    m_sc[...] = jnp.full(m_sc.shape, NEG, F32)
    l_sc[...] = jnp.zeros(l_sc.shape, F32)
    acc_sc[...] = jnp.zeros(acc_sc.shape, F32)
    for bc in range(min(N_RAW, total)):
        for cpy in copies(bc):
            cpy.start()
    score(0, 0)

    def step(i, w):
        def prefetch():
            for cpy in copies(i + N_RAW):
                cpy.start()

        more = i + N_RAW < total
        if isinstance(more, bool):
            if more:
                prefetch()
        else:
            pl.when(more)(prefetch)
        if isinstance(i, int) and i + 1 >= total:
            attend(i, w)
            return
        score(i + 1, 1 - w)
        attend(i, w)

    def body(k, carry):
        step(2 * k, 0)
        step(2 * k + 1, 1)
        return carry

    n_pairs = (total - 2) // 2
    lax.fori_loop(0, n_pairs, body, 0)
    for i in range(2 * n_pairs, total):
        step(i, i % 2)


def _paged_attention(page_table, q, knew, cache_ckv, cache_krt, *, cp, tp, n_new):
    n_b, n_pages = page_table.shape
    page = cache_ckv.shape[1]
    rows = q.shape[1]
    n_ch = n_pages // cp
    kern = functools.partial(_paged_kernel, n_b=n_b, n_ch=n_ch, cp=cp, page=page, tp=tp, n_new=n_new)
    grid_spec = pltpu.PrefetchScalarGridSpec(
        num_scalar_prefetch=1,
        grid=(1,),
        in_specs=[
            pl.BlockSpec(q.shape, lambda i, pt: (0, 0, 0)),
            pl.BlockSpec(knew.shape, lambda i, pt: (0, 0, 0)),
            pl.BlockSpec(memory_space=pl.ANY),
            pl.BlockSpec(memory_space=pl.ANY),
        ],
        out_specs=pl.BlockSpec((n_b, rows, KV_RANK), lambda i, pt: (0, 0, 0)),
        scratch_shapes=[
            pltpu.VMEM((N_RAW, cp * page, KV_RANK), F32),
            pltpu.VMEM((N_RAW, cp, QK_ROPE, page), F32),
            pltpu.VMEM((2, cp * page, KV_RANK), BF16),
            pltpu.VMEM((2, rows, cp * page), F32),
            pltpu.SemaphoreType.DMA((2, N_RAW)),
            pltpu.VMEM((rows, LANES), F32),
            pltpu.VMEM((rows, LANES), F32),
            pltpu.VMEM((rows, KV_RANK), F32),
        ],
    )
    return pl.pallas_call(
        kern,
        grid_spec=grid_spec,
        out_shape=jax.ShapeDtypeStruct((n_b, rows, KV_RANK), F32),
        compiler_params=_cparams(1),
        name="paged_attention",
    )(page_table, q, knew, cache_ckv, cache_krt)


def _v_up_kernel(o_ref, wv_ref, tok_ref):
    n_b, _, tp, _ = o_ref.shape
    for hd in range(MLA_HEADS):
        oh = o_ref[:, hd].reshape(n_b * tp, KV_RANK).astype(BF16)
        tok_ref[:, :, hd * V_HEAD:(hd + 1) * V_HEAD] = _dot(oh, wv_ref[hd]).reshape(n_b, tp, V_HEAD)


def _v_up(o, w_vup):
    n_b, H, tp, _ = o.shape
    return pl.pallas_call(
        _v_up_kernel,
        out_shape=jax.ShapeDtypeStruct((n_b, tp, MLA_WIDTH), F32),
        compiler_params=pltpu.CompilerParams(vmem_limit_bytes=VMEM_LIMIT),
        name="v_up_sample",
    )(o, w_vup)


def _b_post_kernel(x_ref, tok_ref, z_ref, mk_ref, mv_ref, wout_ref, gf_ref, xo_ref, *, final):
    _mix_out(x_ref, tok_ref, z_ref, mk_ref, mv_ref, wout_ref, gf_ref, xo_ref, final)


def _mix_out(x_ref, tok_ref, z_ref, mk_ref, mv_ref, wout_ref, gf_ref, xo_ref, final):
    nb, tq, _ = x_ref.shape
    m = nb * tq
    x = x_ref[...].reshape(m, D_MODEL)
    acc = x
    for g in range(4):
        c0, c1 = g * 256, (g + 1) * 256
        mt = (tok_ref[:, :, c0:c1] * _silu(z_ref[:, :, c0:c1])).reshape(m, 256).astype(BF16)
        acc = acc + _dot(mt, wout_ref[c0:c1, :])
    mem_o = _mem_attend(z_ref[:, :, 1024:1536].reshape(m, MEM_WIDTH), mk_ref, mv_ref, nb, tq)
    for hd in range(MEM_HEADS):
        c0, c1 = hd * MEM_HEAD_DIM, (hd + 1) * MEM_HEAD_DIM
        gate = z_ref[:, :, 1536 + c0:1536 + c1].reshape(m, MEM_HEAD_DIM)
        mm = (mem_o[hd] * _silu(gate)).astype(BF16)
        acc = acc + _dot(mm, wout_ref[1024 + c0:1024 + c1, :])
    if final:
        acc = _rms(acc, gf_ref[...])
    xo_ref[...] = acc.reshape(nb, tq, D_MODEL)


def _b_post(x, tok, zrest, mem_k, mem_v, w_out, g_final, *, layer, nb, tq, final, tag):
    B, T, _ = x.shape
    kern = functools.partial(_b_post_kernel, final=final)
    return pl.pallas_call(
        kern,
        grid=(B // nb, T // tq),
        in_specs=[
            pl.BlockSpec((nb, tq, D_MODEL), lambda b, t: (b, t, 0)),
            pl.BlockSpec((nb, tq, MLA_WIDTH), lambda b, t: (b, t, 0)),
            pl.BlockSpec((nb, tq, 2048), lambda b, t: (b, t, 0)),
            pl.BlockSpec((1, nb, MEM_ROWS, MEM_HEAD_DIM), lambda b, t: (layer, b, 0, 0)),
            pl.BlockSpec((1, nb, MEM_ROWS, MEM_HEAD_DIM), lambda b, t: (layer, b, 0, 0)),
            pl.BlockSpec((1536, D_MODEL), lambda b, t: (0, 0)),
            pl.BlockSpec((1, D_MODEL), lambda b, t: (0, 0)),
        ],
        out_specs=pl.BlockSpec((nb, tq, D_MODEL), lambda b, t: (b, t, 0)),
        out_shape=jax.ShapeDtypeStruct((B, T, D_MODEL), F32),
        compiler_params=_cparams(2),
        name=f"b_post{layer}_{tag}",
    )(x, tok, zrest, mem_k, mem_v, w_out, g_final)


def _rope_tables(pos):
    half = QK_ROPE // 2
    inv = ROPE_THETA ** (-jnp.arange(half, dtype=F32) / half)
    ang = pos.astype(F32)[:, None] * inv[None, :]
    return jnp.tile(jnp.cos(ang), (1, LANES // half)), jnp.tile(jnp.sin(ang), (1, LANES // half))


def kernel(x_prompt, x_sample, state_pool, cache_ckv, cache_krope, cache_mem_k, cache_mem_v, page_table, mem_prompt, g_norm, w_in_a, w_pool_grp, pool_scale, w_in_b, g_q_latent, w_q_up, g_kv_in, w_kv_down, g_kv_latent, w_k_up, w_v_up, g_mem, w_mem_k, w_mem_v, w_out, g_final):
    B, T, _ = x_prompt.shape
    SB, ST, _ = x_sample.shape
    TP = SUBLANES
    past = page_table.shape[1] * cache_ckv.shape[1]

    w_in_a16 = w_in_a.astype(BF16)
    w_grp16 = w_pool_grp.astype(BF16)
    w_out16 = w_out.astype(BF16)
    w_cq16 = w_in_b[:, :, :Q_RANK].astype(BF16)
    w_rest16 = w_in_b[:, :, Q_RANK:].astype(BF16)
    wq = w_q_up.reshape(N_A, Q_RANK, MLA_HEADS, QK_NOPE + QK_ROPE)
    w_qn16 = wq[..., :QK_NOPE].reshape(N_A, Q_RANK, MLA_HEADS * QK_NOPE).astype(BF16)
    w_qr16 = wq[..., QK_NOPE:].reshape(N_A, Q_RANK, MLA_HEADS * QK_ROPE).astype(BF16)
    w_kup16 = jnp.transpose(w_k_up, (1, 2, 0)).astype(BF16)
    w_vup16 = jnp.transpose(w_v_up, (1, 0, 2)).astype(BF16)
    w_kv16 = jnp.pad(w_kv_down, ((0, 0), (0, KCAT - KV_RANK - QK_ROPE))).astype(BF16)
    w_mk16 = w_mem_k.astype(BF16)
    w_mv16 = w_mem_v.astype(BF16)
    g_kv_in2 = g_kv_in[None]
    g_lat2 = g_kv_latent[None]
    g_final2 = g_final[None]

    w_kupt16 = w_kup16.reshape(MLA_HEADS * QK_NOPE, KV_RANK)
    w_vall16 = w_v_up.reshape(KV_RANK, MLA_WIDTH).astype(BF16)

    def trunk(x, prev, mem_k, mem_v, pos, *, tag, nb, tq, heads, n_valid, attend):
        pools = []
        for l in range(N_A):
            x, pool = _a_layer(x, prev[l], g_norm[l][None], w_in_a16[l], w_grp16[l], pool_scale[l][None],
                               mem_k, mem_v, w_out16[l], layer=l, nb=nb, tq=tq, n_valid=n_valid, pos0=pos)
            pools.append(pool[:, 1:])
        cos, sin = _rope_tables(pos + jnp.arange(x.shape[1], dtype=jnp.int32))
        ckv, krope, *keys = _kv_latent(x, g_kv_in2, w_kv16, g_lat2, cos, sin, w_kupt16, w_vall16,
                                       nb=nb, tq=tq, heads=heads)
        for j in range(DEPTH - N_A):
            l = N_A + j
            q, zrest = _b_pre(x, g_norm[l][None], w_cq16[j], w_rest16[j], g_q_latent[j][None],
                              w_qn16[j], w_qr16[j], w_kup16, cos, sin, nb=nb, tq=tq, absorb=not heads,
                              tag=f"{tag}{l}")
            x = attend(x, q, zrest, *keys, layer=l, final=(l == DEPTH - 1))
        return x, jnp.stack(pools, axis=0), ckv, krope

    mem_k_p, mem_v_p = _mem_project(mem_prompt, g_mem, w_mk16, w_mv16)
    prev_p = jnp.zeros((N_A, B, HALO, D_MODEL), F32)

    def attend_prompt(x, q, zrest, knt, krt, v, *, layer, final):
        return _flash_post(q, knt, krt, v, x, zrest, mem_k_p, mem_v_p, w_out16[layer], g_final2,
                           layer=layer, tq=256, final=final)

    y_p, pool_p, ckv_p, krope_p = trunk(
        x_prompt, prev_p, mem_k_p, mem_v_p, 0, tag="p", nb=1, tq=256, heads=True, n_valid=256,
        attend=attend_prompt)

    xs = jnp.pad(x_sample, ((0, 0), (0, TP - ST), (0, 0)))
    prev_s = jnp.pad(state_pool, ((0, 0), (0, 0), (HALO - POOL_BUF, 0), (0, 0)))
    mem_k_s = cache_mem_k.reshape(DEPTH, SB, MEM_ROWS, MEM_HEAD_DIM)
    mem_v_s = cache_mem_v.reshape(DEPTH, SB, MEM_ROWS, MEM_HEAD_DIM)
    cache_krt = jnp.transpose(cache_krope, (0, 2, 1))

    def attend_sample(x, q, zrest, kcat, *, layer, final):
        qf = q.reshape(SB, MLA_HEADS * TP, KCAT)
        knew = jnp.pad(kcat, ((0, 0), (0, LANES - TP), (0, 0)))
        o = _paged_attention(page_table, qf, knew, cache_ckv, cache_krt, cp=8, tp=TP, n_new=ST)
        tok = _v_up(o.reshape(SB, MLA_HEADS, TP, KV_RANK), w_vup16)
        return _b_post(x, tok, zrest, mem_k_s, mem_v_s, w_out16[layer], g_final2, layer=layer, nb=8, tq=TP,
                       final=final, tag="s")

    y_s, pool_s, ckv_s, krope_s = trunk(
        xs, prev_s, mem_k_s, mem_v_s, past, tag="s", nb=8, tq=TP, heads=False, n_valid=ST,
        attend=attend_sample)

    mem_shape = (DEPTH, B, MEM_TOKENS, MEM_HEADS, MEM_HEAD_DIM)
    return (y_p, y_s[:, :ST], pool_p, pool_s, ckv_p, krope_p, ckv_s[:, :ST], krope_s[:, :ST],
            mem_k_p.reshape(mem_shape), mem_v_p.reshape(mem_shape))
```

```python
import functools

import jax
import jax.numpy as jnp
from jax import lax
from jax.experimental import pallas as pl
from jax.experimental.pallas import tpu as pltpu

F32 = jnp.float32
BF16 = jnp.bfloat16

D_MODEL = 1024
DEPTH = 4
N_A = 2
POOL_WINDOWS = (2, 4, 8, 16)
POOL_GROUP = 256
POOL_BUF = 15
MLA_HEADS = 8
QK_NOPE = 128
QK_ROPE = 64
V_HEAD = 128
KV_RANK = 256
Q_RANK = 384
MLA_WIDTH = MLA_HEADS * V_HEAD
MLA_SCALE = (QK_NOPE + QK_ROPE) ** -0.5
ROPE_THETA = 10000.0
MEM_TOKENS = 256
MEM_HEADS = 4
MEM_HEAD_DIM = 128
MEM_WIDTH = MEM_HEADS * MEM_HEAD_DIM
MEM_SCALE = MEM_HEAD_DIM ** -0.5
EPS = 1e-6
NEG = -1e30
LOG2E = 1.4426950408889634

KCAT = 384
SUBLANES = 8
LANES = 128
HALO = 16
E_OFF = SUBLANES + HALO
VMEM_LIMIT = 56 * 1024 * 1024
MEM_ROWS = MEM_TOKENS * MEM_HEADS
N_RAW = 4


def _cparams(n_axes):
    return pltpu.CompilerParams(dimension_semantics=("arbitrary",) * n_axes,
                                vmem_limit_bytes=VMEM_LIMIT)


def _rms(x, g):
    return x * lax.rsqrt(jnp.mean(x * x, axis=-1, keepdims=True) + EPS) * g


def _dot(a, b):
    return jnp.dot(a, b, preferred_element_type=F32)


def _dot_nt(a, b):
    return lax.dot_general(a, b, (((1,), (1,)), ((), ())), preferred_element_type=F32)


def _silu(x):
    return x * jax.nn.sigmoid(x)


def _rope_lanes(a, cos, sin):
    lane = lax.broadcasted_iota(jnp.int32, a.shape, 1)
    up = pltpu.roll(a, 32, 1)
    dn = pltpu.roll(a, 96, 1)
    rot = jnp.where((lane & 63) < 32, -dn, up)
    return a * cos + rot * sin


def _rep(x, n):
    return x if n == 1 else jnp.concatenate([x] * n, axis=1)


def _tile_rows(x, n):
    return x if n == 1 else jnp.concatenate([x] * n, axis=0)


def _mem_attend(qm, mk_ref, mv_ref, nb, tq):
    rows, keys = nb * tq, nb * MEM_TOKENS
    if nb > 1:
        row_seq = lax.broadcasted_iota(jnp.int32, (rows, keys), 0) // tq
        col_seq = lax.broadcasted_iota(jnp.int32, (rows, keys), 1) // MEM_TOKENS
        keep = row_seq == col_seq
    outs = []
    for h in range(MEM_HEADS):
        c0, c1 = h * MEM_HEAD_DIM, (h + 1) * MEM_HEAD_DIM
        q = (qm[:, c0:c1] * (MEM_SCALE * LOG2E)).astype(BF16)
        k = mk_ref[0, :, pl.ds(h, MEM_TOKENS, stride=MEM_HEADS), :].reshape(keys, MEM_HEAD_DIM).astype(BF16)
        v = mv_ref[0, :, pl.ds(h, MEM_TOKENS, stride=MEM_HEADS), :].reshape(keys, MEM_HEAD_DIM).astype(BF16)
        s = _dot_nt(q, k)
        if nb > 1:
            s = jnp.where(keep, s, NEG)
        m = jnp.max(s, axis=-1, keepdims=True)
        p = jnp.exp2(s - m)
        l = jnp.sum(p, axis=-1, keepdims=True)
        outs.append(_dot(p.astype(BF16), v) * (1.0 / l))
    return outs


def _mem_proj_kernel(mem_ref, g_ref, wk_ref, wv_ref, mk_ref, mv_ref):
    x = mem_ref[0]
    xn = x * lax.rsqrt(jnp.mean(x * x, axis=-1, keepdims=True) + EPS)
    for l in range(DEPTH):
        h = (xn * g_ref[l:l + 1, :]).astype(BF16)
        mk = _dot(h, wk_ref[l])
        mv = _dot(h, wv_ref[l])
        for hd in range(MEM_HEADS):
            c0, c1 = hd * MEM_HEAD_DIM, (hd + 1) * MEM_HEAD_DIM
            mk_ref[l, 0, pl.ds(hd, MEM_TOKENS, stride=MEM_HEADS), :] = mk[:, c0:c1]
            mv_ref[l, 0, pl.ds(hd, MEM_TOKENS, stride=MEM_HEADS), :] = mv[:, c0:c1]


def _mem_project(mem, g_mem, wk, wv):
    B, M, _ = mem.shape
    out = jax.ShapeDtypeStruct((DEPTH, B, MEM_ROWS, MEM_HEAD_DIM), F32)
    return pl.pallas_call(
        _mem_proj_kernel,
        grid=(B,),
        in_specs=[
            pl.BlockSpec((1, M, D_MODEL), lambda b: (b, 0, 0)),
            pl.BlockSpec((DEPTH, D_MODEL), lambda b: (0, 0)),
            pl.BlockSpec((DEPTH, D_MODEL, MEM_WIDTH), lambda b: (0, 0, 0)),
            pl.BlockSpec((DEPTH, D_MODEL, MEM_WIDTH), lambda b: (0, 0, 0)),
        ],
        out_specs=[
            pl.BlockSpec((DEPTH, 1, MEM_ROWS, MEM_HEAD_DIM), lambda b: (0, b, 0, 0)),
            pl.BlockSpec((DEPTH, 1, MEM_ROWS, MEM_HEAD_DIM), lambda b: (0, b, 0, 0)),
        ],
        out_shape=[out, out],
        compiler_params=_cparams(1),
        name="mem_project",
    )(mem, g_mem, wk, wv)


def _a_layer_kernel(x_ref, prev_ref, g_ref, win_ref, wgrp_ref, pscale_ref, mk_ref, mv_ref, wout_ref,
                    xo_ref, pool_ref, e_sc, sa_sc, sb_sc, *, nb, tq, n_t, n_valid, pos0):
    t = pl.program_id(1)
    rows = HALO + tq
    m = nb * tq

    @pl.when(t == 0)
    def _():
        zeros = jnp.zeros((nb, SUBLANES, D_MODEL), F32)
        e_sc[:, 0:SUBLANES, :] = zeros
        sa_sc[:, 0:SUBLANES, :] = zeros
        sb_sc[:, 0:SUBLANES, :] = zeros
        e_sc[:, SUBLANES:E_OFF, :] = prev_ref[...]

    x = x_ref[...].reshape(m, D_MODEL)
    h = _rms(x, g_ref[...]).astype(BF16)
    e_sc[:, E_OFF:E_OFF + tq, :] = _dot(h, win_ref[:, 0:1024]).reshape(nb, tq, D_MODEL)

    lo = SUBLANES
    sa_sc[:, lo:lo + rows, :] = e_sc[:, lo:lo + rows, :] + e_sc[:, lo - 1:lo - 1 + rows, :]
    sb_sc[:, lo:lo + rows, 256:1024] = (sa_sc[:, lo:lo + rows, 256:1024]
                                        + sa_sc[:, lo - 2:lo - 2 + rows, 256:1024])
    sa_sc[:, lo:lo + rows, 512:1024] = (sb_sc[:, lo:lo + rows, 512:1024]
                                        + sb_sc[:, lo - 4:lo - 4 + rows, 512:1024])
    s16 = sa_sc[:, E_OFF:E_OFF + tq, 768:1024] + sa_sc[:, E_OFF - 8:E_OFF - 8 + tq, 768:1024]
    sums = (sa_sc[:, E_OFF:E_OFF + tq, 0:256], sb_sc[:, E_OFF:E_OFF + tq, 256:512],
            sa_sc[:, E_OFF:E_OFF + tq, 512:768], s16)

    pos = pos0 + t * tq + lax.broadcasted_iota(jnp.int32, (1, tq, 1), 1)
    mixed_tok = []
    for g, w in enumerate(POOL_WINDOWS):
        c0, c1 = g * POOL_GROUP, (g + 1) * POOL_GROUP
        inv_cnt = 1.0 / jnp.minimum(pos + 1, w).astype(F32)
        pooled = (sums[g] * inv_cnt - e_sc[:, E_OFF:E_OFF + tq, c0:c1]).reshape(m, POOL_GROUP)
        mixed = _dot(pooled.astype(BF16), wgrp_ref[g]) * pscale_ref[:, c0:c1]
        gate = _dot(h, win_ref[:, 1024 + c0:1024 + c1])
        mixed_tok.append((mixed * _silu(gate)).astype(BF16))

    qm = _dot(h, win_ref[:, 2048:2560])
    gate_m = _dot(h, win_ref[:, 2560:3072])
    mem_o = _mem_attend(qm, mk_ref, mv_ref, nb, tq)

    acc = x
    for g in range(4):
        acc = acc + _dot(mixed_tok[g], wout_ref[g * 256:(g + 1) * 256, :])
    for hd in range(MEM_HEADS):
        c0, c1 = hd * MEM_HEAD_DIM, (hd + 1) * MEM_HEAD_DIM
        mm = (mem_o[hd] * _silu(gate_m[:, c0:c1])).astype(BF16)
        acc = acc + _dot(mm, wout_ref[1024 + c0:1024 + c1, :])
    xo_ref[...] = acc.reshape(nb, tq, D_MODEL)

    @pl.when(t == n_t - 1)
    def _():
        pool_ref[...] = e_sc[:, E_OFF + n_valid - HALO:E_OFF + n_valid, :]

    if n_t > 1:
        @pl.when(t < n_t - 1)
        def _():
            e_sc[:, SUBLANES:E_OFF, :] = e_sc[:, SUBLANES + tq:E_OFF + tq, :]


def _a_layer(x, prev, g, w_in, w_grp, pscale, mem_k, mem_v, w_out, *, layer, nb, tq, n_valid, pos0):
    B, T, _ = x.shape
    n_t = T // tq
    rows = E_OFF + tq
    kern = functools.partial(_a_layer_kernel, nb=nb, tq=tq, n_t=n_t, n_valid=n_valid, pos0=pos0)
    return pl.pallas_call(
        kern,
        grid=(B // nb, n_t),
        in_specs=[
            pl.BlockSpec((nb, tq, D_MODEL), lambda b, t: (b, t, 0)),
            pl.BlockSpec((nb, HALO, D_MODEL), lambda b, t: (b, 0, 0)),
            pl.BlockSpec((1, D_MODEL), lambda b, t: (0, 0)),
            pl.BlockSpec((D_MODEL, 3072), lambda b, t: (0, 0)),
            pl.BlockSpec((4, POOL_GROUP, POOL_GROUP), lambda b, t: (0, 0, 0)),
            pl.BlockSpec((1, D_MODEL), lambda b, t: (0, 0)),
            pl.BlockSpec((1, nb, MEM_ROWS, MEM_HEAD_DIM), lambda b, t: (layer, b, 0, 0)),
            pl.BlockSpec((1, nb, MEM_ROWS, MEM_HEAD_DIM), lambda b, t: (layer, b, 0, 0)),
            pl.BlockSpec((1536, D_MODEL), lambda b, t: (0, 0)),
        ],
        out_specs=[
            pl.BlockSpec((nb, tq, D_MODEL), lambda b, t: (b, t, 0)),
            pl.BlockSpec((nb, HALO, D_MODEL), lambda b, t: (b, 0, 0)),
        ],
        out_shape=[jax.ShapeDtypeStruct((B, T, D_MODEL), F32),
                   jax.ShapeDtypeStruct((B, HALO, D_MODEL), F32)],
        scratch_shapes=[pltpu.VMEM((nb, rows, D_MODEL), F32)] * 3,
        compiler_params=_cparams(2),
        name=f"a_layer{layer}_{B}",
    )(x, prev, g, w_in, w_grp, pscale, mem_k, mem_v, w_out)


def _kv_latent_kernel(x_ref, gin_ref, wkv_ref, glat_ref, cos_ref, sin_ref, wkupt_ref, wvup_ref,
                      ckv_ref, kr_ref, *out_refs, heads):
    nb, tq, _ = x_ref.shape
    m = nb * tq
    h = _rms(x_ref[...].reshape(m, D_MODEL), gin_ref[...]).astype(BF16)
    kv = _dot(h, wkv_ref[...])
    ckv = _rms(kv[:, 0:KV_RANK], glat_ref[...])
    kr = _rope_lanes(kv[:, KV_RANK:KCAT], _tile_rows(cos_ref[...], nb), _tile_rows(sin_ref[...], nb))
    ckv_ref[...] = ckv.reshape(nb, tq, KV_RANK)
    kr_ref[...] = kr[:, 0:QK_ROPE].reshape(nb, tq, QK_ROPE)
    if not heads:
        kcat_ref, = out_refs
        kcat_ref[:, :, 0:KV_RANK] = ckv.reshape(nb, tq, KV_RANK).astype(BF16)
        kcat_ref[:, :, KV_RANK:KCAT] = kr.reshape(nb, tq, LANES).astype(BF16)
    else:
        knt_ref, krt_ref, v_ref = out_refs
        ckv16 = ckv.astype(BF16)
        knt_ref[0, 0] = _dot(wkupt_ref[...], ckv.T.astype(BF16)).astype(BF16)
        krt_ref[0, 0] = kr.T.astype(BF16)
        v_ref[0] = _dot(ckv16, wvup_ref[...]).astype(BF16)


def _kv_latent(x, g_in, w_kv, g_lat, cos, sin, w_kupt, w_vup, *, nb, tq, heads):
    B, T, _ = x.shape
    assert nb == 1 or not heads
    out_specs = [
        pl.BlockSpec((nb, tq, KV_RANK), lambda b, t: (b, t, 0)),
        pl.BlockSpec((nb, tq, QK_ROPE), lambda b, t: (b, t, 0)),
    ]
    out_shape = [jax.ShapeDtypeStruct((B, T, KV_RANK), F32),
                 jax.ShapeDtypeStruct((B, T, QK_ROPE), F32)]
    if heads:
        out_specs += [pl.BlockSpec((1, 1, MLA_HEADS * QK_NOPE, tq), lambda b, t: (b, t, 0, 0)),
                      pl.BlockSpec((1, 1, LANES, tq), lambda b, t: (b, t, 0, 0)),
                      pl.BlockSpec((1, tq, MLA_WIDTH), lambda b, t: (b, t, 0))]
        out_shape += [jax.ShapeDtypeStruct((B, T // tq, MLA_HEADS * QK_NOPE, tq), BF16),
                      jax.ShapeDtypeStruct((B, T // tq, LANES, tq), BF16),
                      jax.ShapeDtypeStruct((B, T, MLA_WIDTH), BF16)]
    else:
        out_specs.append(pl.BlockSpec((nb, tq, KCAT), lambda b, t: (b, t, 0)))
        out_shape.append(jax.ShapeDtypeStruct((B, T, KCAT), BF16))
    return pl.pallas_call(
        functools.partial(_kv_latent_kernel, heads=heads),
        grid=(B // nb, T // tq),
        in_specs=[
            pl.BlockSpec((nb, tq, D_MODEL), lambda b, t: (b, t, 0)),
            pl.BlockSpec((1, D_MODEL), lambda b, t: (0, 0)),
            pl.BlockSpec((D_MODEL, KCAT), lambda b, t: (0, 0)),
            pl.BlockSpec((1, KV_RANK), lambda b, t: (0, 0)),
            pl.BlockSpec((tq, LANES), lambda b, t: (t, 0)),
            pl.BlockSpec((tq, LANES), lambda b, t: (t, 0)),
            pl.BlockSpec((MLA_HEADS * QK_NOPE, KV_RANK), lambda b, t: (0, 0)),
            pl.BlockSpec((KV_RANK, MLA_WIDTH), lambda b, t: (0, 0)),
        ],
        out_specs=out_specs,
        out_shape=out_shape,
        compiler_params=_cparams(2),
        name=f"kv_latent_{B}",
    )(x, g_in, w_kv, g_lat, cos, sin, w_kupt, w_vup)


def _b_pre_kernel(x_ref, g_ref, wcq_ref, wrest_ref, gq_ref, wqn_ref, wqr_ref, wkup_ref, cos_ref, sin_ref,
                  q_ref, zrest_ref, *, absorb):
    nb, tq, _ = x_ref.shape
    m = nb * tq
    n0 = KV_RANK if absorb else QK_NOPE
    h = _rms(x_ref[...].reshape(m, D_MODEL), g_ref[...]).astype(BF16)
    zrest_ref[...] = _dot(h, wrest_ref[...]).reshape(nb, tq, 2048)
    cn = _rms(_dot(h, wcq_ref[...]), gq_ref[...]).astype(BF16)
    qn = _dot(cn, wqn_ref[...])
    qr = _dot(cn, wqr_ref[...])
    for hd in range(MLA_HEADS):
        ql = qn[:, hd * QK_NOPE:(hd + 1) * QK_NOPE]
        if absorb:
            ql = _dot(ql.astype(BF16), wkup_ref[hd])
        q_ref[:, hd, :, 0:n0] = (ql * (MLA_SCALE * LOG2E)).reshape(nb, tq, n0).astype(BF16)
    cos = _tile_rows(cos_ref[...], nb)
    sin = _tile_rows(sin_ref[...], nb)
    lane = lax.broadcasted_iota(jnp.int32, cos.shape, 1)
    for j in range(MLA_HEADS // 2):
        rr = _rope_lanes(qr[:, j * LANES:(j + 1) * LANES], cos, sin) * (MLA_SCALE * LOG2E)
        even = jnp.where(lane < QK_ROPE, rr, 0.0)
        odd = jnp.where(lane < QK_ROPE, pltpu.roll(rr, 64, 1), 0.0)
        q_ref[:, 2 * j, :, n0:n0 + LANES] = even.reshape(nb, tq, LANES).astype(BF16)
        q_ref[:, 2 * j + 1, :, n0:n0 + LANES] = odd.reshape(nb, tq, LANES).astype(BF16)


def _b_pre(x, g, w_cq, w_rest, g_q, w_qn, w_qr, w_kup, cos, sin, *, nb, tq, absorb, tag):
    B, T, _ = x.shape
    const2 = lambda b, t: (0, 0)
    qw = (KV_RANK if absorb else QK_NOPE) + LANES
    return pl.pallas_call(
        functools.partial(_b_pre_kernel, absorb=absorb),
        grid=(B // nb, T // tq),
        in_specs=[
            pl.BlockSpec((nb, tq, D_MODEL), lambda b, t: (b, t, 0)),
            pl.BlockSpec((1, D_MODEL), const2),
            pl.BlockSpec((D_MODEL, Q_RANK), const2),
            pl.BlockSpec((D_MODEL, 2048), const2),
            pl.BlockSpec((1, Q_RANK), const2),
            pl.BlockSpec((Q_RANK, MLA_HEADS * QK_NOPE), const2),
            pl.BlockSpec((Q_RANK, MLA_HEADS * QK_ROPE), const2),
            pl.BlockSpec((MLA_HEADS, QK_NOPE, KV_RANK), lambda b, t: (0, 0, 0)),
            pl.BlockSpec((tq, LANES), lambda b, t: (t, 0)),
            pl.BlockSpec((tq, LANES), lambda b, t: (t, 0)),
        ],
        out_specs=[
            pl.BlockSpec((nb, MLA_HEADS, tq, qw), lambda b, t: (b, 0, t, 0)),
            pl.BlockSpec((nb, tq, 2048), lambda b, t: (b, t, 0)),
        ],
        out_shape=[jax.ShapeDtypeStruct((B, MLA_HEADS, T, qw), BF16),
                   jax.ShapeDtypeStruct((B, T, 2048), F32)],
        compiler_params=_cparams(2),
        name=f"b_pre_{tag}",
    )(x, g, w_cq, w_rest, g_q, w_qn, w_qr, w_kup, cos, sin)


def _flash_kernel(q_ref, knt_ref, krt_ref, v_ref, x_ref, z_ref, mk_ref, mv_ref, wout_ref, gf_ref, xo_ref,
                  m_sc, l_sc, acc_sc, tok_sc, *, tq, tk, final):
    i = pl.program_id(1)
    m_sc[...] = jnp.full(m_sc.shape, NEG, F32)
    l_sc[...] = jnp.zeros(l_sc.shape, F32)
    acc_sc[...] = jnp.zeros(acc_sc.shape, F32)

    def block(j, masked):
        krt = krt_ref[0, j]
        rows = pl.ds(pl.multiple_of(j * tk, tk), tk)
        if masked:
            keep = (lax.broadcasted_iota(jnp.int32, (tq, tk), 1)
                    <= lax.broadcasted_iota(jnp.int32, (tq, tk), 0))

        def scores(hd):
            kt = jnp.concatenate([knt_ref[0, j, hd * QK_NOPE:(hd + 1) * QK_NOPE, :], krt], axis=0)
            return _dot(q_ref[0, hd], kt)

        s_next = scores(0)
        for hd in range(MLA_HEADS):
            s = s_next
            if hd + 1 < MLA_HEADS:
                s_next = scores(hd + 1)
            if masked:
                s = jnp.where(keep, s, NEG)
            m_prev = m_sc[hd]
            m_new = jnp.maximum(m_prev, jnp.max(s, axis=-1, keepdims=True))
            alpha = jnp.exp2(m_prev - m_new)
            p = jnp.exp2(s - _rep(m_new, tk // LANES))
            p_lanes = p[:, 0:LANES]
            for c in range(1, tk // LANES):
                p_lanes = p_lanes + p[:, c * LANES:(c + 1) * LANES]
            l_sc[hd] = alpha * l_sc[hd] + p_lanes
            v = v_ref[0, rows, hd * V_HEAD:(hd + 1) * V_HEAD]
            acc_sc[hd] = alpha * acc_sc[hd] + _dot(p.astype(BF16), v)
            m_sc[hd] = m_new

    def body(j, carry):
        block(j, False)
        return carry

    lax.fori_loop(0, i, body, 0)
    block(i, True)

    for hd in range(MLA_HEADS):
        l = jnp.sum(l_sc[hd], axis=-1, keepdims=True)
        tok_sc[0, :, hd * V_HEAD:(hd + 1) * V_HEAD] = acc_sc[hd] * (1.0 / l)

    _mix_out(x_ref, tok_sc, z_ref, mk_ref, mv_ref, wout_ref, gf_ref, xo_ref, final)


def _flash_post(q, knt, krt, v, x, zrest, mem_k, mem_v, w_out, g_final, *, layer, tq, final):
    B, H, T, qw = q.shape
    tk = knt.shape[-1]
    assert tq == tk and qw == QK_NOPE + LANES
    kern = functools.partial(_flash_kernel, tq=tq, tk=tk, final=final)
    return pl.pallas_call(
        kern,
        grid=(B, T // tq),
        in_specs=[
            pl.BlockSpec((1, H, tq, qw), lambda b, i: (b, 0, i, 0)),
            pl.BlockSpec((1, T // tk, H * QK_NOPE, tk), lambda b, i: (b, 0, 0, 0)),
            pl.BlockSpec((1, T // tk, LANES, tk), lambda b, i: (b, 0, 0, 0)),
            pl.BlockSpec((1, T, MLA_WIDTH), lambda b, i: (b, 0, 0)),
            pl.BlockSpec((1, tq, D_MODEL), lambda b, i: (b, i, 0)),
            pl.BlockSpec((1, tq, 2048), lambda b, i: (b, i, 0)),
            pl.BlockSpec((1, 1, MEM_ROWS, MEM_HEAD_DIM), lambda b, i: (layer, b, 0, 0)),
            pl.BlockSpec((1, 1, MEM_ROWS, MEM_HEAD_DIM), lambda b, i: (layer, b, 0, 0)),
            pl.BlockSpec((1536, D_MODEL), lambda b, i: (0, 0)),
            pl.BlockSpec((1, D_MODEL), lambda b, i: (0, 0)),
        ],
        out_specs=pl.BlockSpec((1, tq, D_MODEL), lambda b, i: (b, i, 0)),
        out_shape=jax.ShapeDtypeStruct((B, T, D_MODEL), F32),
        scratch_shapes=[pltpu.VMEM((H, tq, LANES), F32), pltpu.VMEM((H, tq, LANES), F32),
                        pltpu.VMEM((H, tq, V_HEAD), F32), pltpu.VMEM((1, tq, MLA_WIDTH), F32)],
        compiler_params=_cparams(2),
        name=f"flash_post{layer}",
    )(q, knt, krt, v, x, zrest, mem_k, mem_v, w_out, g_final)


def _paged_kernel(pt_ref, q_ref, knew_ref, ckv_hbm, krt_hbm, o_ref,
                  ckv_buf, krt_buf, kbf_sc, s_sc, sem, m_sc, l_sc, acc_sc,
                  *, n_b, n_ch, cp, page, tp, n_new):
    total = n_b * n_ch

    def copies(bc):
        slot = bc % N_RAW
        b = bc // n_ch
        c = bc % n_ch
        out = []
        for p in range(cp):
            pg = pt_ref[b, c * cp + p]
            out.append(pltpu.make_async_copy(ckv_hbm.at[pg], ckv_buf.at[slot, pl.ds(p * page, page)],
                                             sem.at[0, slot]))
            out.append(pltpu.make_async_copy(krt_hbm.at[pg], krt_buf.at[slot, p], sem.at[1, slot]))
        return out

    def score(bc, w):
        slot = bc % N_RAW
        b = bc // n_ch
        for cpy in copies(bc):
            cpy.wait()
        q_lat = q_ref[b, :, 0:KV_RANK]
        q_rope = q_ref[b, :, KV_RANK:KV_RANK + QK_ROPE]
        kc = ckv_buf[slot].astype(BF16)
        krt = jnp.concatenate([krt_buf[slot, p] for p in range(cp)], axis=1).astype(BF16)
        kbf_sc[w] = kc
        s_sc[w] = _dot_nt(q_lat, kc) + _dot(q_rope, krt)

    def update(s, v_bf16, first):
        m_prev = jnp.where(first, NEG, m_sc[...])
        l_prev = jnp.where(first, 0.0, l_sc[...])
        acc_prev = jnp.where(first, 0.0, acc_sc[...])
        m_new = jnp.maximum(m_prev, jnp.max(s, axis=-1, keepdims=True))
        alpha = jnp.exp2(m_prev - m_new)
        p = jnp.exp2(s - _rep(m_new, s.shape[1] // LANES))
        l_sc[...] = alpha * l_prev + jnp.sum(p, axis=-1, keepdims=True)
        acc_sc[...] = _rep(alpha, KV_RANK // LANES) * acc_prev + _dot(p.astype(BF16), v_bf16)
        m_sc[...] = m_new

    def attend(bc, w):
        b = bc // n_ch
        c = bc % n_ch
        update(s_sc[w], kbf_sc[w], c == 0)

        def finish():
            kn = knew_ref[b]
            sn = _dot_nt(q_ref[b], kn)
            tpos = lax.broadcasted_iota(jnp.int32, sn.shape, 0) & (tp - 1)
            col = lax.broadcasted_iota(jnp.int32, sn.shape, 1)
            sn = jnp.where((col <= tpos) & (col < n_new), sn, NEG)
            update(sn, kn[:, 0:KV_RANK], False)
            o_ref[b] = acc_sc[...] * _rep(1.0 / l_sc[...], KV_RANK // LANES)

        last = c == n_ch - 1
        if isinstance(last, bool):
            if last:
                finish()
        else:
            pl.when(last)(finish)

    m_sc[...] = jnp.full(m_sc.shape, NEG, F32)
    l_sc[...] = jnp.zeros(l_sc.shape, F32)
    acc_sc[...] = jnp.zeros(acc_sc.shape, F32)
    for bc in range(min(N_RAW, total)):
        for cpy in copies(bc):
            cpy.start()
    score(0, 0)

    def step(i, w):
        def prefetch():
            for cpy in copies(i + N_RAW):
                cpy.start()

        more = i + N_RAW < total
        if isinstance(more, bool):
            if more:
                prefetch()
        else:
            pl.when(more)(prefetch)
        if isinstance(i, int) and i + 1 >= total:
            attend(i, w)
            return
        score(i + 1, 1 - w)
        attend(i, w)

    def body(k, carry):
        step(2 * k, 0)
        step(2 * k + 1, 1)
        return carry

    n_pairs = (total - 2) // 2
    lax.fori_loop(0, n_pairs, body, 0)
    for i in range(2 * n_pairs, total):
        step(i, i % 2)


def _paged_attention(page_table, q, knew, cache_ckv, cache_krt, *, cp, tp, n_new):
    n_b, n_pages = page_table.shape
    page = cache_ckv.shape[1]
    rows = q.shape[1]
    n_ch = n_pages // cp
    kern = functools.partial(_paged_kernel, n_b=n_b, n_ch=n_ch, cp=cp, page=page, tp=tp, n_new=n_new)
    grid_spec = pltpu.PrefetchScalarGridSpec(
        num_scalar_prefetch=1,
        grid=(1,),
        in_specs=[
            pl.BlockSpec(q.shape, lambda i, pt: (0, 0, 0)),
            pl.BlockSpec(knew.shape, lambda i, pt: (0, 0, 0)),
            pl.BlockSpec(memory_space=pl.ANY),
            pl.BlockSpec(memory_space=pl.ANY),
        ],
        out_specs=pl.BlockSpec((n_b, rows, KV_RANK), lambda i, pt: (0, 0, 0)),
        scratch_shapes=[
            pltpu.VMEM((N_RAW, cp * page, KV_RANK), F32),
            pltpu.VMEM((N_RAW, cp, QK_ROPE, page), F32),
            pltpu.VMEM((2, cp * page, KV_RANK), BF16),
            pltpu.VMEM((2, rows, cp * page), F32),
            pltpu.SemaphoreType.DMA((2, N_RAW)),
            pltpu.VMEM((rows, LANES), F32),
            pltpu.VMEM((rows, LANES), F32),
            pltpu.VMEM((rows, KV_RANK), F32),
        ],
    )
    return pl.pallas_call(
        kern,
        grid_spec=grid_spec,
        out_shape=jax.ShapeDtypeStruct((n_b, rows, KV_RANK), F32),
        compiler_params=_cparams(1),
        name="paged_attention",
    )(page_table, q, knew, cache_ckv, cache_krt)


def _v_up_kernel(o_ref, wv_ref, tok_ref):
    n_b, _, tp, _ = o_ref.shape
    for hd in range(MLA_HEADS):
        oh = o_ref[:, hd].reshape(n_b * tp, KV_RANK).astype(BF16)
        tok_ref[:, :, hd * V_HEAD:(hd + 1) * V_HEAD] = _dot(oh, wv_ref[hd]).reshape(n_b, tp, V_HEAD)


def _v_up(o, w_vup):
    n_b, H, tp, _ = o.shape
    return pl.pallas_call(
        _v_up_kernel,
        out_shape=jax.ShapeDtypeStruct((n_b, tp, MLA_WIDTH), F32),
        compiler_params=pltpu.CompilerParams(vmem_limit_bytes=VMEM_LIMIT),
        name="v_up_sample",
    )(o, w_vup)


def _b_post_kernel(x_ref, tok_ref, z_ref, mk_ref, mv_ref, wout_ref, gf_ref, xo_ref, *, final):
    _mix_out(x_ref, tok_ref, z_ref, mk_ref, mv_ref, wout_ref, gf_ref, xo_ref, final)


def _mix_out(x_ref, tok_ref, z_ref, mk_ref, mv_ref, wout_ref, gf_ref, xo_ref, final):
    nb, tq, _ = x_ref.shape
    m = nb * tq
    x = x_ref[...].reshape(m, D_MODEL)
    acc = x
    for g in range(4):
        c0, c1 = g * 256, (g + 1) * 256
        mt = (tok_ref[:, :, c0:c1] * _silu(z_ref[:, :, c0:c1])).reshape(m, 256).astype(BF16)
        acc = acc + _dot(mt, wout_ref[c0:c1, :])
    mem_o = _mem_attend(z_ref[:, :, 1024:1536].reshape(m, MEM_WIDTH), mk_ref, mv_ref, nb, tq)
    for hd in range(MEM_HEADS):
        c0, c1 = hd * MEM_HEAD_DIM, (hd + 1) * MEM_HEAD_DIM
        gate = z_ref[:, :, 1536 + c0:1536 + c1].reshape(m, MEM_HEAD_DIM)
        mm = (mem_o[hd] * _silu(gate)).astype(BF16)
        acc = acc + _dot(mm, wout_ref[1024 + c0:1024 + c1, :])
    if final:
        acc = _rms(acc, gf_ref[...])
    xo_ref[...] = acc.reshape(nb, tq, D_MODEL)


def _b_post(x, tok, zrest, mem_k, mem_v, w_out, g_final, *, layer, nb, tq, final, tag):
    B, T, _ = x.shape
    kern = functools.partial(_b_post_kernel, final=final)
    return pl.pallas_call(
        kern,
        grid=(B // nb, T // tq),
        in_specs=[
            pl.BlockSpec((nb, tq, D_MODEL), lambda b, t: (b, t, 0)),
            pl.BlockSpec((nb, tq, MLA_WIDTH), lambda b, t: (b, t, 0)),
            pl.BlockSpec((nb, tq, 2048), lambda b, t: (b, t, 0)),
            pl.BlockSpec((1, nb, MEM_ROWS, MEM_HEAD_DIM), lambda b, t: (layer, b, 0, 0)),
            pl.BlockSpec((1, nb, MEM_ROWS, MEM_HEAD_DIM), lambda b, t: (layer, b, 0, 0)),
            pl.BlockSpec((1536, D_MODEL), lambda b, t: (0, 0)),
            pl.BlockSpec((1, D_MODEL), lambda b, t: (0, 0)),
        ],
        out_specs=pl.BlockSpec((nb, tq, D_MODEL), lambda b, t: (b, t, 0)),
        out_shape=jax.ShapeDtypeStruct((B, T, D_MODEL), F32),
        compiler_params=_cparams(2),
        name=f"b_post{layer}_{tag}",
    )(x, tok, zrest, mem_k, mem_v, w_out, g_final)


def _rope_tables(pos):
    half = QK_ROPE // 2
    inv = ROPE_THETA ** (-jnp.arange(half, dtype=F32) / half)
    ang = pos.astype(F32)[:, None] * inv[None, :]
    return jnp.tile(jnp.cos(ang), (1, LANES // half)), jnp.tile(jnp.sin(ang), (1, LANES // half))


def kernel(x_prompt, x_sample, state_pool, cache_ckv, cache_krope, cache_mem_k, cache_mem_v, page_table, mem_prompt, g_norm, w_in_a, w_pool_grp, pool_scale, w_in_b, g_q_latent, w_q_up, g_kv_in, w_kv_down, g_kv_latent, w_k_up, w_v_up, g_mem, w_mem_k, w_mem_v, w_out, g_final):
    B, T, _ = x_prompt.shape
    SB, ST, _ = x_sample.shape
    TP = SUBLANES
    past = page_table.shape[1] * cache_ckv.shape[1]

    w_in_a16 = w_in_a.astype(BF16)
    w_grp16 = w_pool_grp.astype(BF16)
    w_out16 = w_out.astype(BF16)
    w_cq16 = w_in_b[:, :, :Q_RANK].astype(BF16)
    w_rest16 = w_in_b[:, :, Q_RANK:].astype(BF16)
    wq = w_q_up.reshape(N_A, Q_RANK, MLA_HEADS, QK_NOPE + QK_ROPE)
    w_qn16 = wq[..., :QK_NOPE].reshape(N_A, Q_RANK, MLA_HEADS * QK_NOPE).astype(BF16)
    w_qr16 = wq[..., QK_NOPE:].reshape(N_A, Q_RANK, MLA_HEADS * QK_ROPE).astype(BF16)
    w_kup16 = jnp.transpose(w_k_up, (1, 2, 0)).astype(BF16)
    w_vup16 = jnp.transpose(w_v_up, (1, 0, 2)).astype(BF16)
    w_kv16 = jnp.pad(w_kv_down, ((0, 0), (0, KCAT - KV_RANK - QK_ROPE))).astype(BF16)
    w_mk16 = w_mem_k.astype(BF16)
    w_mv16 = w_mem_v.astype(BF16)
    g_kv_in2 = g_kv_in[None]
    g_lat2 = g_kv_latent[None]
    g_final2 = g_final[None]

    w_kupt16 = w_kup16.reshape(MLA_HEADS * QK_NOPE, KV_RANK)
    w_vall16 = w_v_up.reshape(KV_RANK, MLA_WIDTH).astype(BF16)

    def trunk(x, prev, mem_k, mem_v, pos, *, tag, nb, tq_a, tq, tq_b, heads, n_valid, attend):
        pools = []
        for l in range(N_A):
            x, pool = _a_layer(x, prev[l], g_norm[l][None], w_in_a16[l], w_grp16[l], pool_scale[l][None],
                               mem_k, mem_v, w_out16[l], layer=l, nb=nb, tq=tq_a, n_valid=n_valid, pos0=pos)
            pools.append(pool[:, 1:])
        cos, sin = _rope_tables(pos + jnp.arange(x.shape[1], dtype=jnp.int32))
        ckv, krope, *keys = _kv_latent(x, g_kv_in2, w_kv16, g_lat2, cos, sin, w_kupt16, w_vall16,
                                       nb=nb, tq=tq, heads=heads)
        for j in range(DEPTH - N_A):
            l = N_A + j
            q, zrest = _b_pre(x, g_norm[l][None], w_cq16[j], w_rest16[j], g_q_latent[j][None],
                              w_qn16[j], w_qr16[j], w_kup16, cos, sin, nb=nb, tq=tq_b, absorb=not heads,
                              tag=f"{tag}{l}")
            x = attend(x, q, zrest, *keys, layer=l, final=(l == DEPTH - 1))
        return x, jnp.stack(pools, axis=0), ckv, krope

    mem_k_p, mem_v_p = _mem_project(mem_prompt, g_mem, w_mk16, w_mv16)
    prev_p = jnp.zeros((N_A, B, HALO, D_MODEL), F32)

    def attend_prompt(x, q, zrest, knt, krt, v, *, layer, final):
        return _flash_post(q, knt, krt, v, x, zrest, mem_k_p, mem_v_p, w_out16[layer], g_final2,
                           layer=layer, tq=256, final=final)

    y_p, pool_p, ckv_p, krope_p = trunk(
        x_prompt, prev_p, mem_k_p, mem_v_p, 0, tag="p", nb=1, tq_a=512, tq=256, tq_b=512, heads=True, n_valid=512,
        attend=attend_prompt)

    xs = jnp.pad(x_sample, ((0, 0), (0, TP - ST), (0, 0)))
    prev_s = jnp.pad(state_pool, ((0, 0), (0, 0), (HALO - POOL_BUF, 0), (0, 0)))
    mem_k_s = cache_mem_k.reshape(DEPTH, SB, MEM_ROWS, MEM_HEAD_DIM)
    mem_v_s = cache_mem_v.reshape(DEPTH, SB, MEM_ROWS, MEM_HEAD_DIM)
    cache_krt = jnp.transpose(cache_krope, (0, 2, 1))

    def attend_sample(x, q, zrest, kcat, *, layer, final):
        qf = q.reshape(SB, MLA_HEADS * TP, KCAT)
        knew = jnp.pad(kcat, ((0, 0), (0, LANES - TP), (0, 0)))
        o = _paged_attention(page_table, qf, knew, cache_ckv, cache_krt, cp=16, tp=TP, n_new=ST)
        tok = _v_up(o.reshape(SB, MLA_HEADS, TP, KV_RANK), w_vup16)
        return _b_post(x, tok, zrest, mem_k_s, mem_v_s, w_out16[layer], g_final2, layer=layer, nb=8, tq=TP,
                       final=final, tag="s")

    y_s, pool_s, ckv_s, krope_s = trunk(
        xs, prev_s, mem_k_s, mem_v_s, past, tag="s", nb=8, tq_a=TP, tq=TP, tq_b=TP, heads=False, n_valid=ST,
        attend=attend_sample)

    mem_shape = (DEPTH, B, MEM_TOKENS, MEM_HEADS, MEM_HEAD_DIM)
    return (y_p, y_s[:, :ST], pool_p, pool_s, ckv_p, krope_p, ckv_s[:, :ST], krope_s[:, :ST],
            mem_k_p.reshape(mem_shape), mem_v_p.reshape(mem_shape))
```

```python
import functools

import jax
import jax.numpy as jnp
from jax import lax
from jax.experimental import pallas as pl
from jax.experimental.pallas import tpu as pltpu

F32 = jnp.float32
BF16 = jnp.bfloat16

D_MODEL = 1024
DEPTH = 4
N_A = 2
POOL_WINDOWS = (2, 4, 8, 16)
POOL_GROUP = 256
POOL_BUF = 15
MLA_HEADS = 8
QK_NOPE = 128
QK_ROPE = 64
V_HEAD = 128
KV_RANK = 256
Q_RANK = 384
MLA_WIDTH = MLA_HEADS * V_HEAD
MLA_SCALE = (QK_NOPE + QK_ROPE) ** -0.5
ROPE_THETA = 10000.0
MEM_TOKENS = 256
MEM_HEADS = 4
MEM_HEAD_DIM = 128
MEM_WIDTH = MEM_HEADS * MEM_HEAD_DIM
MEM_SCALE = MEM_HEAD_DIM ** -0.5
EPS = 1e-6
NEG = -1e30
LOG2E = 1.4426950408889634

KCAT = 384
SUBLANES = 8
LANES = 128
HALO = 16
E_OFF = SUBLANES + HALO
VMEM_LIMIT = 56 * 1024 * 1024
MEM_ROWS = MEM_TOKENS * MEM_HEADS
N_RAW = 4


def _cparams(n_axes):
    return pltpu.CompilerParams(dimension_semantics=("arbitrary",) * n_axes,
                                vmem_limit_bytes=VMEM_LIMIT)


def _rms(x, g):
    return x * lax.rsqrt(jnp.mean(x * x, axis=-1, keepdims=True) + EPS) * g


def _dot(a, b):
    return jnp.dot(a, b, preferred_element_type=F32)


def _dot_nt(a, b):
    return lax.dot_general(a, b, (((1,), (1,)), ((), ())), preferred_element_type=F32)


def _silu(x):
    return x * jax.nn.sigmoid(x)


def _rope_lanes(a, cos, sin):
    lane = lax.broadcasted_iota(jnp.int32, a.shape, 1)
    up = pltpu.roll(a, 32, 1)
    dn = pltpu.roll(a, 96, 1)
    rot = jnp.where((lane & 63) < 32, -dn, up)
    return a * cos + rot * sin


def _rep(x, n):
    return x if n == 1 else jnp.concatenate([x] * n, axis=1)


def _tile_rows(x, n):
    return x if n == 1 else jnp.concatenate([x] * n, axis=0)


def _mem_attend(qm, mk_ref, mv_ref, nb, tq):
    rows, keys = nb * tq, nb * MEM_TOKENS
    if nb > 1:
        row_seq = lax.broadcasted_iota(jnp.int32, (rows, keys), 0) // tq
        col_seq = lax.broadcasted_iota(jnp.int32, (rows, keys), 1) // MEM_TOKENS
        keep = row_seq == col_seq
    outs = []
    for h in range(MEM_HEADS):
        c0, c1 = h * MEM_HEAD_DIM, (h + 1) * MEM_HEAD_DIM
        q = (qm[:, c0:c1] * (MEM_SCALE * LOG2E)).astype(BF16)
        k = mk_ref[0, :, pl.ds(h, MEM_TOKENS, stride=MEM_HEADS), :].reshape(keys, MEM_HEAD_DIM).astype(BF16)
        v = mv_ref[0, :, pl.ds(h, MEM_TOKENS, stride=MEM_HEADS), :].reshape(keys, MEM_HEAD_DIM).astype(BF16)
        s = _dot_nt(q, k)
        if nb > 1:
            s = jnp.where(keep, s, NEG)
        m = jnp.max(s, axis=-1, keepdims=True)
        p = jnp.exp2(s - m)
        l = jnp.sum(p, axis=-1, keepdims=True)
        outs.append(_dot(p.astype(BF16), v) * (1.0 / l))
    return outs


def _mem_proj_kernel(mem_ref, g_ref, wk_ref, wv_ref, mk_ref, mv_ref):
    x = mem_ref[0]
    xn = x * lax.rsqrt(jnp.mean(x * x, axis=-1, keepdims=True) + EPS)
    for l in range(DEPTH):
        h = (xn * g_ref[l:l + 1, :]).astype(BF16)
        mk = _dot(h, wk_ref[l])
        mv = _dot(h, wv_ref[l])
        for hd in range(MEM_HEADS):
            c0, c1 = hd * MEM_HEAD_DIM, (hd + 1) * MEM_HEAD_DIM
            mk_ref[l, 0, pl.ds(hd, MEM_TOKENS, stride=MEM_HEADS), :] = mk[:, c0:c1]
            mv_ref[l, 0, pl.ds(hd, MEM_TOKENS, stride=MEM_HEADS), :] = mv[:, c0:c1]


def _mem_project(mem, g_mem, wk, wv):
    B, M, _ = mem.shape
    out = jax.ShapeDtypeStruct((DEPTH, B, MEM_ROWS, MEM_HEAD_DIM), F32)
    return pl.pallas_call(
        _mem_proj_kernel,
        grid=(B,),
        in_specs=[
            pl.BlockSpec((1, M, D_MODEL), lambda b: (b, 0, 0)),
            pl.BlockSpec((DEPTH, D_MODEL), lambda b: (0, 0)),
            pl.BlockSpec((DEPTH, D_MODEL, MEM_WIDTH), lambda b: (0, 0, 0)),
            pl.BlockSpec((DEPTH, D_MODEL, MEM_WIDTH), lambda b: (0, 0, 0)),
        ],
        out_specs=[
            pl.BlockSpec((DEPTH, 1, MEM_ROWS, MEM_HEAD_DIM), lambda b: (0, b, 0, 0)),
            pl.BlockSpec((DEPTH, 1, MEM_ROWS, MEM_HEAD_DIM), lambda b: (0, b, 0, 0)),
        ],
        out_shape=[out, out],
        compiler_params=_cparams(1),
        name="mem_project",
    )(mem, g_mem, wk, wv)


def _a_layer_kernel(x_ref, prev_ref, g_ref, win_ref, wgrp_ref, pscale_ref, mk_ref, mv_ref, wout_ref,
                    xo_ref, pool_ref, e_sc, sa_sc, sb_sc, mix_sc, *, nb, tq, n_t, n_valid, pos0):
    t = pl.program_id(1)
    rows = HALO + tq
    m = nb * tq

    @pl.when(t == 0)
    def _():
        zeros = jnp.zeros((nb, SUBLANES, D_MODEL), F32)
        e_sc[:, 0:SUBLANES, :] = zeros
        sa_sc[:, 0:SUBLANES, :] = zeros
        sb_sc[:, 0:SUBLANES, :] = zeros
        e_sc[:, SUBLANES:E_OFF, :] = prev_ref[...]

    x = x_ref[...].reshape(m, D_MODEL)
    h = _rms(x, g_ref[...]).astype(BF16)
    e_sc[:, E_OFF:E_OFF + tq, :] = _dot(h, win_ref[:, 0:1024]).reshape(nb, tq, D_MODEL)

    lo = SUBLANES
    sa_sc[:, lo:lo + rows, :] = e_sc[:, lo:lo + rows, :] + e_sc[:, lo - 1:lo - 1 + rows, :]
    sb_sc[:, lo:lo + rows, 256:1024] = (sa_sc[:, lo:lo + rows, 256:1024]
                                        + sa_sc[:, lo - 2:lo - 2 + rows, 256:1024])
    sa_sc[:, lo:lo + rows, 512:1024] = (sb_sc[:, lo:lo + rows, 512:1024]
                                        + sb_sc[:, lo - 4:lo - 4 + rows, 512:1024])
    s16 = sa_sc[:, E_OFF:E_OFF + tq, 768:1024] + sa_sc[:, E_OFF - 8:E_OFF - 8 + tq, 768:1024]
    sums = (sa_sc[:, E_OFF:E_OFF + tq, 0:256], sb_sc[:, E_OFF:E_OFF + tq, 256:512],
            sa_sc[:, E_OFF:E_OFF + tq, 512:768], s16)

    pos = pos0 + t * tq + lax.broadcasted_iota(jnp.int32, (1, tq, 1), 1)
    for g, w in enumerate(POOL_WINDOWS):
        c0, c1 = g * POOL_GROUP, (g + 1) * POOL_GROUP
        inv_cnt = 1.0 / jnp.minimum(pos + 1, w).astype(F32)
        pooled = (sums[g] * inv_cnt - e_sc[:, E_OFF:E_OFF + tq, c0:c1]).reshape(m, POOL_GROUP)
        mixed = _dot(pooled.astype(BF16), wgrp_ref[g]) * pscale_ref[:, c0:c1]
        gate = _dot(h, win_ref[:, 1024 + c0:1024 + c1])
        mix_sc[:, c0:c1] = (mixed * _silu(gate)).astype(BF16)

    qm = _dot(h, win_ref[:, 2048:2560])
    gate_m = _dot(h, win_ref[:, 2560:3072])
    mem_o = _mem_attend(qm, mk_ref, mv_ref, nb, tq)
    for hd in range(MEM_HEADS):
        c0, c1 = hd * MEM_HEAD_DIM, (hd + 1) * MEM_HEAD_DIM
        mix_sc[:, 1024 + c0:1024 + c1] = (mem_o[hd] * _silu(gate_m[:, c0:c1])).astype(BF16)
    xo_ref[...] = (x + _dot(mix_sc[...], wout_ref[...])).reshape(nb, tq, D_MODEL)

    @pl.when(t == n_t - 1)
    def _():
        pool_ref[...] = e_sc[:, E_OFF + n_valid - HALO:E_OFF + n_valid, :]

    if n_t > 1:
        @pl.when(t < n_t - 1)
        def _():
            e_sc[:, SUBLANES:E_OFF, :] = e_sc[:, SUBLANES + tq:E_OFF + tq, :]


def _a_layer(x, prev, g, w_in, w_grp, pscale, mem_k, mem_v, w_out, *, layer, nb, tq, n_valid, pos0):
    B, T, _ = x.shape
    n_t = T // tq
    rows = E_OFF + tq
    kern = functools.partial(_a_layer_kernel, nb=nb, tq=tq, n_t=n_t, n_valid=n_valid, pos0=pos0)
    return pl.pallas_call(
        kern,
        grid=(B // nb, n_t),
        in_specs=[
            pl.BlockSpec((nb, tq, D_MODEL), lambda b, t: (b, t, 0)),
            pl.BlockSpec((nb, HALO, D_MODEL), lambda b, t: (b, 0, 0)),
            pl.BlockSpec((1, D_MODEL), lambda b, t: (0, 0)),
            pl.BlockSpec((D_MODEL, 3072), lambda b, t: (0, 0)),
            pl.BlockSpec((4, POOL_GROUP, POOL_GROUP), lambda b, t: (0, 0, 0)),
            pl.BlockSpec((1, D_MODEL), lambda b, t: (0, 0)),
            pl.BlockSpec((1, nb, MEM_ROWS, MEM_HEAD_DIM), lambda b, t: (layer, b, 0, 0)),
            pl.BlockSpec((1, nb, MEM_ROWS, MEM_HEAD_DIM), lambda b, t: (layer, b, 0, 0)),
            pl.BlockSpec((1536, D_MODEL), lambda b, t: (0, 0)),
        ],
        out_specs=[
            pl.BlockSpec((nb, tq, D_MODEL), lambda b, t: (b, t, 0)),
            pl.BlockSpec((nb, HALO, D_MODEL), lambda b, t: (b, 0, 0)),
        ],
        out_shape=[jax.ShapeDtypeStruct((B, T, D_MODEL), F32),
                   jax.ShapeDtypeStruct((B, HALO, D_MODEL), F32)],
        scratch_shapes=[pltpu.VMEM((nb, rows, D_MODEL), F32)] * 3 + [pltpu.VMEM((nb * tq, 1536), BF16)],
        compiler_params=_cparams(2),
        name=f"a_layer{layer}_{B}",
    )(x, prev, g, w_in, w_grp, pscale, mem_k, mem_v, w_out)


def _kv_latent_kernel(x_ref, gin_ref, wkv_ref, glat_ref, cos_ref, sin_ref, wkupt_ref, wvup_ref,
                      ckv_ref, kr_ref, *out_refs, heads):
    nb, tq, _ = x_ref.shape
    m = nb * tq
    h = _rms(x_ref[...].reshape(m, D_MODEL), gin_ref[...]).astype(BF16)
    kv = _dot(h, wkv_ref[...])
    ckv = _rms(kv[:, 0:KV_RANK], glat_ref[...])
    kr = _rope_lanes(kv[:, KV_RANK:KCAT], _tile_rows(cos_ref[...], nb), _tile_rows(sin_ref[...], nb))
    ckv_ref[...] = ckv.reshape(nb, tq, KV_RANK)
    kr_ref[...] = kr[:, 0:QK_ROPE].reshape(nb, tq, QK_ROPE)
    if not heads:
        kcat_ref, = out_refs
        kcat_ref[:, :, 0:KV_RANK] = ckv.reshape(nb, tq, KV_RANK).astype(BF16)
        kcat_ref[:, :, KV_RANK:KCAT] = kr.reshape(nb, tq, LANES).astype(BF16)
    else:
        knt_ref, krt_ref, v_ref = out_refs
        ckv16 = ckv.astype(BF16)
        knt_ref[0, 0] = _dot(wkupt_ref[...], ckv.T.astype(BF16)).astype(BF16)
        krt_ref[0, 0] = kr.T.astype(BF16)
        v_ref[0] = _dot(ckv16, wvup_ref[...]).astype(BF16)


def _kv_latent(x, g_in, w_kv, g_lat, cos, sin, w_kupt, w_vup, *, nb, tq, heads):
    B, T, _ = x.shape
    assert nb == 1 or not heads
    out_specs = [
        pl.BlockSpec((nb, tq, KV_RANK), lambda b, t: (b, t, 0)),
        pl.BlockSpec((nb, tq, QK_ROPE), lambda b, t: (b, t, 0)),
    ]
    out_shape = [jax.ShapeDtypeStruct((B, T, KV_RANK), F32),
                 jax.ShapeDtypeStruct((B, T, QK_ROPE), F32)]
    if heads:
        out_specs += [pl.BlockSpec((1, 1, MLA_HEADS * QK_NOPE, tq), lambda b, t: (b, t, 0, 0)),
                      pl.BlockSpec((1, 1, LANES, tq), lambda b, t: (b, t, 0, 0)),
                      pl.BlockSpec((1, tq, MLA_WIDTH), lambda b, t: (b, t, 0))]
        out_shape += [jax.ShapeDtypeStruct((B, T // tq, MLA_HEADS * QK_NOPE, tq), BF16),
                      jax.ShapeDtypeStruct((B, T // tq, LANES, tq), BF16),
                      jax.ShapeDtypeStruct((B, T, MLA_WIDTH), BF16)]
    else:
        out_specs.append(pl.BlockSpec((nb, tq, KCAT), lambda b, t: (b, t, 0)))
        out_shape.append(jax.ShapeDtypeStruct((B, T, KCAT), BF16))
    return pl.pallas_call(
        functools.partial(_kv_latent_kernel, heads=heads),
        grid=(B // nb, T // tq),
        in_specs=[
            pl.BlockSpec((nb, tq, D_MODEL), lambda b, t: (b, t, 0)),
            pl.BlockSpec((1, D_MODEL), lambda b, t: (0, 0)),
            pl.BlockSpec((D_MODEL, KCAT), lambda b, t: (0, 0)),
            pl.BlockSpec((1, KV_RANK), lambda b, t: (0, 0)),
            pl.BlockSpec((tq, LANES), lambda b, t: (t, 0)),
            pl.BlockSpec((tq, LANES), lambda b, t: (t, 0)),
            pl.BlockSpec((MLA_HEADS * QK_NOPE, KV_RANK), lambda b, t: (0, 0)),
            pl.BlockSpec((KV_RANK, MLA_WIDTH), lambda b, t: (0, 0)),
        ],
        out_specs=out_specs,
        out_shape=out_shape,
        compiler_params=_cparams(2),
        name=f"kv_latent_{B}",
    )(x, g_in, w_kv, g_lat, cos, sin, w_kupt, w_vup)


def _b_pre_kernel(x_ref, g_ref, wcq_ref, wrest_ref, gq_ref, wqn_ref, wqr_ref, wkup_ref, cos_ref, sin_ref,
                  q_ref, zrest_ref, *, absorb):
    nb, tq, _ = x_ref.shape
    m = nb * tq
    n0 = KV_RANK if absorb else QK_NOPE
    h = _rms(x_ref[...].reshape(m, D_MODEL), g_ref[...]).astype(BF16)
    zrest_ref[...] = _dot(h, wrest_ref[...]).reshape(nb, tq, 2048)
    cn = _rms(_dot(h, wcq_ref[...]), gq_ref[...]).astype(BF16)
    qn = _dot(cn, wqn_ref[...])
    qr = _dot(cn, wqr_ref[...])
    for hd in range(MLA_HEADS):
        ql = qn[:, hd * QK_NOPE:(hd + 1) * QK_NOPE]
        if absorb:
            ql = _dot(ql.astype(BF16), wkup_ref[hd])
        q_ref[:, hd, :, 0:n0] = (ql * (MLA_SCALE * LOG2E)).reshape(nb, tq, n0).astype(BF16)
    cos = _tile_rows(cos_ref[...], nb)
    sin = _tile_rows(sin_ref[...], nb)
    lane = lax.broadcasted_iota(jnp.int32, cos.shape, 1)
    for j in range(MLA_HEADS // 2):
        rr = _rope_lanes(qr[:, j * LANES:(j + 1) * LANES], cos, sin) * (MLA_SCALE * LOG2E)
        even = jnp.where(lane < QK_ROPE, rr, 0.0)
        odd = jnp.where(lane < QK_ROPE, pltpu.roll(rr, 64, 1), 0.0)
        q_ref[:, 2 * j, :, n0:n0 + LANES] = even.reshape(nb, tq, LANES).astype(BF16)
        q_ref[:, 2 * j + 1, :, n0:n0 + LANES] = odd.reshape(nb, tq, LANES).astype(BF16)


def _b_pre(x, g, w_cq, w_rest, g_q, w_qn, w_qr, w_kup, cos, sin, *, nb, tq, absorb, tag):
    B, T, _ = x.shape
    const2 = lambda b, t: (0, 0)
    qw = (KV_RANK if absorb else QK_NOPE) + LANES
    return pl.pallas_call(
        functools.partial(_b_pre_kernel, absorb=absorb),
        grid=(B // nb, T // tq),
        in_specs=[
            pl.BlockSpec((nb, tq, D_MODEL), lambda b, t: (b, t, 0)),
            pl.BlockSpec((1, D_MODEL), const2),
            pl.BlockSpec((D_MODEL, Q_RANK), const2),
            pl.BlockSpec((D_MODEL, 2048), const2),
            pl.BlockSpec((1, Q_RANK), const2),
            pl.BlockSpec((Q_RANK, MLA_HEADS * QK_NOPE), const2),
            pl.BlockSpec((Q_RANK, MLA_HEADS * QK_ROPE), const2),
            pl.BlockSpec((MLA_HEADS, QK_NOPE, KV_RANK), lambda b, t: (0, 0, 0)),
            pl.BlockSpec((tq, LANES), lambda b, t: (t, 0)),
            pl.BlockSpec((tq, LANES), lambda b, t: (t, 0)),
        ],
        out_specs=[
            pl.BlockSpec((nb, MLA_HEADS, tq, qw), lambda b, t: (b, 0, t, 0)),
            pl.BlockSpec((nb, tq, 2048), lambda b, t: (b, t, 0)),
        ],
        out_shape=[jax.ShapeDtypeStruct((B, MLA_HEADS, T, qw), BF16),
                   jax.ShapeDtypeStruct((B, T, 2048), F32)],
        compiler_params=_cparams(2),
        name=f"b_pre_{tag}",
    )(x, g, w_cq, w_rest, g_q, w_qn, w_qr, w_kup, cos, sin)


def _flash_kernel(q_ref, knt_ref, krt_ref, v_ref, x_ref, z_ref, mk_ref, mv_ref, wout_ref, gf_ref, xo_ref,
                  m_sc, l_sc, acc_sc, tok_sc, mix_sc, *, tq, tk, final):
    i = pl.program_id(1)
    m_sc[...] = jnp.full(m_sc.shape, NEG, F32)
    l_sc[...] = jnp.zeros(l_sc.shape, F32)
    acc_sc[...] = jnp.zeros(acc_sc.shape, F32)

    def block(j, masked):
        krt = krt_ref[0, j]
        rows = pl.ds(pl.multiple_of(j * tk, tk), tk)
        if masked:
            keep = (lax.broadcasted_iota(jnp.int32, (tq, tk), 1)
                    <= lax.broadcasted_iota(jnp.int32, (tq, tk), 0))

        def scores(hd):
            kt = jnp.concatenate([knt_ref[0, j, hd * QK_NOPE:(hd + 1) * QK_NOPE, :], krt], axis=0)
            return _dot(q_ref[0, hd], kt)

        s_next = scores(0)
        for hd in range(MLA_HEADS):
            s = s_next
            if hd + 1 < MLA_HEADS:
                s_next = scores(hd + 1)
            if masked:
                s = jnp.where(keep, s, NEG)
            m_prev = m_sc[hd]
            m_new = jnp.maximum(m_prev, jnp.max(s, axis=-1, keepdims=True))
            alpha = jnp.exp2(m_prev - m_new)
            p = jnp.exp2(s - _rep(m_new, tk // LANES))
            p_lanes = p[:, 0:LANES]
            for c in range(1, tk // LANES):
                p_lanes = p_lanes + p[:, c * LANES:(c + 1) * LANES]
            l_sc[hd] = alpha * l_sc[hd] + p_lanes
            v = v_ref[0, rows, hd * V_HEAD:(hd + 1) * V_HEAD]
            acc_sc[hd] = alpha * acc_sc[hd] + _dot(p.astype(BF16), v)
            m_sc[hd] = m_new

    def body(j, carry):
        block(j, False)
        return carry

    lax.fori_loop(0, i, body, 0)
    block(i, True)

    for hd in range(MLA_HEADS):
        l = jnp.sum(l_sc[hd], axis=-1, keepdims=True)
        tok_sc[0, :, hd * V_HEAD:(hd + 1) * V_HEAD] = acc_sc[hd] * (1.0 / l)

    _mix_out(x_ref, tok_sc, z_ref, mk_ref, mv_ref, wout_ref, gf_ref, xo_ref, mix_sc, final)


def _flash_post(q, knt, krt, v, x, zrest, mem_k, mem_v, w_out, g_final, *, layer, tq, final):
    B, H, T, qw = q.shape
    tk = knt.shape[-1]
    assert tq == tk and qw == QK_NOPE + LANES
    kern = functools.partial(_flash_kernel, tq=tq, tk=tk, final=final)
    return pl.pallas_call(
        kern,
        grid=(B, T // tq),
        in_specs=[
            pl.BlockSpec((1, H, tq, qw), lambda b, i: (b, 0, i, 0)),
            pl.BlockSpec((1, T // tk, H * QK_NOPE, tk), lambda b, i: (b, 0, 0, 0)),
            pl.BlockSpec((1, T // tk, LANES, tk), lambda b, i: (b, 0, 0, 0)),
            pl.BlockSpec((1, T, MLA_WIDTH), lambda b, i: (b, 0, 0)),
            pl.BlockSpec((1, tq, D_MODEL), lambda b, i: (b, i, 0)),
            pl.BlockSpec((1, tq, 2048), lambda b, i: (b, i, 0)),
            pl.BlockSpec((1, 1, MEM_ROWS, MEM_HEAD_DIM), lambda b, i: (layer, b, 0, 0)),
            pl.BlockSpec((1, 1, MEM_ROWS, MEM_HEAD_DIM), lambda b, i: (layer, b, 0, 0)),
            pl.BlockSpec((1536, D_MODEL), lambda b, i: (0, 0)),
            pl.BlockSpec((1, D_MODEL), lambda b, i: (0, 0)),
        ],
        out_specs=pl.BlockSpec((1, tq, D_MODEL), lambda b, i: (b, i, 0)),
        out_shape=jax.ShapeDtypeStruct((B, T, D_MODEL), F32),
        scratch_shapes=[pltpu.VMEM((H, tq, LANES), F32), pltpu.VMEM((H, tq, LANES), F32),
                        pltpu.VMEM((H, tq, V_HEAD), F32), pltpu.VMEM((1, tq, MLA_WIDTH), F32),
                        pltpu.VMEM((tq, 1536), BF16)],
        compiler_params=_cparams(2),
        name=f"flash_post{layer}",
    )(q, knt, krt, v, x, zrest, mem_k, mem_v, w_out, g_final)


def _paged_kernel(pt_ref, q_ref, knew_ref, ckv_hbm, krt_hbm, o_ref,
                  ckv_buf, krt_buf, kbf_sc, s_sc, sem, m_sc, l_sc, acc_sc,
                  *, n_b, n_ch, cp, page, tp, n_new):
    total = n_b * n_ch

    def copies(bc):
        slot = bc % N_RAW
        b = bc // n_ch
        c = bc % n_ch
        out = []
        for p in range(cp):
            pg = pt_ref[b, c * cp + p]
            out.append(pltpu.make_async_copy(ckv_hbm.at[pg], ckv_buf.at[slot, pl.ds(p * page, page)],
                                             sem.at[0, slot]))
            out.append(pltpu.make_async_copy(krt_hbm.at[pg], krt_buf.at[slot, p], sem.at[1, slot]))
        return out

    def score(bc, w):
        slot = bc % N_RAW
        b = bc // n_ch
        for cpy in copies(bc):
            cpy.wait()
        q_lat = q_ref[b, :, 0:KV_RANK]
        q_rope = q_ref[b, :, KV_RANK:KV_RANK + QK_ROPE]
        kc = ckv_buf[slot].astype(BF16)
        krt = jnp.concatenate([krt_buf[slot, p] for p in range(cp)], axis=1).astype(BF16)
        kbf_sc[w] = kc
        s_sc[w] = _dot_nt(q_lat, kc) + _dot(q_rope, krt)

    def update(s, v_bf16, first):
        m_prev = jnp.where(first, NEG, m_sc[...])
        l_prev = jnp.where(first, 0.0, l_sc[...])
        acc_prev = jnp.where(first, 0.0, acc_sc[...])
        m_new = jnp.maximum(m_prev, jnp.max(s, axis=-1, keepdims=True))
        alpha = jnp.exp2(m_prev - m_new)
        p = jnp.exp2(s - _rep(m_new, s.shape[1] // LANES))
        l_sc[...] = alpha * l_prev + jnp.sum(p, axis=-1, keepdims=True)
        acc_sc[...] = _rep(alpha, KV_RANK // LANES) * acc_prev + _dot(p.astype(BF16), v_bf16)
        m_sc[...] = m_new

    def attend(bc, w):
        b = bc // n_ch
        c = bc % n_ch
        update(s_sc[w], kbf_sc[w], c == 0)

        def finish():
            kn = knew_ref[b]
            sn = _dot_nt(q_ref[b], kn)
            tpos = lax.broadcasted_iota(jnp.int32, sn.shape, 0) & (tp - 1)
            col = lax.broadcasted_iota(jnp.int32, sn.shape, 1)
            sn = jnp.where((col <= tpos) & (col < n_new), sn, NEG)
            update(sn, kn[:, 0:KV_RANK], False)
            o_ref[b] = acc_sc[...] * _rep(1.0 / l_sc[...], KV_RANK // LANES)

        last = c == n_ch - 1
        if isinstance(last, bool):
            if last:
                finish()
        else:
            pl.when(last)(finish)

    m_sc[...] = jnp.full(m_sc.shape, NEG, F32)
    l_sc[...] = jnp.zeros(l_sc.shape, F32)
    acc_sc[...] = jnp.zeros(acc_sc.shape, F32)
    for bc in range(min(N_RAW, total)):
        for cpy in copies(bc):
            cpy.start()
    score(0, 0)

    def step(i, w):
        def prefetch():
            for cpy in copies(i + N_RAW):
                cpy.start()

        more = i + N_RAW < total
        if isinstance(more, bool):
            if more:
                prefetch()
        else:
            pl.when(more)(prefetch)
        if isinstance(i, int) and i + 1 >= total:
            attend(i, w)
            return
        score(i + 1, 1 - w)
        attend(i, w)

    def body(k, carry):
        step(2 * k, 0)
        step(2 * k + 1, 1)
        return carry

    n_pairs = (total - 2) // 2
    lax.fori_loop(0, n_pairs, body, 0)
    for i in range(2 * n_pairs, total):
        step(i, i % 2)


def _paged_attention(page_table, q, knew, cache_ckv, cache_krt, *, cp, tp, n_new):
    n_b, n_pages = page_table.shape
    page = cache_ckv.shape[1]
    rows = q.shape[1]
    n_ch = n_pages // cp
    kern = functools.partial(_paged_kernel, n_b=n_b, n_ch=n_ch, cp=cp, page=page, tp=tp, n_new=n_new)
    grid_spec = pltpu.PrefetchScalarGridSpec(
        num_scalar_prefetch=1,
        grid=(1,),
        in_specs=[
            pl.BlockSpec(q.shape, lambda i, pt: (0, 0, 0)),
            pl.BlockSpec(knew.shape, lambda i, pt: (0, 0, 0)),
            pl.BlockSpec(memory_space=pl.ANY),
            pl.BlockSpec(memory_space=pl.ANY),
        ],
        out_specs=pl.BlockSpec((n_b, rows, KV_RANK), lambda i, pt: (0, 0, 0)),
        scratch_shapes=[
            pltpu.VMEM((N_RAW, cp * page, KV_RANK), F32),
            pltpu.VMEM((N_RAW, cp, QK_ROPE, page), F32),
            pltpu.VMEM((2, cp * page, KV_RANK), BF16),
            pltpu.VMEM((2, rows, cp * page), F32),
            pltpu.SemaphoreType.DMA((2, N_RAW)),
            pltpu.VMEM((rows, LANES), F32),
            pltpu.VMEM((rows, LANES), F32),
            pltpu.VMEM((rows, KV_RANK), F32),
        ],
    )
    return pl.pallas_call(
        kern,
        grid_spec=grid_spec,
        out_shape=jax.ShapeDtypeStruct((n_b, rows, KV_RANK), F32),
        compiler_params=_cparams(1),
        name="paged_attention",
    )(page_table, q, knew, cache_ckv, cache_krt)


def _v_up_kernel(o_ref, wv_ref, tok_ref):
    n_b, _, tp, _ = o_ref.shape
    for hd in range(MLA_HEADS):
        oh = o_ref[:, hd].reshape(n_b * tp, KV_RANK).astype(BF16)
        tok_ref[:, :, hd * V_HEAD:(hd + 1) * V_HEAD] = _dot(oh, wv_ref[hd]).reshape(n_b, tp, V_HEAD)


def _v_up(o, w_vup):
    n_b, H, tp, _ = o.shape
    return pl.pallas_call(
        _v_up_kernel,
        out_shape=jax.ShapeDtypeStruct((n_b, tp, MLA_WIDTH), F32),
        compiler_params=pltpu.CompilerParams(vmem_limit_bytes=VMEM_LIMIT),
        name="v_up_sample",
    )(o, w_vup)


def _b_post_kernel(x_ref, tok_ref, z_ref, mk_ref, mv_ref, wout_ref, gf_ref, xo_ref, mix_sc, *, final):
    _mix_out(x_ref, tok_ref, z_ref, mk_ref, mv_ref, wout_ref, gf_ref, xo_ref, mix_sc, final)


def _mix_out(x_ref, tok_ref, z_ref, mk_ref, mv_ref, wout_ref, gf_ref, xo_ref, mix_sc, final):
    nb, tq, _ = x_ref.shape
    m = nb * tq
    for g in range(4):
        c0, c1 = g * 256, (g + 1) * 256
        mix_sc[:, c0:c1] = (tok_ref[:, :, c0:c1] * _silu(z_ref[:, :, c0:c1])).reshape(m, 256).astype(BF16)
    mem_o = _mem_attend(z_ref[:, :, 1024:1536].reshape(m, MEM_WIDTH), mk_ref, mv_ref, nb, tq)
    for hd in range(MEM_HEADS):
        c0, c1 = hd * MEM_HEAD_DIM, (hd + 1) * MEM_HEAD_DIM
        gate = z_ref[:, :, 1536 + c0:1536 + c1].reshape(m, MEM_HEAD_DIM)
        mix_sc[:, 1024 + c0:1024 + c1] = (mem_o[hd] * _silu(gate)).astype(BF16)
    acc = x_ref[...].reshape(m, D_MODEL) + _dot(mix_sc[...], wout_ref[...])
    if final:
        acc = _rms(acc, gf_ref[...])
    xo_ref[...] = acc.reshape(nb, tq, D_MODEL)


def _b_post(x, tok, zrest, mem_k, mem_v, w_out, g_final, *, layer, nb, tq, final, tag):
    B, T, _ = x.shape
    kern = functools.partial(_b_post_kernel, final=final)
    return pl.pallas_call(
        kern,
        grid=(B // nb, T // tq),
        in_specs=[
            pl.BlockSpec((nb, tq, D_MODEL), lambda b, t: (b, t, 0)),
            pl.BlockSpec((nb, tq, MLA_WIDTH), lambda b, t: (b, t, 0)),
            pl.BlockSpec((nb, tq, 2048), lambda b, t: (b, t, 0)),
            pl.BlockSpec((1, nb, MEM_ROWS, MEM_HEAD_DIM), lambda b, t: (layer, b, 0, 0)),
            pl.BlockSpec((1, nb, MEM_ROWS, MEM_HEAD_DIM), lambda b, t: (layer, b, 0, 0)),
            pl.BlockSpec((1536, D_MODEL), lambda b, t: (0, 0)),
            pl.BlockSpec((1, D_MODEL), lambda b, t: (0, 0)),
        ],
        out_specs=pl.BlockSpec((nb, tq, D_MODEL), lambda b, t: (b, t, 0)),
        out_shape=jax.ShapeDtypeStruct((B, T, D_MODEL), F32),
        scratch_shapes=[pltpu.VMEM((nb * tq, 1536), BF16)],
        compiler_params=_cparams(2),
        name=f"b_post{layer}_{tag}",
    )(x, tok, zrest, mem_k, mem_v, w_out, g_final)


def _rope_tables(pos):
    half = QK_ROPE // 2
    inv = ROPE_THETA ** (-jnp.arange(half, dtype=F32) / half)
    ang = pos.astype(F32)[:, None] * inv[None, :]
    return jnp.tile(jnp.cos(ang), (1, LANES // half)), jnp.tile(jnp.sin(ang), (1, LANES // half))


def kernel(x_prompt, x_sample, state_pool, cache_ckv, cache_krope, cache_mem_k, cache_mem_v, page_table, mem_prompt, g_norm, w_in_a, w_pool_grp, pool_scale, w_in_b, g_q_latent, w_q_up, g_kv_in, w_kv_down, g_kv_latent, w_k_up, w_v_up, g_mem, w_mem_k, w_mem_v, w_out, g_final):
    B, T, _ = x_prompt.shape
    SB, ST, _ = x_sample.shape
    TP = SUBLANES
    past = page_table.shape[1] * cache_ckv.shape[1]

    w_in_a16 = w_in_a.astype(BF16)
    w_grp16 = w_pool_grp.astype(BF16)
    w_out16 = w_out.astype(BF16)
    w_cq16 = w_in_b[:, :, :Q_RANK].astype(BF16)
    w_rest16 = w_in_b[:, :, Q_RANK:].astype(BF16)
    wq = w_q_up.reshape(N_A, Q_RANK, MLA_HEADS, QK_NOPE + QK_ROPE)
    w_qn16 = wq[..., :QK_NOPE].reshape(N_A, Q_RANK, MLA_HEADS * QK_NOPE).astype(BF16)
    w_qr16 = wq[..., QK_NOPE:].reshape(N_A, Q_RANK, MLA_HEADS * QK_ROPE).astype(BF16)
    w_kup16 = jnp.transpose(w_k_up, (1, 2, 0)).astype(BF16)
    w_vup16 = jnp.transpose(w_v_up, (1, 0, 2)).astype(BF16)
    w_kv16 = jnp.pad(w_kv_down, ((0, 0), (0, KCAT - KV_RANK - QK_ROPE))).astype(BF16)
    w_mk16 = w_mem_k.astype(BF16)
    w_mv16 = w_mem_v.astype(BF16)
    g_kv_in2 = g_kv_in[None]
    g_lat2 = g_kv_latent[None]
    g_final2 = g_final[None]

    w_kupt16 = w_kup16.reshape(MLA_HEADS * QK_NOPE, KV_RANK)
    w_vall16 = w_v_up.reshape(KV_RANK, MLA_WIDTH).astype(BF16)

    def trunk(x, prev, mem_k, mem_v, pos, *, tag, nb, tq_a, tq, tq_b, heads, n_valid, attend):
        pools = []
        for l in range(N_A):
            x, pool = _a_layer(x, prev[l], g_norm[l][None], w_in_a16[l], w_grp16[l], pool_scale[l][None],
                               mem_k, mem_v, w_out16[l], layer=l, nb=nb, tq=tq_a, n_valid=n_valid, pos0=pos)
            pools.append(pool[:, 1:])
        cos, sin = _rope_tables(pos + jnp.arange(x.shape[1], dtype=jnp.int32))
        ckv, krope, *keys = _kv_latent(x, g_kv_in2, w_kv16, g_lat2, cos, sin, w_kupt16, w_vall16,
                                       nb=nb, tq=tq, heads=heads)
        for j in range(DEPTH - N_A):
            l = N_A + j
            q, zrest = _b_pre(x, g_norm[l][None], w_cq16[j], w_rest16[j], g_q_latent[j][None],
                              w_qn16[j], w_qr16[j], w_kup16, cos, sin, nb=nb, tq=tq_b, absorb=not heads,
                              tag=f"{tag}{l}")
            x = attend(x, q, zrest, *keys, layer=l, final=(l == DEPTH - 1))
        return x, jnp.stack(pools, axis=0), ckv, krope

    mem_k_p, mem_v_p = _mem_project(mem_prompt, g_mem, w_mk16, w_mv16)
    prev_p = jnp.zeros((N_A, B, HALO, D_MODEL), F32)

    def attend_prompt(x, q, zrest, knt, krt, v, *, layer, final):
        return _flash_post(q, knt, krt, v, x, zrest, mem_k_p, mem_v_p, w_out16[layer], g_final2,
                           layer=layer, tq=256, final=final)

    y_p, pool_p, ckv_p, krope_p = trunk(
        x_prompt, prev_p, mem_k_p, mem_v_p, 0, tag="p", nb=1, tq_a=512, tq=256, tq_b=512, heads=True, n_valid=512,
        attend=attend_prompt)

    xs = jnp.pad(x_sample, ((0, 0), (0, TP - ST), (0, 0)))
    prev_s = jnp.pad(state_pool, ((0, 0), (0, 0), (HALO - POOL_BUF, 0), (0, 0)))
    mem_k_s = cache_mem_k.reshape(DEPTH, SB, MEM_ROWS, MEM_HEAD_DIM)
    mem_v_s = cache_mem_v.reshape(DEPTH, SB, MEM_ROWS, MEM_HEAD_DIM)
    cache_krt = jnp.transpose(cache_krope, (0, 2, 1))

    def attend_sample(x, q, zrest, kcat, *, layer, final):
        qf = q.reshape(SB, MLA_HEADS * TP, KCAT)
        knew = jnp.pad(kcat, ((0, 0), (0, LANES - TP), (0, 0)))
        o = _paged_attention(page_table, qf, knew, cache_ckv, cache_krt, cp=16, tp=TP, n_new=ST)
        tok = _v_up(o.reshape(SB, MLA_HEADS, TP, KV_RANK), w_vup16)
        return _b_post(x, tok, zrest, mem_k_s, mem_v_s, w_out16[layer], g_final2, layer=layer, nb=8, tq=TP,
                       final=final, tag="s")

    y_s, pool_s, ckv_s, krope_s = trunk(
        xs, prev_s, mem_k_s, mem_v_s, past, tag="s", nb=8, tq_a=TP, tq=TP, tq_b=TP, heads=False, n_valid=ST,
        attend=attend_sample)

    mem_shape = (DEPTH, B, MEM_TOKENS, MEM_HEADS, MEM_HEAD_DIM)
    return (y_p, y_s[:, :ST], pool_p, pool_s, ckv_p, krope_p, ckv_s[:, :ST], krope_s[:, :ST],
            mem_k_p.reshape(mem_shape), mem_v_p.reshape(mem_shape))
```

```python
import functools

import jax
import jax.numpy as jnp
from jax import lax
from jax.experimental import pallas as pl
from jax.experimental.pallas import tpu as pltpu

F32 = jnp.float32
BF16 = jnp.bfloat16

D_MODEL = 1024
DEPTH = 4
N_A = 2
POOL_WINDOWS = (2, 4, 8, 16)
POOL_GROUP = 256
POOL_BUF = 15
MLA_HEADS = 8
QK_NOPE = 128
QK_ROPE = 64
V_HEAD = 128
KV_RANK = 256
Q_RANK = 384
MLA_WIDTH = MLA_HEADS * V_HEAD
MLA_SCALE = (QK_NOPE + QK_ROPE) ** -0.5
ROPE_THETA = 10000.0
MEM_TOKENS = 256
MEM_HEADS = 4
MEM_HEAD_DIM = 128
MEM_WIDTH = MEM_HEADS * MEM_HEAD_DIM
MEM_SCALE = MEM_HEAD_DIM ** -0.5
POOL_WIDTH = D_MODEL
IN_A = 2 * POOL_WIDTH + 2 * MEM_WIDTH
Z_REST = MLA_WIDTH + 2 * MEM_WIDTH
OUT_W = POOL_WIDTH + MEM_WIDTH
EPS = 1e-6
NEG = -1e30
LOG2E = 1.4426950408889634

KCAT = 384
SUBLANES = 8
LANES = 128
HALO = 16
E_OFF = SUBLANES + HALO
VMEM_LIMIT = 56 * 1024 * 1024
MEM_ROWS = MEM_TOKENS * MEM_HEADS
ATT_TILE = 256
N_RAW = 4


def _cparams(n_axes):
    return pltpu.CompilerParams(dimension_semantics=("arbitrary",) * n_axes,
                                vmem_limit_bytes=VMEM_LIMIT)


def _rms(x, g):
    return x * lax.rsqrt(jnp.mean(x * x, axis=-1, keepdims=True) + EPS) * g


def _dot(a, b):
    return jnp.dot(a, b, preferred_element_type=F32)


def _dot_nt(a, b):
    return lax.dot_general(a, b, (((1,), (1,)), ((), ())), preferred_element_type=F32)


def _silu(x):
    return x * jax.nn.sigmoid(x)


def _rope_lanes(a, cos, sin):
    lane = lax.broadcasted_iota(jnp.int32, a.shape, 1)
    up = pltpu.roll(a, 32, 1)
    dn = pltpu.roll(a, 96, 1)
    rot = jnp.where((lane & 63) < 32, -dn, up)
    return a * cos + rot * sin


def _rep(x, n):
    return x if n == 1 else jnp.concatenate([x] * n, axis=1)


def _tile_rows(x, n):
    return x if n == 1 else jnp.concatenate([x] * n, axis=0)


def _mem_attend(qm, mk_ref, mv_ref, nb, tq):
    rows, keys = nb * tq, nb * MEM_TOKENS
    if nb > 1:
        row_seq = lax.broadcasted_iota(jnp.int32, (rows, keys), 0) // tq
        col_seq = lax.broadcasted_iota(jnp.int32, (rows, keys), 1) // MEM_TOKENS
        keep = row_seq == col_seq
    outs = []
    for h in range(MEM_HEADS):
        c0, c1 = h * MEM_HEAD_DIM, (h + 1) * MEM_HEAD_DIM
        q = (qm[:, c0:c1] * (MEM_SCALE * LOG2E)).astype(BF16)
        k = mk_ref[0, :, pl.ds(h, MEM_TOKENS, stride=MEM_HEADS), :].reshape(keys, MEM_HEAD_DIM).astype(BF16)
        v = mv_ref[0, :, pl.ds(h, MEM_TOKENS, stride=MEM_HEADS), :].reshape(keys, MEM_HEAD_DIM).astype(BF16)
        s = _dot_nt(q, k)
        if nb > 1:
            s = jnp.where(keep, s, NEG)
        m = jnp.max(s, axis=-1, keepdims=True)
        p = jnp.exp2(s - m)
        l = jnp.sum(p, axis=-1, keepdims=True)
        outs.append(_dot(p.astype(BF16), v) * (1.0 / l))
    return outs


def _mem_proj_kernel(mem_ref, g_ref, wk_ref, wv_ref, mk_ref, mv_ref):
    x = mem_ref[0]
    xn = x * lax.rsqrt(jnp.mean(x * x, axis=-1, keepdims=True) + EPS)
    for l in range(DEPTH):
        h = (xn * g_ref[l:l + 1, :]).astype(BF16)
        mk = _dot(h, wk_ref[l])
        mv = _dot(h, wv_ref[l])
        for hd in range(MEM_HEADS):
            c0, c1 = hd * MEM_HEAD_DIM, (hd + 1) * MEM_HEAD_DIM
            mk_ref[l, 0, pl.ds(hd, MEM_TOKENS, stride=MEM_HEADS), :] = mk[:, c0:c1]
            mv_ref[l, 0, pl.ds(hd, MEM_TOKENS, stride=MEM_HEADS), :] = mv[:, c0:c1]


def _mem_project(mem, g_mem, wk, wv):
    B, M, _ = mem.shape
    out = jax.ShapeDtypeStruct((DEPTH, B, MEM_ROWS, MEM_HEAD_DIM), F32)
    return pl.pallas_call(
        _mem_proj_kernel,
        grid=(B,),
        in_specs=[
            pl.BlockSpec((1, M, D_MODEL), lambda b: (b, 0, 0)),
            pl.BlockSpec((DEPTH, D_MODEL), lambda b: (0, 0)),
            pl.BlockSpec((DEPTH, D_MODEL, MEM_WIDTH), lambda b: (0, 0, 0)),
            pl.BlockSpec((DEPTH, D_MODEL, MEM_WIDTH), lambda b: (0, 0, 0)),
        ],
        out_specs=[
            pl.BlockSpec((DEPTH, 1, MEM_ROWS, MEM_HEAD_DIM), lambda b: (0, b, 0, 0)),
            pl.BlockSpec((DEPTH, 1, MEM_ROWS, MEM_HEAD_DIM), lambda b: (0, b, 0, 0)),
        ],
        out_shape=[out, out],
        compiler_params=_cparams(1),
        name="mem_project",
    )(mem, g_mem, wk, wv)


def _a_layer_kernel(x_ref, prev_ref, g_ref, win_ref, wgrp_ref, pscale_ref, mk_ref, mv_ref, wout_ref,
                    xo_ref, pool_ref, e_sc, sa_sc, sb_sc, mix_sc, *, nb, tq, n_t, n_valid, pos0):
    t = pl.program_id(1)
    rows = HALO + tq
    m = nb * tq

    @pl.when(t == 0)
    def _():
        zeros = jnp.zeros((nb, SUBLANES, D_MODEL), F32)
        e_sc[:, 0:SUBLANES, :] = zeros
        sa_sc[:, 0:SUBLANES, :] = zeros
        sb_sc[:, 0:SUBLANES, :] = zeros
        e_sc[:, SUBLANES:E_OFF, :] = prev_ref[...]

    x = x_ref[...].reshape(m, D_MODEL)
    h = _rms(x, g_ref[...]).astype(BF16)
    e_sc[:, E_OFF:E_OFF + tq, :] = _dot(h, win_ref[:, 0:POOL_WIDTH]).reshape(nb, tq, POOL_WIDTH)

    lo = SUBLANES
    sa_sc[:, lo:lo + rows, :] = e_sc[:, lo:lo + rows, :] + e_sc[:, lo - 1:lo - 1 + rows, :]
    sb_sc[:, lo:lo + rows, 256:1024] = (sa_sc[:, lo:lo + rows, 256:1024]
                                        + sa_sc[:, lo - 2:lo - 2 + rows, 256:1024])
    sa_sc[:, lo:lo + rows, 512:1024] = (sb_sc[:, lo:lo + rows, 512:1024]
                                        + sb_sc[:, lo - 4:lo - 4 + rows, 512:1024])
    s16 = sa_sc[:, E_OFF:E_OFF + tq, 768:1024] + sa_sc[:, E_OFF - 8:E_OFF - 8 + tq, 768:1024]
    sums = (sa_sc[:, E_OFF:E_OFF + tq, 0:256], sb_sc[:, E_OFF:E_OFF + tq, 256:512],
            sa_sc[:, E_OFF:E_OFF + tq, 512:768], s16)

    pos = pos0 + t * tq + lax.broadcasted_iota(jnp.int32, (1, tq, 1), 1)
    for g, w in enumerate(POOL_WINDOWS):
        c0, c1 = g * POOL_GROUP, (g + 1) * POOL_GROUP
        inv_cnt = 1.0 / jnp.minimum(pos + 1, w).astype(F32)
        pooled = (sums[g] * inv_cnt - e_sc[:, E_OFF:E_OFF + tq, c0:c1]).reshape(m, POOL_GROUP)
        mixed = _dot(pooled.astype(BF16), wgrp_ref[g]) * pscale_ref[:, c0:c1]
        gate = _dot(h, win_ref[:, POOL_WIDTH + c0:POOL_WIDTH + c1])
        mix_sc[:, c0:c1] = (mixed * _silu(gate)).astype(BF16)

    qm = _dot(h, win_ref[:, 2 * POOL_WIDTH:2 * POOL_WIDTH + MEM_WIDTH])
    gate_m = _dot(h, win_ref[:, 2 * POOL_WIDTH + MEM_WIDTH:IN_A])
    mem_o = _mem_attend(qm, mk_ref, mv_ref, nb, tq)
    for hd in range(MEM_HEADS):
        c0, c1 = hd * MEM_HEAD_DIM, (hd + 1) * MEM_HEAD_DIM
        mix_sc[:, POOL_WIDTH + c0:POOL_WIDTH + c1] =(mem_o[hd] * _silu(gate_m[:, c0:c1])).astype(BF16)
    xo_ref[...] = (x + _dot(mix_sc[...], wout_ref[...])).reshape(nb, tq, D_MODEL)

    @pl.when(t == n_t - 1)
    def _():
        pool_ref[...] = e_sc[:, E_OFF + n_valid - HALO:E_OFF + n_valid, :]

    if n_t > 1:
        @pl.when(t < n_t - 1)
        def _():
            e_sc[:, SUBLANES:E_OFF, :] = e_sc[:, SUBLANES + tq:E_OFF + tq, :]


def _a_layer(x, prev, g, w_in, w_grp, pscale, mem_k, mem_v, w_out, *, layer, nb, tq, n_valid, pos0):
    B, T, _ = x.shape
    n_t = T // tq
    rows = E_OFF + tq
    kern = functools.partial(_a_layer_kernel, nb=nb, tq=tq, n_t=n_t, n_valid=n_valid, pos0=pos0)
    return pl.pallas_call(
        kern,
        grid=(B // nb, n_t),
        in_specs=[
            pl.BlockSpec((nb, tq, D_MODEL), lambda b, t: (b, t, 0)),
            pl.BlockSpec((nb, HALO, D_MODEL), lambda b, t: (b, 0, 0)),
            pl.BlockSpec((1, D_MODEL), lambda b, t: (0, 0)),
            pl.BlockSpec((None, D_MODEL, IN_A), lambda b, t: (layer, 0, 0)),
            pl.BlockSpec((None, 4, POOL_GROUP, POOL_GROUP), lambda b, t: (layer, 0, 0, 0)),
            pl.BlockSpec((1, D_MODEL), lambda b, t: (0, 0)),
            pl.BlockSpec((1, nb, MEM_ROWS, MEM_HEAD_DIM), lambda b, t: (layer, b, 0, 0)),
            pl.BlockSpec((1, nb, MEM_ROWS, MEM_HEAD_DIM), lambda b, t: (layer, b, 0, 0)),
            pl.BlockSpec((None, OUT_W, D_MODEL), lambda b, t: (layer, 0, 0)),
        ],
        out_specs=[
            pl.BlockSpec((nb, tq, D_MODEL), lambda b, t: (b, t, 0)),
            pl.BlockSpec((nb, HALO, D_MODEL), lambda b, t: (b, 0, 0)),
        ],
        out_shape=[jax.ShapeDtypeStruct((B, T, D_MODEL), F32),
                   jax.ShapeDtypeStruct((B, HALO, D_MODEL), F32)],
        scratch_shapes=[pltpu.VMEM((nb, rows, D_MODEL), F32)] * 3 + [pltpu.VMEM((nb * tq, OUT_W), BF16)],
        compiler_params=_cparams(2),
        name=f"a_layer{layer}_{B}",
    )(x, prev, g, w_in, w_grp, pscale, mem_k, mem_v, w_out)


def _kv_latent_kernel(x_ref, gin_ref, wkv_ref, glat_ref, cos_ref, sin_ref, wkupt_ref, wvup_ref,
                      ckv_ref, kr_ref, *out_refs, heads):
    nb, tq, _ = x_ref.shape
    m = nb * tq
    h = _rms(x_ref[...].reshape(m, D_MODEL), gin_ref[...]).astype(BF16)
    kv = _dot(h, wkv_ref[...])
    ckv = _rms(kv[:, 0:KV_RANK], glat_ref[...])
    kr = _rope_lanes(kv[:, KV_RANK:KCAT], _tile_rows(cos_ref[...], nb), _tile_rows(sin_ref[...], nb))
    ckv_ref[...] = ckv.reshape(nb, tq, KV_RANK)
    kr_ref[...] = kr[:, 0:QK_ROPE].reshape(nb, tq, QK_ROPE)
    if not heads:
        kcat_ref, = out_refs
        kcat_ref[:, :, 0:KV_RANK] = ckv.reshape(nb, tq, KV_RANK).astype(BF16)
        kcat_ref[:, :, KV_RANK:KCAT] = kr.reshape(nb, tq, LANES).astype(BF16)
    else:
        knt_ref, krt_ref, v_ref = out_refs
        tk = knt_ref.shape[-1]
        v_ref[0] = _dot(ckv.astype(BF16), wvup_ref[...]).astype(BF16)
        for sub in range(tq // tk):
            r0 = sub * tk
            ckv_t = ckv[r0:r0 + tk, :].T.astype(BF16)
            knt_ref[0, sub] = _dot(wkupt_ref[...], ckv_t).astype(BF16)
            krt_ref[0, sub] = kr[r0:r0 + tk, :].T.astype(BF16)


def _kv_latent(x, g_in, w_kv, g_lat, cos, sin, w_kupt, w_vup, *, nb, tq, tk, heads):
    B, T, _ = x.shape
    assert nb == 1 or not heads
    n_sub = tq // tk
    out_specs = [
        pl.BlockSpec((nb, tq, KV_RANK), lambda b, t: (b, t, 0)),
        pl.BlockSpec((nb, tq, QK_ROPE), lambda b, t: (b, t, 0)),
    ]
    out_shape = [jax.ShapeDtypeStruct((B, T, KV_RANK), F32),
                 jax.ShapeDtypeStruct((B, T, QK_ROPE), F32)]
    if heads:
        out_specs += [pl.BlockSpec((1, n_sub, MLA_HEADS * QK_NOPE, tk), lambda b, t: (b, t, 0, 0)),
                      pl.BlockSpec((1, n_sub, LANES, tk), lambda b, t: (b, t, 0, 0)),
                      pl.BlockSpec((1, tq, MLA_WIDTH), lambda b, t: (b, t, 0))]
        out_shape += [jax.ShapeDtypeStruct((B, T // tk, MLA_HEADS * QK_NOPE, tk), BF16),
                      jax.ShapeDtypeStruct((B, T // tk, LANES, tk), BF16),
                      jax.ShapeDtypeStruct((B, T, MLA_WIDTH), BF16)]
    else:
        out_specs.append(pl.BlockSpec((nb, tq, KCAT), lambda b, t: (b, t, 0)))
        out_shape.append(jax.ShapeDtypeStruct((B, T, KCAT), BF16))
    return pl.pallas_call(
        functools.partial(_kv_latent_kernel, heads=heads),
        grid=(B // nb, T // tq),
        in_specs=[
            pl.BlockSpec((nb, tq, D_MODEL), lambda b, t: (b, t, 0)),
            pl.BlockSpec((1, D_MODEL), lambda b, t: (0, 0)),
            pl.BlockSpec((D_MODEL, KCAT), lambda b, t: (0, 0)),
            pl.BlockSpec((1, KV_RANK), lambda b, t: (0, 0)),
            pl.BlockSpec((tq, LANES), lambda b, t: (t, 0)),
            pl.BlockSpec((tq, LANES), lambda b, t: (t, 0)),
            pl.BlockSpec((MLA_HEADS * QK_NOPE, KV_RANK), lambda b, t: (0, 0)),
            pl.BlockSpec((KV_RANK, MLA_WIDTH), lambda b, t: (0, 0)),
        ],
        out_specs=out_specs,
        out_shape=out_shape,
        compiler_params=_cparams(2),
        name=f"kv_latent_{B}",
    )(x, g_in, w_kv, g_lat, cos, sin, w_kupt, w_vup)


def _b_pre_kernel(x_ref, g_ref, wcq_ref, wrest_ref, gq_ref, wqn_ref, wqr_ref, wkup_ref, cos_ref, sin_ref,
                  q_ref, zrest_ref, *, absorb):
    nb, tq, _ = x_ref.shape
    m = nb * tq
    n0 = KV_RANK if absorb else QK_NOPE
    h = _rms(x_ref[...].reshape(m, D_MODEL), g_ref[...]).astype(BF16)
    zrest_ref[...] = _dot(h, wrest_ref[...]).reshape(nb, tq, Z_REST)
    cn = _rms(_dot(h, wcq_ref[...]), gq_ref[...]).astype(BF16)
    qn = _dot(cn, wqn_ref[...])
    qr = _dot(cn, wqr_ref[...])
    for hd in range(MLA_HEADS):
        ql = qn[:, hd * QK_NOPE:(hd + 1) * QK_NOPE]
        if absorb:
            ql = _dot(ql.astype(BF16), wkup_ref[hd])
        q_ref[:, hd, :, 0:n0] = (ql * (MLA_SCALE * LOG2E)).reshape(nb, tq, n0).astype(BF16)
    cos = _tile_rows(cos_ref[...], nb)
    sin = _tile_rows(sin_ref[...], nb)
    lane = lax.broadcasted_iota(jnp.int32, cos.shape, 1)
    for j in range(MLA_HEADS // 2):
        rr = _rope_lanes(qr[:, j * LANES:(j + 1) * LANES], cos, sin) * (MLA_SCALE * LOG2E)
        even = jnp.where(lane < QK_ROPE, rr, 0.0)
        odd = jnp.where(lane < QK_ROPE, pltpu.roll(rr, 64, 1), 0.0)
        q_ref[:, 2 * j, :, n0:n0 + LANES] = even.reshape(nb, tq, LANES).astype(BF16)
        q_ref[:, 2 * j + 1, :, n0:n0 + LANES] = odd.reshape(nb, tq, LANES).astype(BF16)


def _b_pre(x, g, w_cq, w_rest, g_q, w_qn, w_qr, w_kup, cos, sin, *, j, nb, tq, absorb, tag):
    B, T, _ = x.shape
    const2 = lambda b, t: (0, 0)
    qw = (KV_RANK if absorb else QK_NOPE) + LANES
    return pl.pallas_call(
        functools.partial(_b_pre_kernel, absorb=absorb),
        grid=(B // nb, T // tq),
        in_specs=[
            pl.BlockSpec((nb, tq, D_MODEL), lambda b, t: (b, t, 0)),
            pl.BlockSpec((1, D_MODEL), const2),
            pl.BlockSpec((None, D_MODEL, Q_RANK), lambda b, t: (j, 0, 0)),
            pl.BlockSpec((None, D_MODEL, Z_REST), lambda b, t: (j, 0, 0)),
            pl.BlockSpec((1, Q_RANK), const2),
            pl.BlockSpec((None, Q_RANK, MLA_HEADS * QK_NOPE), lambda b, t: (j, 0, 0)),
            pl.BlockSpec((None, Q_RANK, MLA_HEADS * QK_ROPE), lambda b, t: (j, 0, 0)),
            pl.BlockSpec((MLA_HEADS, QK_NOPE, KV_RANK), lambda b, t: (0, 0, 0)),
            pl.BlockSpec((tq, LANES), lambda b, t: (t, 0)),
            pl.BlockSpec((tq, LANES), lambda b, t: (t, 0)),
        ],
        out_specs=[
            pl.BlockSpec((nb, MLA_HEADS, tq, qw), lambda b, t: (b, 0, t, 0)),
            pl.BlockSpec((nb, tq, Z_REST), lambda b, t: (b, t, 0)),
        ],
        out_shape=[jax.ShapeDtypeStruct((B, MLA_HEADS, T, qw), BF16),
                   jax.ShapeDtypeStruct((B, T, Z_REST), F32)],
        compiler_params=_cparams(2),
        name=f"b_pre_{tag}",
    )(x, g, w_cq, w_rest, g_q, w_qn, w_qr, w_kup, cos, sin)


def _flash_kernel(q_ref, knt_ref, krt_ref, v_ref, x_ref, z_ref, mk_ref, mv_ref, wout_ref, gf_ref, xo_ref,
                  m_sc, l_sc, acc_sc, tok_sc, mix_sc, *, tq, tk, final):
    i = pl.program_id(1)
    m_sc[...] = jnp.full(m_sc.shape, NEG, F32)
    l_sc[...] = jnp.zeros(l_sc.shape, F32)
    acc_sc[...] = jnp.zeros(acc_sc.shape, F32)

    def block(j, masked):
        krt = krt_ref[0, j]
        rows = pl.ds(pl.multiple_of(j * tk, tk), tk)
        if masked:
            keep = (lax.broadcasted_iota(jnp.int32, (tq, tk), 1)
                    <= lax.broadcasted_iota(jnp.int32, (tq, tk), 0))

        def scores(hd):
            kt = jnp.concatenate([knt_ref[0, j, hd * QK_NOPE:(hd + 1) * QK_NOPE, :], krt], axis=0)
            return _dot(q_ref[0, hd], kt)

        s_next = scores(0)
        for hd in range(MLA_HEADS):
            s = s_next
            if hd + 1 < MLA_HEADS:
                s_next = scores(hd + 1)
            if masked:
                s = jnp.where(keep, s, NEG)
            m_prev = m_sc[hd]
            m_new = jnp.maximum(m_prev, jnp.max(s, axis=-1, keepdims=True))
            alpha = jnp.exp2(m_prev - m_new)
            p = jnp.exp2(s - _rep(m_new, tk // LANES))
            p_lanes = p[:, 0:LANES]
            for c in range(1, tk // LANES):
                p_lanes = p_lanes + p[:, c * LANES:(c + 1) * LANES]
            l_sc[hd] = alpha * l_sc[hd] + p_lanes
            v = v_ref[0, rows, hd * V_HEAD:(hd + 1) * V_HEAD]
            acc_sc[hd] = alpha * acc_sc[hd] + _dot(p.astype(BF16), v)
            m_sc[hd] = m_new

    def body(j, carry):
        block(j, False)
        return carry

    lax.fori_loop(0, i, body, 0)
    block(i, True)

    for hd in range(MLA_HEADS):
        l = jnp.sum(l_sc[hd], axis=-1, keepdims=True)
        tok_sc[0, :, hd * V_HEAD:(hd + 1) * V_HEAD] = acc_sc[hd] * (1.0 / l)

    _mix_out(x_ref, tok_sc, z_ref, mk_ref, mv_ref, wout_ref, gf_ref, xo_ref, mix_sc, final)


def _flash_post(q, knt, krt, v, x, zrest, mem_k, mem_v, w_out, g_final, *, layer, tq, final):
    B, H, T, qw = q.shape
    tk = knt.shape[-1]
    assert tq == tk and qw == QK_NOPE + LANES
    kern = functools.partial(_flash_kernel, tq=tq, tk=tk, final=final)
    return pl.pallas_call(
        kern,
        grid=(B, T // tq),
        in_specs=[
            pl.BlockSpec((1, H, tq, qw), lambda b, i: (b, 0, i, 0)),
            pl.BlockSpec((1, T // tk, H * QK_NOPE, tk), lambda b, i: (b, 0, 0, 0)),
            pl.BlockSpec((1, T // tk, LANES, tk), lambda b, i: (b, 0, 0, 0)),
            pl.BlockSpec((1, T, MLA_WIDTH), lambda b, i: (b, 0, 0)),
            pl.BlockSpec((1, tq, D_MODEL), lambda b, i: (b, i, 0)),
            pl.BlockSpec((1, tq, Z_REST), lambda b, i: (b, i, 0)),
            pl.BlockSpec((1, 1, MEM_ROWS, MEM_HEAD_DIM), lambda b, i: (layer, b, 0, 0)),
            pl.BlockSpec((1, 1, MEM_ROWS, MEM_HEAD_DIM), lambda b, i: (layer, b, 0, 0)),
            pl.BlockSpec((None, OUT_W, D_MODEL), lambda b, i: (layer, 0, 0)),
            pl.BlockSpec((1, D_MODEL), lambda b, i: (0, 0)),
        ],
        out_specs=pl.BlockSpec((1, tq, D_MODEL), lambda b, i: (b, i, 0)),
        out_shape=jax.ShapeDtypeStruct((B, T, D_MODEL), F32),
        scratch_shapes=[pltpu.VMEM((H, tq, LANES), F32), pltpu.VMEM((H, tq, LANES), F32),
                        pltpu.VMEM((H, tq, V_HEAD), F32), pltpu.VMEM((1, tq, MLA_WIDTH), F32),
                        pltpu.VMEM((tq, OUT_W), BF16)],
        compiler_params=_cparams(2),
        name=f"flash_post{layer}",
    )(q, knt, krt, v, x, zrest, mem_k, mem_v, w_out, g_final)


def _paged_kernel(pt_ref, q_ref, knew_ref, ckv_hbm, krt_hbm, o_ref,
                  ckv_buf, krt_buf, kbf_sc, s_sc, sem, m_sc, l_sc, acc_sc,
                  *, n_b, n_ch, cp, page, tp, n_new):
    total = n_b * n_ch

    def copies(bc):
        slot = bc % N_RAW
        b = bc // n_ch
        c = bc % n_ch
        out = []
        for p in range(cp):
            pg = pt_ref[b, c * cp + p]
            out.append(pltpu.make_async_copy(ckv_hbm.at[pg], ckv_buf.at[slot, pl.ds(p * page, page)],
                                             sem.at[0, slot]))
            out.append(pltpu.make_async_copy(krt_hbm.at[pg], krt_buf.at[slot, p], sem.at[1, slot]))
        return out

    def score(bc, w):
        slot = bc % N_RAW
        b = bc // n_ch
        for cpy in copies(bc):
            cpy.wait()
        q_lat = q_ref[b, :, 0:KV_RANK]
        q_rope = q_ref[b, :, KV_RANK:KV_RANK + QK_ROPE]
        kc = ckv_buf[slot].astype(BF16)
        krt = jnp.concatenate([krt_buf[slot, p] for p in range(cp)], axis=1).astype(BF16)
        kbf_sc[w] = kc
        s_sc[w] = _dot_nt(q_lat, kc) + _dot(q_rope, krt)

    def update(s, v_bf16, first):
        m_prev = jnp.where(first, NEG, m_sc[...])
        l_prev = jnp.where(first, 0.0, l_sc[...])
        acc_prev = jnp.where(first, 0.0, acc_sc[...])
        m_new = jnp.maximum(m_prev, jnp.max(s, axis=-1, keepdims=True))
        alpha = jnp.exp2(m_prev - m_new)
        p = jnp.exp2(s - _rep(m_new, s.shape[1] // LANES))
        l_sc[...] = alpha * l_prev + jnp.sum(p, axis=-1, keepdims=True)
        acc_sc[...] = _rep(alpha, KV_RANK // LANES) * acc_prev + _dot(p.astype(BF16), v_bf16)
        m_sc[...] = m_new

    def attend(bc, w):
        b = bc // n_ch
        c = bc % n_ch
        update(s_sc[w], kbf_sc[w], c == 0)

        def finish():
            kn = knew_ref[b]
            sn = _dot_nt(q_ref[b], kn)
            tpos = lax.broadcasted_iota(jnp.int32, sn.shape, 0) & (tp - 1)
            col = lax.broadcasted_iota(jnp.int32, sn.shape, 1)
            sn = jnp.where((col <= tpos) & (col < n_new), sn, NEG)
            update(sn, kn[:, 0:KV_RANK], False)
            o_ref[b] = acc_sc[...] * _rep(1.0 / l_sc[...], KV_RANK // LANES)

        last = c == n_ch - 1
        if isinstance(last, bool):
            if last:
                finish()
        else:
            pl.when(last)(finish)

    m_sc[...] = jnp.full(m_sc.shape, NEG, F32)
    l_sc[...] = jnp.zeros(l_sc.shape, F32)
    acc_sc[...] = jnp.zeros(acc_sc.shape, F32)
    for bc in range(min(N_RAW, total)):
        for cpy in copies(bc):
            cpy.start()
    score(0, 0)

    def step(i, w):
        def prefetch():
            for cpy in copies(i + N_RAW):
                cpy.start()

        more = i + N_RAW < total
        if isinstance(more, bool):
            if more:
                prefetch()
        else:
            pl.when(more)(prefetch)
        if isinstance(i, int) and i + 1 >= total:
            attend(i, w)
            return
        score(i + 1, 1 - w)
        attend(i, w)

    def body(k, carry):
        step(2 * k, 0)
        step(2 * k + 1, 1)
        return carry

    n_pairs = (total - 2) // 2
    lax.fori_loop(0, n_pairs, body, 0)
    for i in range(2 * n_pairs, total):
        step(i, i % 2)


def _paged_attention(page_table, q, knew, cache_ckv, cache_krt, *, cp, tp, n_new):
    n_b, n_pages = page_table.shape
    page = cache_ckv.shape[1]
    rows = q.shape[1]
    n_ch = n_pages // cp
    kern = functools.partial(_paged_kernel, n_b=n_b, n_ch=n_ch, cp=cp, page=page, tp=tp, n_new=n_new)
    grid_spec = pltpu.PrefetchScalarGridSpec(
        num_scalar_prefetch=1,
        grid=(1,),
        in_specs=[
            pl.BlockSpec(q.shape, lambda i, pt: (0, 0, 0)),
            pl.BlockSpec(knew.shape, lambda i, pt: (0, 0, 0)),
            pl.BlockSpec(memory_space=pl.ANY),
            pl.BlockSpec(memory_space=pl.ANY),
        ],
        out_specs=pl.BlockSpec((n_b, rows, KV_RANK), lambda i, pt: (0, 0, 0)),
        scratch_shapes=[
            pltpu.VMEM((N_RAW, cp * page, KV_RANK), F32),
            pltpu.VMEM((N_RAW, cp, QK_ROPE, page), F32),
            pltpu.VMEM((2, cp * page, KV_RANK), BF16),
            pltpu.VMEM((2, rows, cp * page), F32),
            pltpu.SemaphoreType.DMA((2, N_RAW)),
            pltpu.VMEM((rows, LANES), F32),
            pltpu.VMEM((rows, LANES), F32),
            pltpu.VMEM((rows, KV_RANK), F32),
        ],
    )
    return pl.pallas_call(
        kern,
        grid_spec=grid_spec,
        out_shape=jax.ShapeDtypeStruct((n_b, rows, KV_RANK), F32),
        compiler_params=_cparams(1),
        name="paged_attention",
    )(page_table, q, knew, cache_ckv, cache_krt)


def _v_up_kernel(o_ref, wv_ref, tok_ref):
    n_b, _, tp, _ = o_ref.shape
    for hd in range(MLA_HEADS):
        oh = o_ref[:, hd].reshape(n_b * tp, KV_RANK).astype(BF16)
        tok_ref[:, :, hd * V_HEAD:(hd + 1) * V_HEAD] = _dot(oh, wv_ref[hd]).reshape(n_b, tp, V_HEAD)


def _v_up(o, w_vup):
    n_b, H, tp, _ = o.shape
    return pl.pallas_call(
        _v_up_kernel,
        out_shape=jax.ShapeDtypeStruct((n_b, tp, MLA_WIDTH), F32),
        compiler_params=pltpu.CompilerParams(vmem_limit_bytes=VMEM_LIMIT),
        name="v_up_sample",
    )(o, w_vup)


def _b_post_kernel(x_ref, tok_ref, z_ref, mk_ref, mv_ref, wout_ref, gf_ref, xo_ref, mix_sc, *, final):
    _mix_out(x_ref, tok_ref, z_ref, mk_ref, mv_ref, wout_ref, gf_ref, xo_ref, mix_sc, final)


def _mix_out(x_ref, tok_ref, z_ref, mk_ref, mv_ref, wout_ref, gf_ref, xo_ref, mix_sc, final):
    nb, tq, _ = x_ref.shape
    m = nb * tq
    for g in range(4):
        c0, c1 = g * 256, (g + 1) * 256
        mix_sc[:, c0:c1] = (tok_ref[:, :, c0:c1] * _silu(z_ref[:, :, c0:c1])).reshape(m, 256).astype(BF16)
    g0 = MLA_WIDTH + MEM_WIDTH
    mem_o = _mem_attend(z_ref[:, :, MLA_WIDTH:g0].reshape(m, MEM_WIDTH), mk_ref, mv_ref, nb, tq)
    for hd in range(MEM_HEADS):
        c0, c1 = hd * MEM_HEAD_DIM, (hd + 1) * MEM_HEAD_DIM
        gate = z_ref[:, :, g0 + c0:g0 + c1].reshape(m, MEM_HEAD_DIM)
        mix_sc[:, POOL_WIDTH + c0:POOL_WIDTH + c1] =(mem_o[hd] * _silu(gate)).astype(BF16)
    acc = x_ref[...].reshape(m, D_MODEL) + _dot(mix_sc[...], wout_ref[...])
    if final:
        acc = _rms(acc, gf_ref[...])
    xo_ref[...] = acc.reshape(nb, tq, D_MODEL)


def _b_post(x, tok, zrest, mem_k, mem_v, w_out, g_final, *, layer, nb, tq, final, tag):
    B, T, _ = x.shape
    kern = functools.partial(_b_post_kernel, final=final)
    return pl.pallas_call(
        kern,
        grid=(B // nb, T // tq),
        in_specs=[
            pl.BlockSpec((nb, tq, D_MODEL), lambda b, t: (b, t, 0)),
            pl.BlockSpec((nb, tq, MLA_WIDTH), lambda b, t: (b, t, 0)),
            pl.BlockSpec((nb, tq, Z_REST), lambda b, t: (b, t, 0)),
            pl.BlockSpec((1, nb, MEM_ROWS, MEM_HEAD_DIM), lambda b, t: (layer, b, 0, 0)),
            pl.BlockSpec((1, nb, MEM_ROWS, MEM_HEAD_DIM), lambda b, t: (layer, b, 0, 0)),
            pl.BlockSpec((None, OUT_W, D_MODEL), lambda b, t: (layer, 0, 0)),
            pl.BlockSpec((1, D_MODEL), lambda b, t: (0, 0)),
        ],
        out_specs=pl.BlockSpec((nb, tq, D_MODEL), lambda b, t: (b, t, 0)),
        out_shape=jax.ShapeDtypeStruct((B, T, D_MODEL), F32),
        scratch_shapes=[pltpu.VMEM((nb * tq, OUT_W), BF16)],
        compiler_params=_cparams(2),
        name=f"b_post{layer}_{tag}",
    )(x, tok, zrest, mem_k, mem_v, w_out, g_final)


def _rope_tables(pos):
    half = QK_ROPE // 2
    inv = ROPE_THETA ** (-jnp.arange(half, dtype=F32) / half)
    ang = pos.astype(F32)[:, None] * inv[None, :]
    return jnp.tile(jnp.cos(ang), (1, LANES // half)), jnp.tile(jnp.sin(ang), (1, LANES // half))


def kernel(x_prompt, x_sample, state_pool, cache_ckv, cache_krope, cache_mem_k, cache_mem_v, page_table, mem_prompt, g_norm, w_in_a, w_pool_grp, pool_scale, w_in_b, g_q_latent, w_q_up, g_kv_in, w_kv_down, g_kv_latent, w_k_up, w_v_up, g_mem, w_mem_k, w_mem_v, w_out, g_final):
    B, T, _ = x_prompt.shape
    SB, ST, _ = x_sample.shape
    TP = SUBLANES
    past = page_table.shape[1] * cache_ckv.shape[1]

    w_in_a16 = w_in_a.astype(BF16)
    w_grp16 = w_pool_grp.astype(BF16)
    w_out16 = w_out.astype(BF16)
    w_cq16 = w_in_b[:, :, :Q_RANK].astype(BF16)
    w_rest16 = w_in_b[:, :, Q_RANK:].astype(BF16)
    wq = w_q_up.reshape(N_A, Q_RANK, MLA_HEADS, QK_NOPE + QK_ROPE)
    w_qn16 = wq[..., :QK_NOPE].reshape(N_A, Q_RANK, MLA_HEADS * QK_NOPE).astype(BF16)
    w_qr16 = wq[..., QK_NOPE:].reshape(N_A, Q_RANK, MLA_HEADS * QK_ROPE).astype(BF16)
    w_kup16 = jnp.transpose(w_k_up, (1, 2, 0)).astype(BF16)
    w_vup16 = jnp.transpose(w_v_up, (1, 0, 2)).astype(BF16)
    w_kv16 = jnp.pad(w_kv_down, ((0, 0), (0, KCAT - KV_RANK - QK_ROPE))).astype(BF16)
    w_mk16 = w_mem_k.astype(BF16)
    w_mv16 = w_mem_v.astype(BF16)
    g_kv_in2 = g_kv_in[None]
    g_lat2 = g_kv_latent[None]
    g_final2 = g_final[None]

    w_kupt16 = w_kup16.reshape(MLA_HEADS * QK_NOPE, KV_RANK)
    w_vall16 = w_v_up.reshape(KV_RANK, MLA_WIDTH).astype(BF16)

    def trunk(x, prev, mem_k, mem_v, pos, *, tag, nb, tq_a, tq, tq_b, heads, n_valid, attend):
        pools = []
        for l in range(N_A):
            x, pool = _a_layer(x, prev[l], g_norm[l][None], w_in_a16, w_grp16, pool_scale[l][None],
                               mem_k, mem_v, w_out16, layer=l, nb=nb, tq=tq_a, n_valid=n_valid, pos0=pos)
            pools.append(pool[:, 1:])
        cos, sin = _rope_tables(pos + jnp.arange(x.shape[1], dtype=jnp.int32))
        ckv, krope, *keys = _kv_latent(x, g_kv_in2, w_kv16, g_lat2, cos, sin, w_kupt16, w_vall16,
                                       nb=nb, tq=tq, tk=min(tq, ATT_TILE), heads=heads)
        for j in range(DEPTH - N_A):
            l = N_A + j
            q, zrest = _b_pre(x, g_norm[l][None], w_cq16, w_rest16, g_q_latent[j][None],
                              w_qn16, w_qr16, w_kup16, cos, sin, j=j, nb=nb, tq=tq_b, absorb=not heads,
                              tag=f"{tag}{l}")
            x = attend(x, q, zrest, *keys, layer=l, final=(l == DEPTH - 1))
        return x, jnp.stack(pools, axis=0), ckv, krope

    mem_k_p, mem_v_p = _mem_project(mem_prompt, g_mem, w_mk16, w_mv16)
    prev_p = jnp.zeros((N_A, B, HALO, D_MODEL), F32)

    def attend_prompt(x, q, zrest, knt, krt, v, *, layer, final):
        return _flash_post(q, knt, krt, v, x, zrest, mem_k_p, mem_v_p, w_out16, g_final2,
                           layer=layer, tq=ATT_TILE, final=final)

    y_p, pool_p, ckv_p, krope_p = trunk(
        x_prompt, prev_p, mem_k_p, mem_v_p, 0, tag="p", nb=1, tq_a=512, tq=512, tq_b=512, heads=True, n_valid=512,
        attend=attend_prompt)

    xs = jnp.pad(x_sample, ((0, 0), (0, TP - ST), (0, 0)))
    prev_s = jnp.pad(state_pool, ((0, 0), (0, 0), (HALO - POOL_BUF, 0), (0, 0)))
    mem_k_s = cache_mem_k.reshape(DEPTH, SB, MEM_ROWS, MEM_HEAD_DIM)
    mem_v_s = cache_mem_v.reshape(DEPTH, SB, MEM_ROWS, MEM_HEAD_DIM)
    cache_krt = jnp.transpose(cache_krope, (0, 2, 1))

    def attend_sample(x, q, zrest, kcat, *, layer, final):
        qf = q.reshape(SB, MLA_HEADS * TP, KCAT)
        knew = jnp.pad(kcat, ((0, 0), (0, LANES - TP), (0, 0)))
        o = _paged_attention(page_table, qf, knew, cache_ckv, cache_krt, cp=16, tp=TP, n_new=ST)
        tok = _v_up(o.reshape(SB, MLA_HEADS, TP, KV_RANK), w_vup16)
        return _b_post(x, tok, zrest, mem_k_s, mem_v_s, w_out16, g_final2, layer=layer, nb=8, tq=TP,
                       final=final, tag="s")

    y_s, pool_s, ckv_s, krope_s = trunk(
        xs, prev_s, mem_k_s, mem_v_s, past, tag="s", nb=8, tq_a=TP, tq=TP, tq_b=TP, heads=False, n_valid=ST,
        attend=attend_sample)

    mem_shape = (DEPTH, B, MEM_TOKENS, MEM_HEADS, MEM_HEAD_DIM)
    return (y_p, y_s[:, :ST], pool_p, pool_s, ckv_p, krope_p, ckv_s[:, :ST], krope_s[:, :ST],
            mem_k_p.reshape(mem_shape), mem_v_p.reshape(mem_shape))
```

```python
import functools

import jax
import jax.numpy as jnp
from jax import lax
from jax.experimental import pallas as pl
from jax.experimental.pallas import tpu as pltpu

F32 = jnp.float32
BF16 = jnp.bfloat16

D_MODEL = 1024
DEPTH = 4
N_A = 2
POOL_WINDOWS = (2, 4, 8, 16)
POOL_GROUP = 256
POOL_BUF = 15
MLA_HEADS = 8
QK_NOPE = 128
QK_ROPE = 64
V_HEAD = 128
KV_RANK = 256
Q_RANK = 384
MLA_WIDTH = MLA_HEADS * V_HEAD
MLA_SCALE = (QK_NOPE + QK_ROPE) ** -0.5
ROPE_THETA = 10000.0
MEM_TOKENS = 256
MEM_HEADS = 4
MEM_HEAD_DIM = 128
MEM_WIDTH = MEM_HEADS * MEM_HEAD_DIM
MEM_SCALE = MEM_HEAD_DIM ** -0.5
POOL_WIDTH = D_MODEL
IN_A = 2 * POOL_WIDTH + 2 * MEM_WIDTH
Z_REST = MLA_WIDTH + 2 * MEM_WIDTH
OUT_W = POOL_WIDTH + MEM_WIDTH
EPS = 1e-6
NEG = -1e30
LOG2E = 1.4426950408889634

KCAT = 384
SUBLANES = 8
LANES = 128
HALO = 16
E_OFF = SUBLANES + HALO
VMEM_LIMIT = 56 * 1024 * 1024
MEM_ROWS = MEM_TOKENS * MEM_HEADS
ATT_TILE = 256
N_RAW = 4


def _cparams(n_axes):
    return pltpu.CompilerParams(dimension_semantics=("arbitrary",) * n_axes,
                                vmem_limit_bytes=VMEM_LIMIT)


def _rms(x, g):
    return x * lax.rsqrt(jnp.mean(x * x, axis=-1, keepdims=True) + EPS) * g


def _dot(a, b):
    return jnp.dot(a, b, preferred_element_type=F32)


def _dot_nt(a, b):
    return lax.dot_general(a, b, (((1,), (1,)), ((), ())), preferred_element_type=F32)


def _silu(x):
    return x * jax.nn.sigmoid(x)


def _rope_lanes(a, cos, sin):
    lane = lax.broadcasted_iota(jnp.int32, a.shape, 1)
    up = pltpu.roll(a, 32, 1)
    dn = pltpu.roll(a, 96, 1)
    rot = jnp.where((lane & 63) < 32, -dn, up)
    return a * cos + rot * sin


def _rep(x, n):
    return x if n == 1 else jnp.concatenate([x] * n, axis=1)


def _tile_rows(x, n):
    return x if n == 1 else jnp.concatenate([x] * n, axis=0)


def _mem_attend(qm, mk_ref, mv_ref, nb, tq):
    rows, keys = nb * tq, nb * MEM_TOKENS
    if nb > 1:
        row_seq = lax.broadcasted_iota(jnp.int32, (rows, keys), 0) // tq
        col_seq = lax.broadcasted_iota(jnp.int32, (rows, keys), 1) // MEM_TOKENS
        keep = row_seq == col_seq
    outs = []
    for h in range(MEM_HEADS):
        c0, c1 = h * MEM_HEAD_DIM, (h + 1) * MEM_HEAD_DIM
        q = (qm[:, c0:c1] * (MEM_SCALE * LOG2E)).astype(BF16)
        k = mk_ref[0, :, pl.ds(h, MEM_TOKENS, stride=MEM_HEADS), :].reshape(keys, MEM_HEAD_DIM).astype(BF16)
        v = mv_ref[0, :, pl.ds(h, MEM_TOKENS, stride=MEM_HEADS), :].reshape(keys, MEM_HEAD_DIM).astype(BF16)
        s = _dot_nt(q, k)
        if nb > 1:
            s = jnp.where(keep, s, NEG)
        m = jnp.max(s, axis=-1, keepdims=True)
        p = jnp.exp2(s - m)
        l = jnp.sum(p, axis=-1, keepdims=True)
        outs.append(_dot(p.astype(BF16), v) * (1.0 / l))
    return outs


def _mem_proj_kernel(mem_ref, g_ref, wk_ref, wv_ref, mk_ref, mv_ref):
    x = mem_ref[0]
    xn = x * lax.rsqrt(jnp.mean(x * x, axis=-1, keepdims=True) + EPS)
    for l in range(DEPTH):
        h = (xn * g_ref[l:l + 1, :]).astype(BF16)
        mk = _dot(h, wk_ref[l])
        mv = _dot(h, wv_ref[l])
        for hd in range(MEM_HEADS):
            c0, c1 = hd * MEM_HEAD_DIM, (hd + 1) * MEM_HEAD_DIM
            mk_ref[l, 0, pl.ds(hd, MEM_TOKENS, stride=MEM_HEADS), :] = mk[:, c0:c1]
            mv_ref[l, 0, pl.ds(hd, MEM_TOKENS, stride=MEM_HEADS), :] = mv[:, c0:c1]


def _mem_project(mem, g_mem, wk, wv):
    B, M, _ = mem.shape
    out = jax.ShapeDtypeStruct((DEPTH, B, MEM_ROWS, MEM_HEAD_DIM), F32)
    return pl.pallas_call(
        _mem_proj_kernel,
        grid=(B,),
        in_specs=[
            pl.BlockSpec((1, M, D_MODEL), lambda b: (b, 0, 0)),
            pl.BlockSpec((DEPTH, D_MODEL), lambda b: (0, 0)),
            pl.BlockSpec((DEPTH, D_MODEL, MEM_WIDTH), lambda b: (0, 0, 0)),
            pl.BlockSpec((DEPTH, D_MODEL, MEM_WIDTH), lambda b: (0, 0, 0)),
        ],
        out_specs=[
            pl.BlockSpec((DEPTH, 1, MEM_ROWS, MEM_HEAD_DIM), lambda b: (0, b, 0, 0)),
            pl.BlockSpec((DEPTH, 1, MEM_ROWS, MEM_HEAD_DIM), lambda b: (0, b, 0, 0)),
        ],
        out_shape=[out, out],
        compiler_params=_cparams(1),
        name="mem_project",
    )(mem, g_mem, wk, wv)


def _a_layer_kernel(x_ref, prev_ref, g_ref, win_ref, wgrp_ref, pscale_ref, mk_ref, mv_ref, wout_ref,
                    xo_ref, pool_ref, e_sc, sa_sc, sb_sc, mix_sc, *, nb, tq, n_t, n_valid, pos0):
    t = pl.program_id(1)
    rows = HALO + tq
    m = nb * tq

    @pl.when(t == 0)
    def _():
        zeros = jnp.zeros((nb, SUBLANES, D_MODEL), F32)
        e_sc[:, 0:SUBLANES, :] = zeros
        sa_sc[:, 0:SUBLANES, :] = zeros
        sb_sc[:, 0:SUBLANES, :] = zeros
        e_sc[:, SUBLANES:E_OFF, :] = prev_ref[...]

    x = x_ref[...].reshape(m, D_MODEL)
    h = _rms(x, g_ref[...]).astype(BF16)
    e_sc[:, E_OFF:E_OFF + tq, :] = _dot(h, win_ref[:, 0:POOL_WIDTH]).reshape(nb, tq, POOL_WIDTH)

    lo = SUBLANES
    sa_sc[:, lo:lo + rows, :] = e_sc[:, lo:lo + rows, :] + e_sc[:, lo - 1:lo - 1 + rows, :]
    sb_sc[:, lo:lo + rows, 256:1024] = (sa_sc[:, lo:lo + rows, 256:1024]
                                        + sa_sc[:, lo - 2:lo - 2 + rows, 256:1024])
    sa_sc[:, lo:lo + rows, 512:1024] = (sb_sc[:, lo:lo + rows, 512:1024]
                                        + sb_sc[:, lo - 4:lo - 4 + rows, 512:1024])
    s16 = sa_sc[:, E_OFF:E_OFF + tq, 768:1024] + sa_sc[:, E_OFF - 8:E_OFF - 8 + tq, 768:1024]
    sums = (sa_sc[:, E_OFF:E_OFF + tq, 0:256], sb_sc[:, E_OFF:E_OFF + tq, 256:512],
            sa_sc[:, E_OFF:E_OFF + tq, 512:768], s16)

    pos = pos0 + t * tq + lax.broadcasted_iota(jnp.int32, (1, tq, 1), 1)
    for g, w in enumerate(POOL_WINDOWS):
        c0, c1 = g * POOL_GROUP, (g + 1) * POOL_GROUP
        inv_cnt = 1.0 / jnp.minimum(pos + 1, w).astype(F32)
        pooled = (sums[g] * inv_cnt - e_sc[:, E_OFF:E_OFF + tq, c0:c1]).reshape(m, POOL_GROUP)
        mixed = _dot(pooled.astype(BF16), wgrp_ref[g]) * pscale_ref[:, c0:c1]
        gate = _dot(h, win_ref[:, POOL_WIDTH + c0:POOL_WIDTH + c1])
        mix_sc[:, c0:c1] = (mixed * _silu(gate)).astype(BF16)

    qm = _dot(h, win_ref[:, 2 * POOL_WIDTH:2 * POOL_WIDTH + MEM_WIDTH])
    gate_m = _dot(h, win_ref[:, 2 * POOL_WIDTH + MEM_WIDTH:IN_A])
    mem_o = _mem_attend(qm, mk_ref, mv_ref, nb, tq)
    for hd in range(MEM_HEADS):
        c0, c1 = hd * MEM_HEAD_DIM, (hd + 1) * MEM_HEAD_DIM
        mix_sc[:, POOL_WIDTH + c0:POOL_WIDTH + c1] =(mem_o[hd] * _silu(gate_m[:, c0:c1])).astype(BF16)
    xo_ref[...] = (x + _dot(mix_sc[...], wout_ref[...])).reshape(nb, tq, D_MODEL)

    @pl.when(t == n_t - 1)
    def _():
        pool_ref[...] = e_sc[:, E_OFF + n_valid - HALO:E_OFF + n_valid, :]

    if n_t > 1:
        @pl.when(t < n_t - 1)
        def _():
            e_sc[:, SUBLANES:E_OFF, :] = e_sc[:, SUBLANES + tq:E_OFF + tq, :]


def _a_layer(x, prev, g, w_in, w_grp, pscale, mem_k, mem_v, w_out, *, layer, nb, tq, n_valid, pos0):
    B, T, _ = x.shape
    n_t = T // tq
    rows = E_OFF + tq
    kern = functools.partial(_a_layer_kernel, nb=nb, tq=tq, n_t=n_t, n_valid=n_valid, pos0=pos0)
    return pl.pallas_call(
        kern,
        grid=(B // nb, n_t),
        in_specs=[
            pl.BlockSpec((nb, tq, D_MODEL), lambda b, t: (b, t, 0)),
            pl.BlockSpec((nb, HALO, D_MODEL), lambda b, t: (b, 0, 0)),
            pl.BlockSpec((1, D_MODEL), lambda b, t: (0, 0)),
            pl.BlockSpec((None, D_MODEL, IN_A), lambda b, t: (layer, 0, 0)),
            pl.BlockSpec((None, 4, POOL_GROUP, POOL_GROUP), lambda b, t: (layer, 0, 0, 0)),
            pl.BlockSpec((1, D_MODEL), lambda b, t: (0, 0)),
            pl.BlockSpec((1, nb, MEM_ROWS, MEM_HEAD_DIM), lambda b, t: (layer, b, 0, 0)),
            pl.BlockSpec((1, nb, MEM_ROWS, MEM_HEAD_DIM), lambda b, t: (layer, b, 0, 0)),
            pl.BlockSpec((None, OUT_W, D_MODEL), lambda b, t: (layer, 0, 0)),
        ],
        out_specs=[
            pl.BlockSpec((nb, tq, D_MODEL), lambda b, t: (b, t, 0)),
            pl.BlockSpec((nb, HALO, D_MODEL), lambda b, t: (b, 0, 0)),
        ],
        out_shape=[jax.ShapeDtypeStruct((B, T, D_MODEL), F32),
                   jax.ShapeDtypeStruct((B, HALO, D_MODEL), F32)],
        scratch_shapes=[pltpu.VMEM((nb, rows, D_MODEL), F32)] * 3 + [pltpu.VMEM((nb * tq, OUT_W), BF16)],
        compiler_params=_cparams(2),
        name=f"a_layer{layer}_{B}",
    )(x, prev, g, w_in, w_grp, pscale, mem_k, mem_v, w_out)


def _kv_latent_kernel(x_ref, gin_ref, wkv_ref, glat_ref, cos_ref, sin_ref, wkupt_ref, wvup_ref,
                      ckv_ref, kr_ref, *out_refs, heads):
    nb, tq, _ = x_ref.shape
    m = nb * tq
    h = _rms(x_ref[...].reshape(m, D_MODEL), gin_ref[...]).astype(BF16)
    kv = _dot(h, wkv_ref[...])
    ckv = _rms(kv[:, 0:KV_RANK], glat_ref[...])
    kr = _rope_lanes(kv[:, KV_RANK:KCAT], _tile_rows(cos_ref[...], nb), _tile_rows(sin_ref[...], nb))
    ckv_ref[...] = ckv.reshape(nb, tq, KV_RANK)
    if not heads:
        kcat_ref, = out_refs
        kr_ref[...] = kr[:, 0:QK_ROPE].reshape(nb, tq, QK_ROPE)
        kcat_ref[:, :, 0:KV_RANK] = ckv.reshape(nb, tq, KV_RANK).astype(BF16)
        kcat_ref[:, :, KV_RANK:KCAT] = kr.reshape(nb, tq, LANES).astype(BF16)
    else:
        knt_ref, krt_ref, v_ref = out_refs
        tk = knt_ref.shape[-1]
        v_ref[0] = _dot(ckv.astype(BF16), wvup_ref[...]).astype(BF16)
        for sub in range(tq // tk):
            r0 = sub * tk
            ckv_t = ckv[r0:r0 + tk, :].T.astype(BF16)
            knt_ref[0, sub] = _dot(wkupt_ref[...], ckv_t).astype(BF16)
            kr_t = kr[r0:r0 + tk, :].T
            kr_ref[0, :, r0:r0 + tk] = kr_t[0:QK_ROPE, :]
            krt_ref[0, sub] = kr_t.astype(BF16)


def _kv_latent(x, g_in, w_kv, g_lat, cos, sin, w_kupt, w_vup, *, nb, tq, tk, heads):
    B, T, _ = x.shape
    assert nb == 1 or not heads
    n_sub = tq // tk
    out_specs = [
        pl.BlockSpec((nb, tq, KV_RANK), lambda b, t: (b, t, 0)),
        pl.BlockSpec((1, QK_ROPE, tq), lambda b, t: (b, 0, t)) if heads
        else pl.BlockSpec((nb, tq, QK_ROPE), lambda b, t: (b, t, 0)),
    ]
    out_shape = [jax.ShapeDtypeStruct((B, T, KV_RANK), F32),
                 jax.ShapeDtypeStruct((B, QK_ROPE, T) if heads else (B, T, QK_ROPE), F32)]
    if heads:
        out_specs += [pl.BlockSpec((1, n_sub, MLA_HEADS * QK_NOPE, tk), lambda b, t: (b, t, 0, 0)),
                      pl.BlockSpec((1, n_sub, LANES, tk), lambda b, t: (b, t, 0, 0)),
                      pl.BlockSpec((1, tq, MLA_WIDTH), lambda b, t: (b, t, 0))]
        out_shape += [jax.ShapeDtypeStruct((B, T // tk, MLA_HEADS * QK_NOPE, tk), BF16),
                      jax.ShapeDtypeStruct((B, T // tk, LANES, tk), BF16),
                      jax.ShapeDtypeStruct((B, T, MLA_WIDTH), BF16)]
    else:
        out_specs.append(pl.BlockSpec((nb, tq, KCAT), lambda b, t: (b, t, 0)))
        out_shape.append(jax.ShapeDtypeStruct((B, T, KCAT), BF16))
    return pl.pallas_call(
        functools.partial(_kv_latent_kernel, heads=heads),
        grid=(B // nb, T // tq),
        in_specs=[
            pl.BlockSpec((nb, tq, D_MODEL), lambda b, t: (b, t, 0)),
            pl.BlockSpec((1, D_MODEL), lambda b, t: (0, 0)),
            pl.BlockSpec((D_MODEL, KCAT), lambda b, t: (0, 0)),
            pl.BlockSpec((1, KV_RANK), lambda b, t: (0, 0)),
            pl.BlockSpec((tq, LANES), lambda b, t: (t, 0)),
            pl.BlockSpec((tq, LANES), lambda b, t: (t, 0)),
            pl.BlockSpec((MLA_HEADS * QK_NOPE, KV_RANK), lambda b, t: (0, 0)),
            pl.BlockSpec((KV_RANK, MLA_WIDTH), lambda b, t: (0, 0)),
        ],
        out_specs=out_specs,
        out_shape=out_shape,
        compiler_params=_cparams(2),
        name=f"kv_latent_{B}",
    )(x, g_in, w_kv, g_lat, cos, sin, w_kupt, w_vup)


def _b_pre_kernel(x_ref, g_ref, winb_ref, gq_ref, wqn_ref, wqr_ref, wkup_ref, cos_ref, sin_ref,
                  q_ref, zrest_ref, *, absorb):
    nb, tq, _ = x_ref.shape
    m = nb * tq
    n0 = KV_RANK if absorb else QK_NOPE
    h = _rms(x_ref[...].reshape(m, D_MODEL), g_ref[...]).astype(BF16)
    zrest_ref[...] = _dot(h, winb_ref[:, Q_RANK:]).reshape(nb, tq, Z_REST)
    cn = _rms(_dot(h, winb_ref[:, 0:Q_RANK]), gq_ref[...]).astype(BF16)
    qn = _dot(cn, wqn_ref[...])
    qr = _dot(cn, wqr_ref[...])
    for hd in range(MLA_HEADS):
        ql = qn[:, hd * QK_NOPE:(hd + 1) * QK_NOPE]
        if absorb:
            ql = _dot(ql.astype(BF16), wkup_ref[hd])
        q_ref[:, hd, :, 0:n0] = (ql * (MLA_SCALE * LOG2E)).reshape(nb, tq, n0).astype(BF16)
    cos = _tile_rows(cos_ref[...], nb)
    sin = _tile_rows(sin_ref[...], nb)
    lane = lax.broadcasted_iota(jnp.int32, cos.shape, 1)
    for j in range(MLA_HEADS // 2):
        rr = _rope_lanes(qr[:, j * LANES:(j + 1) * LANES], cos, sin) * (MLA_SCALE * LOG2E)
        even = jnp.where(lane < QK_ROPE, rr, 0.0)
        odd = jnp.where(lane < QK_ROPE, pltpu.roll(rr, 64, 1), 0.0)
        q_ref[:, 2 * j, :, n0:n0 + LANES] = even.reshape(nb, tq, LANES).astype(BF16)
        q_ref[:, 2 * j + 1, :, n0:n0 + LANES] = odd.reshape(nb, tq, LANES).astype(BF16)


def _b_pre(x, g, w_inb, g_q, w_qn, w_qr, w_kup, cos, sin, *, j, nb, tq, absorb, tag):
    B, T, _ = x.shape
    const2 = lambda b, t: (0, 0)
    qw = (KV_RANK if absorb else QK_NOPE) + LANES
    return pl.pallas_call(
        functools.partial(_b_pre_kernel, absorb=absorb),
        grid=(B // nb, T // tq),
        in_specs=[
            pl.BlockSpec((nb, tq, D_MODEL), lambda b, t: (b, t, 0)),
            pl.BlockSpec((1, D_MODEL), const2),
            pl.BlockSpec((None, D_MODEL, Q_RANK + Z_REST), lambda b, t: (j, 0, 0)),
            pl.BlockSpec((1, Q_RANK), const2),
            pl.BlockSpec((None, Q_RANK, MLA_HEADS * QK_NOPE), lambda b, t: (j, 0, 0)),
            pl.BlockSpec((None, Q_RANK, MLA_HEADS * QK_ROPE), lambda b, t: (j, 0, 0)),
            pl.BlockSpec((MLA_HEADS, QK_NOPE, KV_RANK), lambda b, t: (0, 0, 0)),
            pl.BlockSpec((tq, LANES), lambda b, t: (t, 0)),
            pl.BlockSpec((tq, LANES), lambda b, t: (t, 0)),
        ],
        out_specs=[
            pl.BlockSpec((nb, MLA_HEADS, tq, qw), lambda b, t: (b, 0, t, 0)),
            pl.BlockSpec((nb, tq, Z_REST), lambda b, t: (b, t, 0)),
        ],
        out_shape=[jax.ShapeDtypeStruct((B, MLA_HEADS, T, qw), BF16),
                   jax.ShapeDtypeStruct((B, T, Z_REST), F32)],
        compiler_params=_cparams(2),
        name=f"b_pre_{tag}",
    )(x, g, w_inb, g_q, w_qn, w_qr, w_kup, cos, sin)


def _flash_kernel(q_ref, knt_ref, krt_ref, v_ref, x_ref, z_ref, mk_ref, mv_ref, wout_ref, gf_ref, xo_ref,
                  m_sc, l_sc, acc_sc, tok_sc, mix_sc, *, tq, tk, final):
    i = pl.program_id(1)
    m_sc[...] = jnp.full(m_sc.shape, NEG, F32)
    l_sc[...] = jnp.zeros(l_sc.shape, F32)
    acc_sc[...] = jnp.zeros(acc_sc.shape, F32)

    def block(j, masked):
        krt = krt_ref[0, j]
        rows = pl.ds(pl.multiple_of(j * tk, tk), tk)
        if masked:
            keep = (lax.broadcasted_iota(jnp.int32, (tq, tk), 1)
                    <= lax.broadcasted_iota(jnp.int32, (tq, tk), 0))

        def scores(hd):
            kt = jnp.concatenate([knt_ref[0, j, hd * QK_NOPE:(hd + 1) * QK_NOPE, :], krt], axis=0)
            return _dot(q_ref[0, hd], kt)

        s_next = scores(0)
        for hd in range(MLA_HEADS):
            s = s_next
            if hd + 1 < MLA_HEADS:
                s_next = scores(hd + 1)
            if masked:
                s = jnp.where(keep, s, NEG)
            m_prev = m_sc[hd]
            m_new = jnp.maximum(m_prev, jnp.max(s, axis=-1, keepdims=True))
            alpha = jnp.exp2(m_prev - m_new)
            p = jnp.exp2(s - _rep(m_new, tk // LANES))
            p_lanes = p[:, 0:LANES]
            for c in range(1, tk // LANES):
                p_lanes = p_lanes + p[:, c * LANES:(c + 1) * LANES]
            l_sc[hd] = alpha * l_sc[hd] + p_lanes
            v = v_ref[0, rows, hd * V_HEAD:(hd + 1) * V_HEAD]
            acc_sc[hd] = alpha * acc_sc[hd] + _dot(p.astype(BF16), v)
            m_sc[hd] = m_new

    def body(j, carry):
        block(j, False)
        return carry

    lax.fori_loop(0, i, body, 0)
    block(i, True)

    for hd in range(MLA_HEADS):
        l = jnp.sum(l_sc[hd], axis=-1, keepdims=True)
        tok_sc[0, :, hd * V_HEAD:(hd + 1) * V_HEAD] = acc_sc[hd] * (1.0 / l)

    _mix_out(x_ref, tok_sc, z_ref, mk_ref, mv_ref, wout_ref, gf_ref, xo_ref, mix_sc, final)


def _flash_post(q, knt, krt, v, x, zrest, mem_k, mem_v, w_out, g_final, *, layer, tq, final):
    B, H, T, qw = q.shape
    tk = knt.shape[-1]
    assert tq == tk and qw == QK_NOPE + LANES
    kern = functools.partial(_flash_kernel, tq=tq, tk=tk, final=final)
    return pl.pallas_call(
        kern,
        grid=(B, T // tq),
        in_specs=[
            pl.BlockSpec((1, H, tq, qw), lambda b, i: (b, 0, i, 0)),
            pl.BlockSpec((1, T // tk, H * QK_NOPE, tk), lambda b, i: (b, 0, 0, 0)),
            pl.BlockSpec((1, T // tk, LANES, tk), lambda b, i: (b, 0, 0, 0)),
            pl.BlockSpec((1, T, MLA_WIDTH), lambda b, i: (b, 0, 0)),
            pl.BlockSpec((1, tq, D_MODEL), lambda b, i: (b, i, 0)),
            pl.BlockSpec((1, tq, Z_REST), lambda b, i: (b, i, 0)),
            pl.BlockSpec((1, 1, MEM_ROWS, MEM_HEAD_DIM), lambda b, i: (layer, b, 0, 0)),
            pl.BlockSpec((1, 1, MEM_ROWS, MEM_HEAD_DIM), lambda b, i: (layer, b, 0, 0)),
            pl.BlockSpec((None, OUT_W, D_MODEL), lambda b, i: (layer, 0, 0)),
            pl.BlockSpec((1, D_MODEL), lambda b, i: (0, 0)),
        ],
        out_specs=pl.BlockSpec((1, tq, D_MODEL), lambda b, i: (b, i, 0)),
        out_shape=jax.ShapeDtypeStruct((B, T, D_MODEL), F32),
        scratch_shapes=[pltpu.VMEM((H, tq, LANES), F32), pltpu.VMEM((H, tq, LANES), F32),
                        pltpu.VMEM((H, tq, V_HEAD), F32), pltpu.VMEM((1, tq, MLA_WIDTH), F32),
                        pltpu.VMEM((tq, OUT_W), BF16)],
        compiler_params=_cparams(2),
        name=f"flash_post{layer}",
    )(q, knt, krt, v, x, zrest, mem_k, mem_v, w_out, g_final)


def _paged_kernel(pt_ref, q_ref, knew_ref, ckv_hbm, krt_hbm, o_ref,
                  ckv_buf, krt_buf, kbf_sc, s_sc, sem, m_sc, l_sc, acc_sc,
                  *, n_b, n_ch, cp, page, tp, n_new):
    total = n_b * n_ch

    def copies(bc):
        slot = bc % N_RAW
        b = bc // n_ch
        c = bc % n_ch
        out = []
        for p in range(cp):
            pg = pt_ref[b, c * cp + p]
            out.append(pltpu.make_async_copy(ckv_hbm.at[pg], ckv_buf.at[slot, pl.ds(p * page, page)],
                                             sem.at[0, slot]))
            out.append(pltpu.make_async_copy(krt_hbm.at[pg], krt_buf.at[slot, p], sem.at[1, slot]))
        return out

    def score(bc, w):
        slot = bc % N_RAW
        b = bc // n_ch
        for cpy in copies(bc):
            cpy.wait()
        q_lat = q_ref[b, :, 0:KV_RANK]
        q_rope = q_ref[b, :, KV_RANK:KV_RANK + QK_ROPE]
        kc = ckv_buf[slot].astype(BF16)
        krt = jnp.concatenate([krt_buf[slot, p] for p in range(cp)], axis=1).astype(BF16)
        kbf_sc[w] = kc
        s_sc[w] = _dot_nt(q_lat, kc) + _dot(q_rope, krt)

    def update(s, v_bf16, first):
        m_prev = jnp.where(first, NEG, m_sc[...])
        l_prev = jnp.where(first, 0.0, l_sc[...])
        acc_prev = jnp.where(first, 0.0, acc_sc[...])
        m_new = jnp.maximum(m_prev, jnp.max(s, axis=-1, keepdims=True))
        alpha = jnp.exp2(m_prev - m_new)
        p = jnp.exp2(s - _rep(m_new, s.shape[1] // LANES))
        l_sc[...] = alpha * l_prev + jnp.sum(p, axis=-1, keepdims=True)
        acc_sc[...] = _rep(alpha, KV_RANK // LANES) * acc_prev + _dot(p.astype(BF16), v_bf16)
        m_sc[...] = m_new

    def attend(bc, w):
        b = bc // n_ch
        c = bc % n_ch
        update(s_sc[w], kbf_sc[w], c == 0)

        def finish():
            kn = knew_ref[b]
            sn = _dot_nt(q_ref[b], kn)
            tpos = lax.broadcasted_iota(jnp.int32, sn.shape, 0) & (tp - 1)
            col = lax.broadcasted_iota(jnp.int32, sn.shape, 1)
            sn = jnp.where((col <= tpos) & (col < n_new), sn, NEG)
            update(sn, kn[:, 0:KV_RANK], False)
            o_ref[b] = acc_sc[...] * _rep(1.0 / l_sc[...], KV_RANK // LANES)

        last = c == n_ch - 1
        if isinstance(last, bool):
            if last:
                finish()
        else:
            pl.when(last)(finish)

    m_sc[...] = jnp.full(m_sc.shape, NEG, F32)
    l_sc[...] = jnp.zeros(l_sc.shape, F32)
    acc_sc[...] = jnp.zeros(acc_sc.shape, F32)
    for bc in range(min(N_RAW, total)):
        for cpy in copies(bc):
            cpy.start()
    score(0, 0)

    def step(i, w):
        def prefetch():
            for cpy in copies(i + N_RAW):
                cpy.start()

        more = i + N_RAW < total
        if isinstance(more, bool):
            if more:
                prefetch()
        else:
            pl.when(more)(prefetch)
        if isinstance(i, int) and i + 1 >= total:
            attend(i, w)
            return
        score(i + 1, 1 - w)
        attend(i, w)

    def body(k, carry):
        step(2 * k, 0)
        step(2 * k + 1, 1)
        return carry

    n_pairs = (total - 2) // 2
    lax.fori_loop(0, n_pairs, body, 0)
    for i in range(2 * n_pairs, total):
        step(i, i % 2)


def _paged_attention(page_table, q, knew, cache_ckv, cache_krt, *, cp, tp, n_new):
    n_b, n_pages = page_table.shape
    page = cache_ckv.shape[1]
    rows = q.shape[1]
    n_ch = n_pages // cp
    kern = functools.partial(_paged_kernel, n_b=n_b, n_ch=n_ch, cp=cp, page=page, tp=tp, n_new=n_new)
    grid_spec = pltpu.PrefetchScalarGridSpec(
        num_scalar_prefetch=1,
        grid=(1,),
        in_specs=[
            pl.BlockSpec(q.shape, lambda i, pt: (0, 0, 0)),
            pl.BlockSpec(knew.shape, lambda i, pt: (0, 0, 0)),
            pl.BlockSpec(memory_space=pl.ANY),
            pl.BlockSpec(memory_space=pl.ANY),
        ],
        out_specs=pl.BlockSpec((n_b, rows, KV_RANK), lambda i, pt: (0, 0, 0)),
        scratch_shapes=[
            pltpu.VMEM((N_RAW, cp * page, KV_RANK), F32),
            pltpu.VMEM((N_RAW, cp, QK_ROPE, page), F32),
            pltpu.VMEM((2, cp * page, KV_RANK), BF16),
            pltpu.VMEM((2, rows, cp * page), F32),
            pltpu.SemaphoreType.DMA((2, N_RAW)),
            pltpu.VMEM((rows, LANES), F32),
            pltpu.VMEM((rows, LANES), F32),
            pltpu.VMEM((rows, KV_RANK), F32),
        ],
    )
    return pl.pallas_call(
        kern,
        grid_spec=grid_spec,
        out_shape=jax.ShapeDtypeStruct((n_b, rows, KV_RANK), F32),
        compiler_params=_cparams(1),
        name="paged_attention",
    )(page_table, q, knew, cache_ckv, cache_krt)


def _v_up_kernel(o_ref, wv_ref, tok_ref):
    n_b, _, tp, _ = o_ref.shape
    for hd in range(MLA_HEADS):
        oh = o_ref[:, hd].reshape(n_b * tp, KV_RANK).astype(BF16)
        tok_ref[:, :, hd * V_HEAD:(hd + 1) * V_HEAD] = _dot(oh, wv_ref[hd]).reshape(n_b, tp, V_HEAD)


def _v_up(o, w_vup):
    n_b, H, tp, _ = o.shape
    return pl.pallas_call(
        _v_up_kernel,
        out_shape=jax.ShapeDtypeStruct((n_b, tp, MLA_WIDTH), F32),
        compiler_params=pltpu.CompilerParams(vmem_limit_bytes=VMEM_LIMIT),
        name="v_up_sample",
    )(o, w_vup)


def _b_post_kernel(x_ref, tok_ref, z_ref, mk_ref, mv_ref, wout_ref, gf_ref, xo_ref, mix_sc, *, final):
    _mix_out(x_ref, tok_ref, z_ref, mk_ref, mv_ref, wout_ref, gf_ref, xo_ref, mix_sc, final)


def _mix_out(x_ref, tok_ref, z_ref, mk_ref, mv_ref, wout_ref, gf_ref, xo_ref, mix_sc, final):
    nb, tq, _ = x_ref.shape
    m = nb * tq
    for g in range(4):
        c0, c1 = g * 256, (g + 1) * 256
        mix_sc[:, c0:c1] = (tok_ref[:, :, c0:c1] * _silu(z_ref[:, :, c0:c1])).reshape(m, 256).astype(BF16)
    g0 = MLA_WIDTH + MEM_WIDTH
    mem_o = _mem_attend(z_ref[:, :, MLA_WIDTH:g0].reshape(m, MEM_WIDTH), mk_ref, mv_ref, nb, tq)
    for hd in range(MEM_HEADS):
        c0, c1 = hd * MEM_HEAD_DIM, (hd + 1) * MEM_HEAD_DIM
        gate = z_ref[:, :, g0 + c0:g0 + c1].reshape(m, MEM_HEAD_DIM)
        mix_sc[:, POOL_WIDTH + c0:POOL_WIDTH + c1] =(mem_o[hd] * _silu(gate)).astype(BF16)
    acc = x_ref[...].reshape(m, D_MODEL) + _dot(mix_sc[...], wout_ref[...])
    if final:
        acc = _rms(acc, gf_ref[...])
    xo_ref[...] = acc.reshape(nb, tq, D_MODEL)


def _b_post(x, tok, zrest, mem_k, mem_v, w_out, g_final, *, layer, nb, tq, final, tag):
    B, T, _ = x.shape
    kern = functools.partial(_b_post_kernel, final=final)
    return pl.pallas_call(
        kern,
        grid=(B // nb, T // tq),
        in_specs=[
            pl.BlockSpec((nb, tq, D_MODEL), lambda b, t: (b, t, 0)),
            pl.BlockSpec((nb, tq, MLA_WIDTH), lambda b, t: (b, t, 0)),
            pl.BlockSpec((nb, tq, Z_REST), lambda b, t: (b, t, 0)),
            pl.BlockSpec((1, nb, MEM_ROWS, MEM_HEAD_DIM), lambda b, t: (layer, b, 0, 0)),
            pl.BlockSpec((1, nb, MEM_ROWS, MEM_HEAD_DIM), lambda b, t: (layer, b, 0, 0)),
            pl.BlockSpec((None, OUT_W, D_MODEL), lambda b, t: (layer, 0, 0)),
            pl.BlockSpec((1, D_MODEL), lambda b, t: (0, 0)),
        ],
        out_specs=pl.BlockSpec((nb, tq, D_MODEL), lambda b, t: (b, t, 0)),
        out_shape=jax.ShapeDtypeStruct((B, T, D_MODEL), F32),
        scratch_shapes=[pltpu.VMEM((nb * tq, OUT_W), BF16)],
        compiler_params=_cparams(2),
        name=f"b_post{layer}_{tag}",
    )(x, tok, zrest, mem_k, mem_v, w_out, g_final)


def _rope_tables(pos):
    half = QK_ROPE // 2
    inv = ROPE_THETA ** (-jnp.arange(half, dtype=F32) / half)
    ang = pos.astype(F32)[:, None] * inv[None, :]
    return jnp.tile(jnp.cos(ang), (1, LANES // half)), jnp.tile(jnp.sin(ang), (1, LANES // half))


def kernel(x_prompt, x_sample, state_pool, cache_ckv, cache_krope, cache_mem_k, cache_mem_v, page_table, mem_prompt, g_norm, w_in_a, w_pool_grp, pool_scale, w_in_b, g_q_latent, w_q_up, g_kv_in, w_kv_down, g_kv_latent, w_k_up, w_v_up, g_mem, w_mem_k, w_mem_v, w_out, g_final):
    B, T, _ = x_prompt.shape
    SB, ST, _ = x_sample.shape
    TP = SUBLANES
    past = page_table.shape[1] * cache_ckv.shape[1]

    w_in_a16 = w_in_a.astype(BF16)
    w_grp16 = w_pool_grp.astype(BF16)
    w_out16 = w_out.astype(BF16)
    w_inb16 = w_in_b.astype(BF16)
    wq = w_q_up.reshape(N_A, Q_RANK, MLA_HEADS, QK_NOPE + QK_ROPE)
    w_qn16 = wq[..., :QK_NOPE].reshape(N_A, Q_RANK, MLA_HEADS * QK_NOPE).astype(BF16)
    w_qr16 = wq[..., QK_NOPE:].reshape(N_A, Q_RANK, MLA_HEADS * QK_ROPE).astype(BF16)
    w_kup16 = jnp.transpose(w_k_up, (1, 2, 0)).astype(BF16)
    w_vup16 = jnp.transpose(w_v_up, (1, 0, 2)).astype(BF16)
    w_kv16 = jnp.pad(w_kv_down, ((0, 0), (0, KCAT - KV_RANK - QK_ROPE))).astype(BF16)
    w_mk16 = w_mem_k.astype(BF16)
    w_mv16 = w_mem_v.astype(BF16)
    g_kv_in2 = g_kv_in[None]
    g_lat2 = g_kv_latent[None]
    g_final2 = g_final[None]

    w_kupt16 = w_kup16.reshape(MLA_HEADS * QK_NOPE, KV_RANK)
    w_vall16 = w_v_up.reshape(KV_RANK, MLA_WIDTH).astype(BF16)

    def trunk(x, prev, mem_k, mem_v, pos, *, tag, nb, tq_a, tq, tq_b, heads, n_valid, attend):
        pools = []
        for l in range(N_A):
            x, pool = _a_layer(x, prev[l], g_norm[l][None], w_in_a16, w_grp16, pool_scale[l][None],
                               mem_k, mem_v, w_out16, layer=l, nb=nb, tq=tq_a, n_valid=n_valid, pos0=pos)
            pools.append(pool[:, 1:])
        cos, sin = _rope_tables(pos + jnp.arange(x.shape[1], dtype=jnp.int32))
        ckv, krope, *keys = _kv_latent(x, g_kv_in2, w_kv16, g_lat2, cos, sin, w_kupt16, w_vall16,
                                       nb=nb, tq=tq, tk=min(tq, ATT_TILE), heads=heads)
        for j in range(DEPTH - N_A):
            l = N_A + j
            q, zrest = _b_pre(x, g_norm[l][None], w_inb16, g_q_latent[j][None],
                              w_qn16, w_qr16, w_kup16, cos, sin, j=j, nb=nb, tq=tq_b, absorb=not heads,
                              tag=f"{tag}{l}")
            x = attend(x, q, zrest, *keys, layer=l, final=(l == DEPTH - 1))
        if heads:
            krope = jnp.transpose(krope, (0, 2, 1))
        return x, jnp.stack(pools, axis=0), ckv, krope

    mem_k_p, mem_v_p = _mem_project(mem_prompt, g_mem, w_mk16, w_mv16)
    prev_p = jnp.zeros((N_A, B, HALO, D_MODEL), F32)

    def attend_prompt(x, q, zrest, knt, krt, v, *, layer, final):
        return _flash_post(q, knt, krt, v, x, zrest, mem_k_p, mem_v_p, w_out16, g_final2,
                           layer=layer, tq=ATT_TILE, final=final)

    y_p, pool_p, ckv_p, krope_p = trunk(
        x_prompt, prev_p, mem_k_p, mem_v_p, 0, tag="p", nb=1, tq_a=512, tq=512, tq_b=512, heads=True, n_valid=512,
        attend=attend_prompt)

    xs = jnp.pad(x_sample, ((0, 0), (0, TP - ST), (0, 0)))
    prev_s = jnp.pad(state_pool, ((0, 0), (0, 0), (HALO - POOL_BUF, 0), (0, 0)))
    mem_k_s = cache_mem_k.reshape(DEPTH, SB, MEM_ROWS, MEM_HEAD_DIM)
    mem_v_s = cache_mem_v.reshape(DEPTH, SB, MEM_ROWS, MEM_HEAD_DIM)
    cache_krt = jnp.transpose(cache_krope, (0, 2, 1))

    def attend_sample(x, q, zrest, kcat, *, layer, final):
        qf = q.reshape(SB, MLA_HEADS * TP, KCAT)
        knew = jnp.pad(kcat, ((0, 0), (0, LANES - TP), (0, 0)))
        o = _paged_attention(page_table, qf, knew, cache_ckv, cache_krt, cp=16, tp=TP, n_new=ST)
        tok = _v_up(o.reshape(SB, MLA_HEADS, TP, KV_RANK), w_vup16)
        return _b_post(x, tok, zrest, mem_k_s, mem_v_s, w_out16, g_final2, layer=layer, nb=8, tq=TP,
                       final=final, tag="s")

    y_s, pool_s, ckv_s, krope_s = trunk(
        xs, prev_s, mem_k_s, mem_v_s, past, tag="s", nb=8, tq_a=TP, tq=TP, tq_b=TP, heads=False, n_valid=ST,
        attend=attend_sample)

    mem_shape = (DEPTH, B, MEM_TOKENS, MEM_HEADS, MEM_HEAD_DIM)
    return (y_p, y_s[:, :ST], pool_p, pool_s, ckv_p, krope_p, ckv_s[:, :ST], krope_s[:, :ST],
            mem_k_p.reshape(mem_shape), mem_v_p.reshape(mem_shape))
```

```python
import functools

import jax
import jax.numpy as jnp
from jax import lax
from jax.experimental import pallas as pl
from jax.experimental.pallas import tpu as pltpu

F32 = jnp.float32
BF16 = jnp.bfloat16

D_MODEL = 1024
DEPTH = 4
N_A = 2
POOL_WINDOWS = (2, 4, 8, 16)
POOL_GROUP = 256
POOL_BUF = 15
MLA_HEADS = 8
QK_NOPE = 128
QK_ROPE = 64
V_HEAD = 128
KV_RANK = 256
Q_RANK = 384
MLA_WIDTH = MLA_HEADS * V_HEAD
MLA_SCALE = (QK_NOPE + QK_ROPE) ** -0.5
ROPE_THETA = 10000.0
MEM_TOKENS = 256
MEM_HEADS = 4
MEM_HEAD_DIM = 128
MEM_WIDTH = MEM_HEADS * MEM_HEAD_DIM
MEM_SCALE = MEM_HEAD_DIM ** -0.5
POOL_WIDTH = D_MODEL
IN_A = 2 * POOL_WIDTH + 2 * MEM_WIDTH
Z_REST = MLA_WIDTH + 2 * MEM_WIDTH
OUT_W = POOL_WIDTH + MEM_WIDTH
EPS = 1e-6
NEG = -1e30
LOG2E = 1.4426950408889634

KCAT = 384
SUBLANES = 8
LANES = 128
HALO = 16
E_OFF = SUBLANES + HALO
VMEM_LIMIT = 56 * 1024 * 1024
MEM_ROWS = MEM_TOKENS * MEM_HEADS
ATT_TILE = 256
N_RAW = 4


def _cparams(n_axes):
    return pltpu.CompilerParams(dimension_semantics=("arbitrary",) * n_axes,
                                vmem_limit_bytes=VMEM_LIMIT)


def _rms(x, g):
    return x * lax.rsqrt(jnp.mean(x * x, axis=-1, keepdims=True) + EPS) * g


def _dot(a, b):
    return jnp.dot(a, b, preferred_element_type=F32)


def _dot_nt(a, b):
    return lax.dot_general(a, b, (((1,), (1,)), ((), ())), preferred_element_type=F32)


def _silu(x):
    return x * jax.nn.sigmoid(x)


def _rope_lanes(a, cos, sin):
    lane = lax.broadcasted_iota(jnp.int32, a.shape, 1)
    up = pltpu.roll(a, 32, 1)
    dn = pltpu.roll(a, 96, 1)
    rot = jnp.where((lane & 63) < 32, -dn, up)
    return a * cos + rot * sin


def _rep(x, n):
    return x if n == 1 else jnp.concatenate([x] * n, axis=1)


def _tile_rows(x, n):
    return x if n == 1 else jnp.concatenate([x] * n, axis=0)


def _mem_attend(qm, mk_ref, mv_ref, nb, tq):
    rows, keys = nb * tq, nb * MEM_TOKENS
    if nb > 1:
        row_seq = lax.broadcasted_iota(jnp.int32, (rows, keys), 0) // tq
        col_seq = lax.broadcasted_iota(jnp.int32, (rows, keys), 1) // MEM_TOKENS
        keep = row_seq == col_seq
    outs = []
    for h in range(MEM_HEADS):
        c0, c1 = h * MEM_HEAD_DIM, (h + 1) * MEM_HEAD_DIM
        q = (qm[:, c0:c1] * (MEM_SCALE * LOG2E)).astype(BF16)
        k = mk_ref[0, :, pl.ds(h, MEM_TOKENS, stride=MEM_HEADS), :].reshape(keys, MEM_HEAD_DIM).astype(BF16)
        v = mv_ref[0, :, pl.ds(h, MEM_TOKENS, stride=MEM_HEADS), :].reshape(keys, MEM_HEAD_DIM).astype(BF16)
        s = _dot_nt(q, k)
        if nb > 1:
            s = jnp.where(keep, s, NEG)
        m = jnp.max(s, axis=-1, keepdims=True)
        p = jnp.exp2(s - m)
        l = jnp.sum(p, axis=-1, keepdims=True)
        outs.append(_dot(p.astype(BF16), v) * (1.0 / l))
    return outs


def _mem_proj_kernel(mem_ref, g_ref, wk_ref, wv_ref, mk_ref, mv_ref):
    x = mem_ref[0]
    xn = x * lax.rsqrt(jnp.mean(x * x, axis=-1, keepdims=True) + EPS)
    for l in range(DEPTH):
        h = (xn * g_ref[l:l + 1, :]).astype(BF16)
        mk = _dot(h, wk_ref[l])
        mv = _dot(h, wv_ref[l])
        for hd in range(MEM_HEADS):
            c0, c1 = hd * MEM_HEAD_DIM, (hd + 1) * MEM_HEAD_DIM
            mk_ref[l, 0, pl.ds(hd, MEM_TOKENS, stride=MEM_HEADS), :] = mk[:, c0:c1]
            mv_ref[l, 0, pl.ds(hd, MEM_TOKENS, stride=MEM_HEADS), :] = mv[:, c0:c1]


def _mem_project(mem, g_mem, wk, wv):
    B, M, _ = mem.shape
    out = jax.ShapeDtypeStruct((DEPTH, B, MEM_ROWS, MEM_HEAD_DIM), F32)
    return pl.pallas_call(
        _mem_proj_kernel,
        grid=(B,),
        in_specs=[
            pl.BlockSpec((1, M, D_MODEL), lambda b: (b, 0, 0)),
            pl.BlockSpec((DEPTH, D_MODEL), lambda b: (0, 0)),
            pl.BlockSpec((DEPTH, D_MODEL, MEM_WIDTH), lambda b: (0, 0, 0)),
            pl.BlockSpec((DEPTH, D_MODEL, MEM_WIDTH), lambda b: (0, 0, 0)),
        ],
        out_specs=[
            pl.BlockSpec((DEPTH, 1, MEM_ROWS, MEM_HEAD_DIM), lambda b: (0, b, 0, 0)),
            pl.BlockSpec((DEPTH, 1, MEM_ROWS, MEM_HEAD_DIM), lambda b: (0, b, 0, 0)),
        ],
        out_shape=[out, out],
        compiler_params=_cparams(1),
        name="mem_project",
    )(mem, g_mem, wk, wv)


def _a_layer_kernel(x_ref, prev_ref, g_ref, win_ref, wgrp_ref, pscale_ref, mk_ref, mv_ref, wout_ref,
                    xo_ref, pool_ref, e_sc, sa_sc, sb_sc, mix_sc, *, nb, tq, n_t, n_valid, pos0):
    t = pl.program_id(1)
    rows = HALO + tq
    m = nb * tq

    @pl.when(t == 0)
    def _():
        zeros = jnp.zeros((nb, SUBLANES, D_MODEL), F32)
        e_sc[:, 0:SUBLANES, :] = zeros
        sa_sc[:, 0:SUBLANES, :] = zeros
        sb_sc[:, 0:SUBLANES, :] = zeros
        e_sc[:, SUBLANES:E_OFF, :] = prev_ref[...]

    x = x_ref[...].reshape(m, D_MODEL)
    h = _rms(x, g_ref[...]).astype(BF16)
    e_sc[:, E_OFF:E_OFF + tq, :] = _dot(h, win_ref[:, 0:POOL_WIDTH]).reshape(nb, tq, POOL_WIDTH)

    lo = SUBLANES
    sa_sc[:, lo:lo + rows, :] = e_sc[:, lo:lo + rows, :] + e_sc[:, lo - 1:lo - 1 + rows, :]
    sb_sc[:, lo:lo + rows, 256:1024] = (sa_sc[:, lo:lo + rows, 256:1024]
                                        + sa_sc[:, lo - 2:lo - 2 + rows, 256:1024])
    sa_sc[:, lo:lo + rows, 512:1024] = (sb_sc[:, lo:lo + rows, 512:1024]
                                        + sb_sc[:, lo - 4:lo - 4 + rows, 512:1024])
    s16 = sa_sc[:, E_OFF:E_OFF + tq, 768:1024] + sa_sc[:, E_OFF - 8:E_OFF - 8 + tq, 768:1024]
    sums = (sa_sc[:, E_OFF:E_OFF + tq, 0:256], sb_sc[:, E_OFF:E_OFF + tq, 256:512],
            sa_sc[:, E_OFF:E_OFF + tq, 512:768], s16)

    pos = pos0 + t * tq + lax.broadcasted_iota(jnp.int32, (1, tq, 1), 1)
    for g, w in enumerate(POOL_WINDOWS):
        c0, c1 = g * POOL_GROUP, (g + 1) * POOL_GROUP
        inv_cnt = 1.0 / jnp.minimum(pos + 1, w).astype(F32)
        pooled = (sums[g] * inv_cnt - e_sc[:, E_OFF:E_OFF + tq, c0:c1]).reshape(m, POOL_GROUP)
        mixed = _dot(pooled.astype(BF16), wgrp_ref[g]) * pscale_ref[:, c0:c1]
        gate = _dot(h, win_ref[:, POOL_WIDTH + c0:POOL_WIDTH + c1])
        mix_sc[:, c0:c1] = (mixed * _silu(gate)).astype(BF16)

    qm = _dot(h, win_ref[:, 2 * POOL_WIDTH:2 * POOL_WIDTH + MEM_WIDTH])
    gate_m = _dot(h, win_ref[:, 2 * POOL_WIDTH + MEM_WIDTH:IN_A])
    mem_o = _mem_attend(qm, mk_ref, mv_ref, nb, tq)
    for hd in range(MEM_HEADS):
        c0, c1 = hd * MEM_HEAD_DIM, (hd + 1) * MEM_HEAD_DIM
        mix_sc[:, POOL_WIDTH + c0:POOL_WIDTH + c1] =(mem_o[hd] * _silu(gate_m[:, c0:c1])).astype(BF16)
    xo_ref[...] = (x + _dot(mix_sc[...], wout_ref[...])).reshape(nb, tq, D_MODEL)

    @pl.when(t == n_t - 1)
    def _():
        pool_ref[...] = e_sc[:, E_OFF + n_valid - HALO:E_OFF + n_valid, :]

    if n_t > 1:
        @pl.when(t < n_t - 1)
        def _():
            e_sc[:, SUBLANES:E_OFF, :] = e_sc[:, SUBLANES + tq:E_OFF + tq, :]


def _a_layer(x, prev, g, w_in, w_grp, pscale, mem_k, mem_v, w_out, *, layer, nb, tq, n_valid, pos0):
    B, T, _ = x.shape
    n_t = T // tq
    rows = E_OFF + tq
    kern = functools.partial(_a_layer_kernel, nb=nb, tq=tq, n_t=n_t, n_valid=n_valid, pos0=pos0)
    return pl.pallas_call(
        kern,
        grid=(B // nb, n_t),
        in_specs=[
            pl.BlockSpec((nb, tq, D_MODEL), lambda b, t: (b, t, 0)),
            pl.BlockSpec((nb, HALO, D_MODEL), lambda b, t: (b, 0, 0)),
            pl.BlockSpec((1, D_MODEL), lambda b, t: (0, 0)),
            pl.BlockSpec((None, D_MODEL, IN_A), lambda b, t: (layer, 0, 0)),
            pl.BlockSpec((None, 4, POOL_GROUP, POOL_GROUP), lambda b, t: (layer, 0, 0, 0)),
            pl.BlockSpec((1, D_MODEL), lambda b, t: (0, 0)),
            pl.BlockSpec((1, nb, MEM_ROWS, MEM_HEAD_DIM), lambda b, t: (layer, b, 0, 0)),
            pl.BlockSpec((1, nb, MEM_ROWS, MEM_HEAD_DIM), lambda b, t: (layer, b, 0, 0)),
            pl.BlockSpec((None, OUT_W, D_MODEL), lambda b, t: (layer, 0, 0)),
        ],
        out_specs=[
            pl.BlockSpec((nb, tq, D_MODEL), lambda b, t: (b, t, 0)),
            pl.BlockSpec((nb, HALO, D_MODEL), lambda b, t: (b, 0, 0)),
        ],
        out_shape=[jax.ShapeDtypeStruct((B, T, D_MODEL), F32),
                   jax.ShapeDtypeStruct((B, HALO, D_MODEL), F32)],
        scratch_shapes=[pltpu.VMEM((nb, rows, D_MODEL), F32)] * 3 + [pltpu.VMEM((nb * tq, OUT_W), BF16)],
        compiler_params=_cparams(2),
        name=f"a_layer{layer}_{B}",
    )(x, prev, g, w_in, w_grp, pscale, mem_k, mem_v, w_out)


def _kv_latent_kernel(x_ref, gin_ref, wkv_ref, glat_ref, cos_ref, sin_ref, wkupt_ref, wvup_ref,
                      ckv_ref, kr_ref, *out_refs, heads):
    nb, tq, _ = x_ref.shape
    m = nb * tq
    h = _rms(x_ref[...].reshape(m, D_MODEL), gin_ref[...]).astype(BF16)
    kv = _dot(h, wkv_ref[...])
    ckv = _rms(kv[:, 0:KV_RANK], glat_ref[...])
    kr = _rope_lanes(kv[:, KV_RANK:KCAT], _tile_rows(cos_ref[...], nb), _tile_rows(sin_ref[...], nb))
    ckv_ref[...] = ckv.reshape(nb, tq, KV_RANK)
    if not heads:
        kcat_ref, = out_refs
        kr_ref[...] = kr[:, 0:QK_ROPE].reshape(nb, tq, QK_ROPE)
        kcat_ref[:, :, 0:KV_RANK] = ckv.reshape(nb, tq, KV_RANK).astype(BF16)
        kcat_ref[:, :, KV_RANK:KCAT] = kr.reshape(nb, tq, LANES).astype(BF16)
    else:
        knt_ref, krt_ref, v_ref = out_refs
        tk = knt_ref.shape[-1]
        v_ref[0] = _dot(ckv.astype(BF16), wvup_ref[...]).astype(BF16)
        for sub in range(tq // tk):
            r0 = sub * tk
            ckv_t = ckv[r0:r0 + tk, :].T.astype(BF16)
            knt_ref[0, sub] = _dot(wkupt_ref[...], ckv_t).astype(BF16)
            kr_t = kr[r0:r0 + tk, :].T
            kr_ref[0, :, r0:r0 + tk] = kr_t[0:QK_ROPE, :]
            krt_ref[0, sub] = kr_t.astype(BF16)


def _kv_latent(x, g_in, w_kv, g_lat, cos, sin, w_kupt, w_vup, *, nb, tq, tk, heads):
    B, T, _ = x.shape
    assert nb == 1 or not heads
    n_sub = tq // tk
    out_specs = [
        pl.BlockSpec((nb, tq, KV_RANK), lambda b, t: (b, t, 0)),
        pl.BlockSpec((1, QK_ROPE, tq), lambda b, t: (b, 0, t)) if heads
        else pl.BlockSpec((nb, tq, QK_ROPE), lambda b, t: (b, t, 0)),
    ]
    out_shape = [jax.ShapeDtypeStruct((B, T, KV_RANK), F32),
                 jax.ShapeDtypeStruct((B, QK_ROPE, T) if heads else (B, T, QK_ROPE), F32)]
    if heads:
        out_specs += [pl.BlockSpec((1, n_sub, MLA_HEADS * QK_NOPE, tk), lambda b, t: (b, t, 0, 0)),
                      pl.BlockSpec((1, n_sub, LANES, tk), lambda b, t: (b, t, 0, 0)),
                      pl.BlockSpec((1, tq, MLA_WIDTH), lambda b, t: (b, t, 0))]
        out_shape += [jax.ShapeDtypeStruct((B, T // tk, MLA_HEADS * QK_NOPE, tk), BF16),
                      jax.ShapeDtypeStruct((B, T // tk, LANES, tk), BF16),
                      jax.ShapeDtypeStruct((B, T, MLA_WIDTH), BF16)]
    else:
        out_specs.append(pl.BlockSpec((nb, tq, KCAT), lambda b, t: (b, t, 0)))
        out_shape.append(jax.ShapeDtypeStruct((B, T, KCAT), BF16))
    return pl.pallas_call(
        functools.partial(_kv_latent_kernel, heads=heads),
        grid=(B // nb, T // tq),
        in_specs=[
            pl.BlockSpec((nb, tq, D_MODEL), lambda b, t: (b, t, 0)),
            pl.BlockSpec((1, D_MODEL), lambda b, t: (0, 0)),
            pl.BlockSpec((D_MODEL, KCAT), lambda b, t: (0, 0)),
            pl.BlockSpec((1, KV_RANK), lambda b, t: (0, 0)),
            pl.BlockSpec((tq, LANES), lambda b, t: (t, 0)),
            pl.BlockSpec((tq, LANES), lambda b, t: (t, 0)),
            pl.BlockSpec((MLA_HEADS * QK_NOPE, KV_RANK), lambda b, t: (0, 0)),
            pl.BlockSpec((KV_RANK, MLA_WIDTH), lambda b, t: (0, 0)),
        ],
        out_specs=out_specs,
        out_shape=out_shape,
        compiler_params=_cparams(2),
        name=f"kv_latent_{B}",
    )(x, g_in, w_kv, g_lat, cos, sin, w_kupt, w_vup)


def _b_pre_kernel(x_ref, g_ref, winb_ref, gq_ref, wqn_ref, wqr_ref, wkup_ref, cos_ref, sin_ref,
                  q_ref, zrest_ref, *, absorb):
    nb, tq, _ = x_ref.shape
    m = nb * tq
    n0 = KV_RANK if absorb else QK_NOPE
    h = _rms(x_ref[...].reshape(m, D_MODEL), g_ref[...]).astype(BF16)
    zrest_ref[...] = _dot(h, winb_ref[:, Q_RANK:]).reshape(nb, tq, Z_REST)
    cn = _rms(_dot(h, winb_ref[:, 0:Q_RANK]), gq_ref[...]).astype(BF16)
    qn = _dot(cn, wqn_ref[...])
    qr = _dot(cn, wqr_ref[...])
    for hd in range(MLA_HEADS):
        ql = qn[:, hd * QK_NOPE:(hd + 1) * QK_NOPE]
        if absorb:
            ql = _dot(ql.astype(BF16), wkup_ref[hd])
        q_ref[:, hd, :, 0:n0] = (ql * (MLA_SCALE * LOG2E)).reshape(nb, tq, n0).astype(BF16)
    cos = _tile_rows(cos_ref[...], nb)
    sin = _tile_rows(sin_ref[...], nb)
    lane = lax.broadcasted_iota(jnp.int32, cos.shape, 1)
    for j in range(MLA_HEADS // 2):
        rr = _rope_lanes(qr[:, j * LANES:(j + 1) * LANES], cos, sin) * (MLA_SCALE * LOG2E)
        even = jnp.where(lane < QK_ROPE, rr, 0.0)
        odd = jnp.where(lane < QK_ROPE, pltpu.roll(rr, 64, 1), 0.0)
        q_ref[:, 2 * j, :, n0:n0 + LANES] = even.reshape(nb, tq, LANES).astype(BF16)
        q_ref[:, 2 * j + 1, :, n0:n0 + LANES] = odd.reshape(nb, tq, LANES).astype(BF16)


def _b_pre(x, g, w_inb, g_q, w_qn, w_qr, w_kup, cos, sin, *, j, nb, tq, absorb, tag):
    B, T, _ = x.shape
    const2 = lambda b, t: (0, 0)
    qw = (KV_RANK if absorb else QK_NOPE) + LANES
    return pl.pallas_call(
        functools.partial(_b_pre_kernel, absorb=absorb),
        grid=(B // nb, T // tq),
        in_specs=[
            pl.BlockSpec((nb, tq, D_MODEL), lambda b, t: (b, t, 0)),
            pl.BlockSpec((1, D_MODEL), const2),
            pl.BlockSpec((None, D_MODEL, Q_RANK + Z_REST), lambda b, t: (j, 0, 0)),
            pl.BlockSpec((1, Q_RANK), const2),
            pl.BlockSpec((None, Q_RANK, MLA_HEADS * QK_NOPE), lambda b, t: (j, 0, 0)),
            pl.BlockSpec((None, Q_RANK, MLA_HEADS * QK_ROPE), lambda b, t: (j, 0, 0)),
            pl.BlockSpec((MLA_HEADS, QK_NOPE, KV_RANK), lambda b, t: (0, 0, 0)),
            pl.BlockSpec((tq, LANES), lambda b, t: (t, 0)),
            pl.BlockSpec((tq, LANES), lambda b, t: (t, 0)),
        ],
        out_specs=[
            pl.BlockSpec((nb, MLA_HEADS, tq, qw), lambda b, t: (b, 0, t, 0)),
            pl.BlockSpec((nb, tq, Z_REST), lambda b, t: (b, t, 0)),
        ],
        out_shape=[jax.ShapeDtypeStruct((B, MLA_HEADS, T, qw), BF16),
                   jax.ShapeDtypeStruct((B, T, Z_REST), F32)],
        compiler_params=_cparams(2),
        name=f"b_pre_{tag}",
    )(x, g, w_inb, g_q, w_qn, w_qr, w_kup, cos, sin)


def _flash_kernel(q_ref, knt_ref, krt_ref, v_ref, x_ref, z_ref, mk_ref, mv_ref, wout_ref, gf_ref, xo_ref,
                  m_sc, l_sc, acc_sc, tok_sc, mix_sc, *, tq, tk, final):
    i = pl.program_id(1)
    m_sc[...] = jnp.full(m_sc.shape, NEG, F32)
    l_sc[...] = jnp.zeros(l_sc.shape, F32)
    acc_sc[...] = jnp.zeros(acc_sc.shape, F32)

    def block(j, masked):
        krt = krt_ref[0, j]
        rows = pl.ds(pl.multiple_of(j * tk, tk), tk)
        if masked:
            keep = (lax.broadcasted_iota(jnp.int32, (tq, tk), 1)
                    <= lax.broadcasted_iota(jnp.int32, (tq, tk), 0))

        def scores(hd):
            kt = jnp.concatenate([knt_ref[0, j, hd * QK_NOPE:(hd + 1) * QK_NOPE, :], krt], axis=0)
            return _dot(q_ref[0, hd], kt)

        s_next = scores(0)
        for hd in range(MLA_HEADS):
            s = s_next
            if hd + 1 < MLA_HEADS:
                s_next = scores(hd + 1)
            if masked:
                s = jnp.where(keep, s, NEG)
            m_prev = m_sc[hd]
            m_new = jnp.maximum(m_prev, jnp.max(s, axis=-1, keepdims=True))
            alpha = jnp.exp2(m_prev - m_new)
            p = jnp.exp2(s - _rep(m_new, tk // LANES))
            p_lanes = p[:, 0:LANES]
            for c in range(1, tk // LANES):
                p_lanes = p_lanes + p[:, c * LANES:(c + 1) * LANES]
            l_sc[hd] = alpha * l_sc[hd] + p_lanes
            v = v_ref[0, rows, hd * V_HEAD:(hd + 1) * V_HEAD]
            acc_sc[hd] = alpha * acc_sc[hd] + _dot(p.astype(BF16), v)
            m_sc[hd] = m_new

    def body(jj, carry):
        block(2 * jj, False)
        block(2 * jj + 1, False)
        return carry

    lax.fori_loop(0, i // 2, body, 0)

    @pl.when(i % 2 == 1)
    def _():
        block(i - 1, False)

    block(i, True)

    for hd in range(MLA_HEADS):
        l = jnp.sum(l_sc[hd], axis=-1, keepdims=True)
        tok_sc[0, :, hd * V_HEAD:(hd + 1) * V_HEAD] = acc_sc[hd] * (1.0 / l)

    _mix_out(x_ref, tok_sc, z_ref, mk_ref, mv_ref, wout_ref, gf_ref, xo_ref, mix_sc, final)


def _flash_post(q, knt, krt, v, x, zrest, mem_k, mem_v, w_out, g_final, *, layer, tq, final):
    B, H, T, qw = q.shape
    tk = knt.shape[-1]
    assert tq == tk and qw == QK_NOPE + LANES
    kern = functools.partial(_flash_kernel, tq=tq, tk=tk, final=final)
    return pl.pallas_call(
        kern,
        grid=(B, T // tq),
        in_specs=[
            pl.BlockSpec((1, H, tq, qw), lambda b, i: (b, 0, i, 0)),
            pl.BlockSpec((1, T // tk, H * QK_NOPE, tk), lambda b, i: (b, 0, 0, 0)),
            pl.BlockSpec((1, T // tk, LANES, tk), lambda b, i: (b, 0, 0, 0)),
            pl.BlockSpec((1, T, MLA_WIDTH), lambda b, i: (b, 0, 0)),
            pl.BlockSpec((1, tq, D_MODEL), lambda b, i: (b, i, 0)),
            pl.BlockSpec((1, tq, Z_REST), lambda b, i: (b, i, 0)),
            pl.BlockSpec((1, 1, MEM_ROWS, MEM_HEAD_DIM), lambda b, i: (layer, b, 0, 0)),
            pl.BlockSpec((1, 1, MEM_ROWS, MEM_HEAD_DIM), lambda b, i: (layer, b, 0, 0)),
            pl.BlockSpec((None, OUT_W, D_MODEL), lambda b, i: (layer, 0, 0)),
            pl.BlockSpec((1, D_MODEL), lambda b, i: (0, 0)),
        ],
        out_specs=pl.BlockSpec((1, tq, D_MODEL), lambda b, i: (b, i, 0)),
        out_shape=jax.ShapeDtypeStruct((B, T, D_MODEL), F32),
        scratch_shapes=[pltpu.VMEM((H, tq, LANES), F32), pltpu.VMEM((H, tq, LANES), F32),
                        pltpu.VMEM((H, tq, V_HEAD), F32), pltpu.VMEM((1, tq, MLA_WIDTH), F32),
                        pltpu.VMEM((tq, OUT_W), BF16)],
        compiler_params=_cparams(2),
        name=f"flash_post{layer}",
    )(q, knt, krt, v, x, zrest, mem_k, mem_v, w_out, g_final)


def _paged_kernel(pt_ref, q_ref, knew_ref, ckv_hbm, krt_hbm, o_ref,
                  ckv_buf, krt_buf, kbf_sc, s_sc, sem, m_sc, l_sc, acc_sc,
                  *, n_b, n_ch, cp, page, tp, n_new):
    total = n_b * n_ch

    def copies(bc):
        slot = bc % N_RAW
        b = bc // n_ch
        c = bc % n_ch
        out = []
        for p in range(cp):
            pg = pt_ref[b, c * cp + p]
            out.append(pltpu.make_async_copy(ckv_hbm.at[pg], ckv_buf.at[slot, pl.ds(p * page, page)],
                                             sem.at[0, slot]))
            out.append(pltpu.make_async_copy(krt_hbm.at[pg], krt_buf.at[slot, p], sem.at[1, slot]))
        return out

    def score(bc, w):
        slot = bc % N_RAW
        b = bc // n_ch
        for cpy in copies(bc):
            cpy.wait()
        q_lat = q_ref[b, :, 0:KV_RANK]
        q_rope = q_ref[b, :, KV_RANK:KV_RANK + QK_ROPE]
        kc = ckv_buf[slot].astype(BF16)
        krt = jnp.concatenate([krt_buf[slot, p] for p in range(cp)], axis=1).astype(BF16)
        kbf_sc[w] = kc
        s_sc[w] = _dot_nt(q_lat, kc) + _dot(q_rope, krt)

    def update(s, v_bf16, first):
        m_prev = jnp.where(first, NEG, m_sc[...])
        l_prev = jnp.where(first, 0.0, l_sc[...])
        acc_prev = jnp.where(first, 0.0, acc_sc[...])
        m_new = jnp.maximum(m_prev, jnp.max(s, axis=-1, keepdims=True))
        alpha = jnp.exp2(m_prev - m_new)
        p = jnp.exp2(s - _rep(m_new, s.shape[1] // LANES))
        l_sc[...] = alpha * l_prev + jnp.sum(p, axis=-1, keepdims=True)
        acc_sc[...] = _rep(alpha, KV_RANK // LANES) * acc_prev + _dot(p.astype(BF16), v_bf16)
        m_sc[...] = m_new

    def attend(bc, w):
        b = bc // n_ch
        c = bc % n_ch
        update(s_sc[w], kbf_sc[w], c == 0)

        def finish():
            kn = knew_ref[b]
            sn = _dot_nt(q_ref[b], kn)
            tpos = lax.broadcasted_iota(jnp.int32, sn.shape, 0) & (tp - 1)
            col = lax.broadcasted_iota(jnp.int32, sn.shape, 1)
            sn = jnp.where((col <= tpos) & (col < n_new), sn, NEG)
            update(sn, kn[:, 0:KV_RANK], False)
            o_ref[b] = acc_sc[...] * _rep(1.0 / l_sc[...], KV_RANK // LANES)

        last = c == n_ch - 1
        if isinstance(last, bool):
            if last:
                finish()
        else:
            pl.when(last)(finish)

    m_sc[...] = jnp.full(m_sc.shape, NEG, F32)
    l_sc[...] = jnp.zeros(l_sc.shape, F32)
    acc_sc[...] = jnp.zeros(acc_sc.shape, F32)
    for bc in range(min(N_RAW, total)):
        for cpy in copies(bc):
            cpy.start()
    score(0, 0)

    def step(i, w):
        def prefetch():
            for cpy in copies(i + N_RAW):
                cpy.start()

        more = i + N_RAW < total
        if isinstance(more, bool):
            if more:
                prefetch()
        else:
            pl.when(more)(prefetch)
        if isinstance(i, int) and i + 1 >= total:
            attend(i, w)
            return
        score(i + 1, 1 - w)
        attend(i, w)

    def body(k, carry):
        step(2 * k, 0)
        step(2 * k + 1, 1)
        return carry

    n_pairs = (total - 2) // 2
    lax.fori_loop(0, n_pairs, body, 0)
    for i in range(2 * n_pairs, total):
        step(i, i % 2)


def _paged_attention(page_table, q, knew, cache_ckv, cache_krt, *, cp, tp, n_new):
    n_b, n_pages = page_table.shape
    page = cache_ckv.shape[1]
    rows = q.shape[1]
    n_ch = n_pages // cp
    kern = functools.partial(_paged_kernel, n_b=n_b, n_ch=n_ch, cp=cp, page=page, tp=tp, n_new=n_new)
    grid_spec = pltpu.PrefetchScalarGridSpec(
        num_scalar_prefetch=1,
        grid=(1,),
        in_specs=[
            pl.BlockSpec(q.shape, lambda i, pt: (0, 0, 0)),
            pl.BlockSpec(knew.shape, lambda i, pt: (0, 0, 0)),
            pl.BlockSpec(memory_space=pl.ANY),
            pl.BlockSpec(memory_space=pl.ANY),
        ],
        out_specs=pl.BlockSpec((n_b, rows, KV_RANK), lambda i, pt: (0, 0, 0)),
        scratch_shapes=[
            pltpu.VMEM((N_RAW, cp * page, KV_RANK), F32),
            pltpu.VMEM((N_RAW, cp, QK_ROPE, page), F32),
            pltpu.VMEM((2, cp * page, KV_RANK), BF16),
            pltpu.VMEM((2, rows, cp * page), F32),
            pltpu.SemaphoreType.DMA((2, N_RAW)),
            pltpu.VMEM((rows, LANES), F32),
            pltpu.VMEM((rows, LANES), F32),
            pltpu.VMEM((rows, KV_RANK), F32),
        ],
    )
    return pl.pallas_call(
        kern,
        grid_spec=grid_spec,
        out_shape=jax.ShapeDtypeStruct((n_b, rows, KV_RANK), F32),
        compiler_params=_cparams(1),
        name="paged_attention",
    )(page_table, q, knew, cache_ckv, cache_krt)


def _v_up_kernel(o_ref, wv_ref, tok_ref):
    n_b, _, tp, _ = o_ref.shape
    for hd in range(MLA_HEADS):
        oh = o_ref[:, hd].reshape(n_b * tp, KV_RANK).astype(BF16)
        tok_ref[:, :, hd * V_HEAD:(hd + 1) * V_HEAD] = _dot(oh, wv_ref[hd]).reshape(n_b, tp, V_HEAD)


def _v_up(o, w_vup):
    n_b, H, tp, _ = o.shape
    return pl.pallas_call(
        _v_up_kernel,
        out_shape=jax.ShapeDtypeStruct((n_b, tp, MLA_WIDTH), F32),
        compiler_params=pltpu.CompilerParams(vmem_limit_bytes=VMEM_LIMIT),
        name="v_up_sample",
    )(o, w_vup)


def _b_post_kernel(x_ref, tok_ref, z_ref, mk_ref, mv_ref, wout_ref, gf_ref, xo_ref, mix_sc, *, final):
    _mix_out(x_ref, tok_ref, z_ref, mk_ref, mv_ref, wout_ref, gf_ref, xo_ref, mix_sc, final)


def _mix_out(x_ref, tok_ref, z_ref, mk_ref, mv_ref, wout_ref, gf_ref, xo_ref, mix_sc, final):
    nb, tq, _ = x_ref.shape
    m = nb * tq
    for g in range(4):
        c0, c1 = g * 256, (g + 1) * 256
        mix_sc[:, c0:c1] = (tok_ref[:, :, c0:c1] * _silu(z_ref[:, :, c0:c1])).reshape(m, 256).astype(BF16)
    g0 = MLA_WIDTH + MEM_WIDTH
    mem_o = _mem_attend(z_ref[:, :, MLA_WIDTH:g0].reshape(m, MEM_WIDTH), mk_ref, mv_ref, nb, tq)
    for hd in range(MEM_HEADS):
        c0, c1 = hd * MEM_HEAD_DIM, (hd + 1) * MEM_HEAD_DIM
        gate = z_ref[:, :, g0 + c0:g0 + c1].reshape(m, MEM_HEAD_DIM)
        mix_sc[:, POOL_WIDTH + c0:POOL_WIDTH + c1] =(mem_o[hd] * _silu(gate)).astype(BF16)
    acc = x_ref[...].reshape(m, D_MODEL) + _dot(mix_sc[...], wout_ref[...])
    if final:
        acc = _rms(acc, gf_ref[...])
    xo_ref[...] = acc.reshape(nb, tq, D_MODEL)


def _b_post(x, tok, zrest, mem_k, mem_v, w_out, g_final, *, layer, nb, tq, final, tag):
    B, T, _ = x.shape
    kern = functools.partial(_b_post_kernel, final=final)
    return pl.pallas_call(
        kern,
        grid=(B // nb, T // tq),
        in_specs=[
            pl.BlockSpec((nb, tq, D_MODEL), lambda b, t: (b, t, 0)),
            pl.BlockSpec((nb, tq, MLA_WIDTH), lambda b, t: (b, t, 0)),
            pl.BlockSpec((nb, tq, Z_REST), lambda b, t: (b, t, 0)),
            pl.BlockSpec((1, nb, MEM_ROWS, MEM_HEAD_DIM), lambda b, t: (layer, b, 0, 0)),
            pl.BlockSpec((1, nb, MEM_ROWS, MEM_HEAD_DIM), lambda b, t: (layer, b, 0, 0)),
            pl.BlockSpec((None, OUT_W, D_MODEL), lambda b, t: (layer, 0, 0)),
            pl.BlockSpec((1, D_MODEL), lambda b, t: (0, 0)),
        ],
        out_specs=pl.BlockSpec((nb, tq, D_MODEL), lambda b, t: (b, t, 0)),
        out_shape=jax.ShapeDtypeStruct((B, T, D_MODEL), F32),
        scratch_shapes=[pltpu.VMEM((nb * tq, OUT_W), BF16)],
        compiler_params=_cparams(2),
        name=f"b_post{layer}_{tag}",
    )(x, tok, zrest, mem_k, mem_v, w_out, g_final)


def _rope_tables(pos):
    half = QK_ROPE // 2
    inv = ROPE_THETA ** (-jnp.arange(half, dtype=F32) / half)
    ang = pos.astype(F32)[:, None] * inv[None, :]
    return jnp.tile(jnp.cos(ang), (1, LANES // half)), jnp.tile(jnp.sin(ang), (1, LANES // half))


def kernel(x_prompt, x_sample, state_pool, cache_ckv, cache_krope, cache_mem_k, cache_mem_v, page_table, mem_prompt, g_norm, w_in_a, w_pool_grp, pool_scale, w_in_b, g_q_latent, w_q_up, g_kv_in, w_kv_down, g_kv_latent, w_k_up, w_v_up, g_mem, w_mem_k, w_mem_v, w_out, g_final):
    B, T, _ = x_prompt.shape
    SB, ST, _ = x_sample.shape
    TP = SUBLANES
    past = page_table.shape[1] * cache_ckv.shape[1]

    w_in_a16 = w_in_a.astype(BF16)
    w_grp16 = w_pool_grp.astype(BF16)
    w_out16 = w_out.astype(BF16)
    w_inb16 = w_in_b.astype(BF16)
    wq = w_q_up.reshape(N_A, Q_RANK, MLA_HEADS, QK_NOPE + QK_ROPE)
    w_qn16 = wq[..., :QK_NOPE].reshape(N_A, Q_RANK, MLA_HEADS * QK_NOPE).astype(BF16)
    w_qr16 = wq[..., QK_NOPE:].reshape(N_A, Q_RANK, MLA_HEADS * QK_ROPE).astype(BF16)
    w_kup16 = jnp.transpose(w_k_up, (1, 2, 0)).astype(BF16)
    w_vup16 = jnp.transpose(w_v_up, (1, 0, 2)).astype(BF16)
    w_kv16 = jnp.pad(w_kv_down, ((0, 0), (0, KCAT - KV_RANK - QK_ROPE))).astype(BF16)
    w_mk16 = w_mem_k.astype(BF16)
    w_mv16 = w_mem_v.astype(BF16)
    g_kv_in2 = g_kv_in[None]
    g_lat2 = g_kv_latent[None]
    g_final2 = g_final[None]

    w_kupt16 = w_kup16.reshape(MLA_HEADS * QK_NOPE, KV_RANK)
    w_vall16 = w_v_up.reshape(KV_RANK, MLA_WIDTH).astype(BF16)

    def trunk(x, prev, mem_k, mem_v, pos, *, tag, nb, tq_a, tq, tq_b, heads, n_valid, attend):
        pools = []
        for l in range(N_A):
            x, pool = _a_layer(x, prev[l], g_norm[l][None], w_in_a16, w_grp16, pool_scale[l][None],
                               mem_k, mem_v, w_out16, layer=l, nb=nb, tq=tq_a, n_valid=n_valid, pos0=pos)
            pools.append(pool[:, 1:])
        cos, sin = _rope_tables(pos + jnp.arange(x.shape[1], dtype=jnp.int32))
        ckv, krope, *keys = _kv_latent(x, g_kv_in2, w_kv16, g_lat2, cos, sin, w_kupt16, w_vall16,
                                       nb=nb, tq=tq, tk=min(tq, ATT_TILE), heads=heads)
        for j in range(DEPTH - N_A):
            l = N_A + j
            q, zrest = _b_pre(x, g_norm[l][None], w_inb16, g_q_latent[j][None],
                              w_qn16, w_qr16, w_kup16, cos, sin, j=j, nb=nb, tq=tq_b, absorb=not heads,
                              tag=f"{tag}{l}")
            x = attend(x, q, zrest, *keys, layer=l, final=(l == DEPTH - 1))
        if heads:
            krope = jnp.transpose(krope, (0, 2, 1))
        return x, jnp.stack(pools, axis=0), ckv, krope

    mem_k_p, mem_v_p = _mem_project(mem_prompt, g_mem, w_mk16, w_mv16)
    prev_p = jnp.zeros((N_A, B, HALO, D_MODEL), F32)

    def attend_prompt(x, q, zrest, knt, krt, v, *, layer, final):
        return _flash_post(q, knt, krt, v, x, zrest, mem_k_p, mem_v_p, w_out16, g_final2,
                           layer=layer, tq=ATT_TILE, final=final)

    y_p, pool_p, ckv_p, krope_p = trunk(
        x_prompt, prev_p, mem_k_p, mem_v_p, 0, tag="p", nb=1, tq_a=512, tq=512, tq_b=512, heads=True, n_valid=512,
        attend=attend_prompt)

    xs = jnp.pad(x_sample, ((0, 0), (0, TP - ST), (0, 0)))
    prev_s = jnp.pad(state_pool, ((0, 0), (0, 0), (HALO - POOL_BUF, 0), (0, 0)))
    mem_k_s = cache_mem_k.reshape(DEPTH, SB, MEM_ROWS, MEM_HEAD_DIM)
    mem_v_s = cache_mem_v.reshape(DEPTH, SB, MEM_ROWS, MEM_HEAD_DIM)
    cache_krt = jnp.transpose(cache_krope, (0, 2, 1))

    def attend_sample(x, q, zrest, kcat, *, layer, final):
        qf = q.reshape(SB, MLA_HEADS * TP, KCAT)
        knew = jnp.pad(kcat, ((0, 0), (0, LANES - TP), (0, 0)))
        o = _paged_attention(page_table, qf, knew, cache_ckv, cache_krt, cp=32, tp=TP, n_new=ST)
        tok = _v_up(o.reshape(SB, MLA_HEADS, TP, KV_RANK), w_vup16)
        return _b_post(x, tok, zrest, mem_k_s, mem_v_s, w_out16, g_final2, layer=layer, nb=8, tq=TP,
                       final=final, tag="s")

    y_s, pool_s, ckv_s, krope_s = trunk(
        xs, prev_s, mem_k_s, mem_v_s, past, tag="s", nb=8, tq_a=TP, tq=TP, tq_b=TP, heads=False, n_valid=ST,
        attend=attend_sample)

    mem_shape = (DEPTH, B, MEM_TOKENS, MEM_HEADS, MEM_HEAD_DIM)
    return (y_p, y_s[:, :ST], pool_p, pool_s, ckv_p, krope_p, ckv_s[:, :ST], krope_s[:, :ST],
            mem_k_p.reshape(mem_shape), mem_v_p.reshape(mem_shape))
```

```python
import functools

import jax
import jax.numpy as jnp
from jax import lax
from jax.experimental import pallas as pl
from jax.experimental.pallas import tpu as pltpu

F32 = jnp.float32
BF16 = jnp.bfloat16

D_MODEL = 1024
DEPTH = 4
N_A = 2
POOL_WINDOWS = (2, 4, 8, 16)
POOL_GROUP = 256
POOL_BUF = 15
MLA_HEADS = 8
QK_NOPE = 128
QK_ROPE = 64
V_HEAD = 128
KV_RANK = 256
Q_RANK = 384
MLA_WIDTH = MLA_HEADS * V_HEAD
MLA_SCALE = (QK_NOPE + QK_ROPE) ** -0.5
ROPE_THETA = 10000.0
MEM_TOKENS = 256
MEM_HEADS = 4
MEM_HEAD_DIM = 128
MEM_WIDTH = MEM_HEADS * MEM_HEAD_DIM
MEM_SCALE = MEM_HEAD_DIM ** -0.5
POOL_WIDTH = D_MODEL
IN_A = 2 * POOL_WIDTH + 2 * MEM_WIDTH
Z_REST = MLA_WIDTH + 2 * MEM_WIDTH
OUT_W = POOL_WIDTH + MEM_WIDTH
EPS = 1e-6
NEG = -1e30
LOG2E = 1.4426950408889634

KCAT = 384
SUBLANES = 8
LANES = 128
HALO = 16
E_OFF = SUBLANES + HALO
VMEM_LIMIT = 56 * 1024 * 1024
MEM_ROWS = MEM_TOKENS * MEM_HEADS
ATT_TILE = 256
N_RAW = 4


def _cparams(n_axes):
    return pltpu.CompilerParams(dimension_semantics=("arbitrary",) * n_axes,
                                vmem_limit_bytes=VMEM_LIMIT)


def _rms(x, g):
    return x * lax.rsqrt(jnp.mean(x * x, axis=-1, keepdims=True) + EPS) * g


def _dot(a, b):
    return jnp.dot(a, b, preferred_element_type=F32)


def _dot_nt(a, b):
    return lax.dot_general(a, b, (((1,), (1,)), ((), ())), preferred_element_type=F32)


def _silu(x):
    return x * jax.nn.sigmoid(x)


def _rope_lanes(a, cos, sin):
    half = QK_ROPE // 2
    lane = lax.broadcasted_iota(jnp.int32, a.shape, 1)
    up = pltpu.roll(a, half, 1)
    dn = pltpu.roll(a, LANES - half, 1)
    rot = jnp.where((lane & (QK_ROPE - 1)) < half, -dn, up)
    return a * cos + rot * sin


def _rep(x, n):
    return x if n == 1 else jnp.concatenate([x] * n, axis=1)


def _tile_rows(x, n):
    return x if n == 1 else jnp.concatenate([x] * n, axis=0)


def _mem_attend(qm, mk_ref, mv_ref, nb, tq):
    rows, keys = nb * tq, nb * MEM_TOKENS
    if nb > 1:
        row_seq = lax.broadcasted_iota(jnp.int32, (rows, keys), 0) // tq
        col_seq = lax.broadcasted_iota(jnp.int32, (rows, keys), 1) // MEM_TOKENS
        keep = row_seq == col_seq
    outs = []
    for h in range(MEM_HEADS):
        c0, c1 = h * MEM_HEAD_DIM, (h + 1) * MEM_HEAD_DIM
        q = (qm[:, c0:c1] * (MEM_SCALE * LOG2E)).astype(BF16)
        k = mk_ref[0, :, pl.ds(h, MEM_TOKENS, stride=MEM_HEADS), :].reshape(keys, MEM_HEAD_DIM).astype(BF16)
        v = mv_ref[0, :, pl.ds(h, MEM_TOKENS, stride=MEM_HEADS), :].reshape(keys, MEM_HEAD_DIM).astype(BF16)
        s = _dot_nt(q, k)
        if nb > 1:
            s = jnp.where(keep, s, NEG)
        m = jnp.max(s, axis=-1, keepdims=True)
        p = jnp.exp2(s - m)
        l = jnp.sum(p, axis=-1, keepdims=True)
        outs.append(_dot(p.astype(BF16), v) * (1.0 / l))
    return outs


def _mem_proj_kernel(mem_ref, g_ref, wk_ref, wv_ref, mk_ref, mv_ref):
    x = mem_ref[0]
    xn = x * lax.rsqrt(jnp.mean(x * x, axis=-1, keepdims=True) + EPS)
    for l in range(DEPTH):
        h = (xn * g_ref[l:l + 1, :]).astype(BF16)
        mk = _dot(h, wk_ref[l])
        mv = _dot(h, wv_ref[l])
        for hd in range(MEM_HEADS):
            c0, c1 = hd * MEM_HEAD_DIM, (hd + 1) * MEM_HEAD_DIM
            mk_ref[l, 0, pl.ds(hd, MEM_TOKENS, stride=MEM_HEADS), :] = mk[:, c0:c1]
            mv_ref[l, 0, pl.ds(hd, MEM_TOKENS, stride=MEM_HEADS), :] = mv[:, c0:c1]


def _mem_project(mem, g_mem, wk, wv):
    B, M, _ = mem.shape
    out = jax.ShapeDtypeStruct((DEPTH, B, MEM_ROWS, MEM_HEAD_DIM), F32)
    return pl.pallas_call(
        _mem_proj_kernel,
        grid=(B,),
        in_specs=[
            pl.BlockSpec((1, M, D_MODEL), lambda b: (b, 0, 0)),
            pl.BlockSpec((DEPTH, D_MODEL), lambda b: (0, 0)),
            pl.BlockSpec((DEPTH, D_MODEL, MEM_WIDTH), lambda b: (0, 0, 0)),
            pl.BlockSpec((DEPTH, D_MODEL, MEM_WIDTH), lambda b: (0, 0, 0)),
        ],
        out_specs=[
            pl.BlockSpec((DEPTH, 1, MEM_ROWS, MEM_HEAD_DIM), lambda b: (0, b, 0, 0)),
            pl.BlockSpec((DEPTH, 1, MEM_ROWS, MEM_HEAD_DIM), lambda b: (0, b, 0, 0)),
        ],
        out_shape=[out, out],
        compiler_params=_cparams(1),
        name="mem_project",
    )(mem, g_mem, wk, wv)


def _a_layer_kernel(x_ref, prev_ref, g_ref, win_ref, wgrp_ref, pscale_ref, mk_ref, mv_ref, wout_ref,
                    xo_ref, pool_ref, e_sc, sa_sc, sb_sc, mix_sc, *, nb, tq, n_t, n_valid, pos0):
    t = pl.program_id(1)
    rows = HALO + tq
    m = nb * tq

    @pl.when(t == 0)
    def _():
        zeros = jnp.zeros((nb, SUBLANES, D_MODEL), F32)
        e_sc[:, 0:SUBLANES, :] = zeros
        sa_sc[:, 0:SUBLANES, :] = zeros
        sb_sc[:, 0:SUBLANES, :] = zeros
        e_sc[:, SUBLANES:E_OFF, :] = prev_ref[...]

    x = x_ref[...].reshape(m, D_MODEL)
    h = _rms(x, g_ref[...]).astype(BF16)
    e_sc[:, E_OFF:E_OFF + tq, :] = _dot(h, win_ref[:, 0:POOL_WIDTH]).reshape(nb, tq, POOL_WIDTH)

    lo = SUBLANES
    sa_sc[:, lo:lo + rows, :] = e_sc[:, lo:lo + rows, :] + e_sc[:, lo - 1:lo - 1 + rows, :]
    sb_sc[:, lo:lo + rows, 256:1024] = (sa_sc[:, lo:lo + rows, 256:1024]
                                        + sa_sc[:, lo - 2:lo - 2 + rows, 256:1024])
    sa_sc[:, lo:lo + rows, 512:1024] = (sb_sc[:, lo:lo + rows, 512:1024]
                                        + sb_sc[:, lo - 4:lo - 4 + rows, 512:1024])
    s16 = sa_sc[:, E_OFF:E_OFF + tq, 768:1024] + sa_sc[:, E_OFF - 8:E_OFF - 8 + tq, 768:1024]
    sums = (sa_sc[:, E_OFF:E_OFF + tq, 0:256], sb_sc[:, E_OFF:E_OFF + tq, 256:512],
            sa_sc[:, E_OFF:E_OFF + tq, 512:768], s16)

    pos = pos0 + t * tq + lax.broadcasted_iota(jnp.int32, (1, tq, 1), 1)
    for g, w in enumerate(POOL_WINDOWS):
        c0, c1 = g * POOL_GROUP, (g + 1) * POOL_GROUP
        inv_cnt = 1.0 / jnp.minimum(pos + 1, w).astype(F32)
        pooled = (sums[g] * inv_cnt - e_sc[:, E_OFF:E_OFF + tq, c0:c1]).reshape(m, POOL_GROUP)
        mixed = _dot(pooled.astype(BF16), wgrp_ref[g]) * pscale_ref[:, c0:c1]
        gate = _dot(h, win_ref[:, POOL_WIDTH + c0:POOL_WIDTH + c1])
        mix_sc[:, c0:c1] = (mixed * _silu(gate)).astype(BF16)

    qm = _dot(h, win_ref[:, 2 * POOL_WIDTH:2 * POOL_WIDTH + MEM_WIDTH])
    gate_m = _dot(h, win_ref[:, 2 * POOL_WIDTH + MEM_WIDTH:IN_A])
    mem_o = _mem_attend(qm, mk_ref, mv_ref, nb, tq)
    for hd in range(MEM_HEADS):
        c0, c1 = hd * MEM_HEAD_DIM, (hd + 1) * MEM_HEAD_DIM
        mix_sc[:, POOL_WIDTH + c0:POOL_WIDTH + c1] =(mem_o[hd] * _silu(gate_m[:, c0:c1])).astype(BF16)
    xo_ref[...] = (x + _dot(mix_sc[...], wout_ref[...])).reshape(nb, tq, D_MODEL)

    @pl.when(t == n_t - 1)
    def _():
        pool_ref[...] = e_sc[:, E_OFF + n_valid - HALO:E_OFF + n_valid, :]

    if n_t > 1:
        @pl.when(t < n_t - 1)
        def _():
            e_sc[:, SUBLANES:E_OFF, :] = e_sc[:, SUBLANES + tq:E_OFF + tq, :]


def _a_layer(x, prev, g, w_in, w_grp, pscale, mem_k, mem_v, w_out, *, layer, nb, tq, n_valid, pos0):
    B, T, _ = x.shape
    n_t = T // tq
    rows = E_OFF + tq
    kern = functools.partial(_a_layer_kernel, nb=nb, tq=tq, n_t=n_t, n_valid=n_valid, pos0=pos0)
    return pl.pallas_call(
        kern,
        grid=(B // nb, n_t),
        in_specs=[
            pl.BlockSpec((nb, tq, D_MODEL), lambda b, t: (b, t, 0)),
            pl.BlockSpec((nb, HALO, D_MODEL), lambda b, t: (b, 0, 0)),
            pl.BlockSpec((1, D_MODEL), lambda b, t: (0, 0)),
            pl.BlockSpec((None, D_MODEL, IN_A), lambda b, t: (layer, 0, 0)),
            pl.BlockSpec((None, 4, POOL_GROUP, POOL_GROUP), lambda b, t: (layer, 0, 0, 0)),
            pl.BlockSpec((1, D_MODEL), lambda b, t: (0, 0)),
            pl.BlockSpec((1, nb, MEM_ROWS, MEM_HEAD_DIM), lambda b, t: (layer, b, 0, 0)),
            pl.BlockSpec((1, nb, MEM_ROWS, MEM_HEAD_DIM), lambda b, t: (layer, b, 0, 0)),
            pl.BlockSpec((None, OUT_W, D_MODEL), lambda b, t: (layer, 0, 0)),
        ],
        out_specs=[
            pl.BlockSpec((nb, tq, D_MODEL), lambda b, t: (b, t, 0)),
            pl.BlockSpec((nb, HALO, D_MODEL), lambda b, t: (b, 0, 0)),
        ],
        out_shape=[jax.ShapeDtypeStruct((B, T, D_MODEL), F32),
                   jax.ShapeDtypeStruct((B, HALO, D_MODEL), F32)],
        scratch_shapes=[pltpu.VMEM((nb, rows, D_MODEL), F32)] * 3 + [pltpu.VMEM((nb * tq, OUT_W), BF16)],
        compiler_params=_cparams(2),
        name=f"a_layer{layer}_{B}",
    )(x, prev, g, w_in, w_grp, pscale, mem_k, mem_v, w_out)


def _kv_latent_kernel(x_ref, gin_ref, wkv_ref, glat_ref, cos_ref, sin_ref, wkupt_ref, wvup_ref,
                      ckv_ref, kr_ref, *out_refs, heads):
    nb, tq, _ = x_ref.shape
    m = nb * tq
    h = _rms(x_ref[...].reshape(m, D_MODEL), gin_ref[...]).astype(BF16)
    kv = _dot(h, wkv_ref[...])
    ckv = _rms(kv[:, 0:KV_RANK], glat_ref[...])
    kr = _rope_lanes(kv[:, KV_RANK:KCAT], _tile_rows(cos_ref[...], nb), _tile_rows(sin_ref[...], nb))
    ckv_ref[...] = ckv.reshape(nb, tq, KV_RANK)
    if not heads:
        kcat_ref, = out_refs
        kr_ref[...] = kr[:, 0:QK_ROPE].reshape(nb, tq, QK_ROPE)
        kcat_ref[:, :, 0:KV_RANK] = ckv.reshape(nb, tq, KV_RANK).astype(BF16)
        kcat_ref[:, :, KV_RANK:KCAT] = kr.reshape(nb, tq, LANES).astype(BF16)
    else:
        knt_ref, krt_ref, v_ref = out_refs
        tk = knt_ref.shape[-1]
        v_ref[0] = _dot(ckv.astype(BF16), wvup_ref[...]).astype(BF16)
        for sub in range(tq // tk):
            r0 = sub * tk
            ckv_t = ckv[r0:r0 + tk, :].T.astype(BF16)
            knt_ref[0, sub] = _dot(wkupt_ref[...], ckv_t).astype(BF16)
            kr_t = kr[r0:r0 + tk, :].T
            kr_ref[0, :, r0:r0 + tk] = kr_t[0:QK_ROPE, :]
            krt_ref[0, sub] = kr_t.astype(BF16)


def _kv_latent(x, g_in, w_kv, g_lat, cos, sin, w_kupt, w_vup, *, nb, tq, tk, heads):
    B, T, _ = x.shape
    assert nb == 1 or not heads
    n_sub = tq // tk
    out_specs = [
        pl.BlockSpec((nb, tq, KV_RANK), lambda b, t: (b, t, 0)),
        pl.BlockSpec((1, QK_ROPE, tq), lambda b, t: (b, 0, t)) if heads
        else pl.BlockSpec((nb, tq, QK_ROPE), lambda b, t: (b, t, 0)),
    ]
    out_shape = [jax.ShapeDtypeStruct((B, T, KV_RANK), F32),
                 jax.ShapeDtypeStruct((B, QK_ROPE, T) if heads else (B, T, QK_ROPE), F32)]
    if heads:
        out_specs += [pl.BlockSpec((1, n_sub, MLA_HEADS * QK_NOPE, tk), lambda b, t: (b, t, 0, 0)),
                      pl.BlockSpec((1, n_sub, LANES, tk), lambda b, t: (b, t, 0, 0)),
                      pl.BlockSpec((1, tq, MLA_WIDTH), lambda b, t: (b, t, 0))]
        out_shape += [jax.ShapeDtypeStruct((B, T // tk, MLA_HEADS * QK_NOPE, tk), BF16),
                      jax.ShapeDtypeStruct((B, T // tk, LANES, tk), BF16),
                      jax.ShapeDtypeStruct((B, T, MLA_WIDTH), BF16)]
    else:
        out_specs.append(pl.BlockSpec((nb, tq, KCAT), lambda b, t: (b, t, 0)))
        out_shape.append(jax.ShapeDtypeStruct((B, T, KCAT), BF16))
    return pl.pallas_call(
        functools.partial(_kv_latent_kernel, heads=heads),
        grid=(B // nb, T // tq),
        in_specs=[
            pl.BlockSpec((nb, tq, D_MODEL), lambda b, t: (b, t, 0)),
            pl.BlockSpec((1, D_MODEL), lambda b, t: (0, 0)),
            pl.BlockSpec((D_MODEL, KCAT), lambda b, t: (0, 0)),
            pl.BlockSpec((1, KV_RANK), lambda b, t: (0, 0)),
            pl.BlockSpec((tq, LANES), lambda b, t: (t, 0)),
            pl.BlockSpec((tq, LANES), lambda b, t: (t, 0)),
            pl.BlockSpec((MLA_HEADS * QK_NOPE, KV_RANK), lambda b, t: (0, 0)),
            pl.BlockSpec((KV_RANK, MLA_WIDTH), lambda b, t: (0, 0)),
        ],
        out_specs=out_specs,
        out_shape=out_shape,
        compiler_params=_cparams(2),
        name=f"kv_latent_{B}",
    )(x, g_in, w_kv, g_lat, cos, sin, w_kupt, w_vup)


def _b_pre_kernel(x_ref, g_ref, winb_ref, gq_ref, wqn_ref, wqr_ref, wkup_ref, cos_ref, sin_ref,
                  q_ref, zrest_ref, *, absorb):
    nb, tq, _ = x_ref.shape
    m = nb * tq
    n0 = KV_RANK if absorb else QK_NOPE
    h = _rms(x_ref[...].reshape(m, D_MODEL), g_ref[...]).astype(BF16)
    zrest_ref[...] = _dot(h, winb_ref[:, Q_RANK:]).reshape(nb, tq, Z_REST)
    cn = _rms(_dot(h, winb_ref[:, 0:Q_RANK]), gq_ref[...]).astype(BF16)
    qn = _dot(cn, wqn_ref[...])
    qr = _dot(cn, wqr_ref[...])
    for hd in range(MLA_HEADS):
        ql = qn[:, hd * QK_NOPE:(hd + 1) * QK_NOPE]
        if absorb:
            ql = _dot(ql.astype(BF16), wkup_ref[hd])
        q_ref[:, hd, :, 0:n0] = (ql * (MLA_SCALE * LOG2E)).reshape(nb, tq, n0).astype(BF16)
    cos = _tile_rows(cos_ref[...], nb)
    sin = _tile_rows(sin_ref[...], nb)
    lane = lax.broadcasted_iota(jnp.int32, cos.shape, 1)
    for j in range(MLA_HEADS // 2):
        rr = _rope_lanes(qr[:, j * LANES:(j + 1) * LANES], cos, sin) * (MLA_SCALE * LOG2E)
        even = jnp.where(lane < QK_ROPE, rr, 0.0)
        odd = jnp.where(lane < QK_ROPE, pltpu.roll(rr, 64, 1), 0.0)
        q_ref[:, 2 * j, :, n0:n0 + LANES] = even.reshape(nb, tq, LANES).astype(BF16)
        q_ref[:, 2 * j + 1, :, n0:n0 + LANES] = odd.reshape(nb, tq, LANES).astype(BF16)


def _b_pre(x, g, w_inb, g_q, w_qn, w_qr, w_kup, cos, sin, *, j, nb, tq, absorb, tag):
    B, T, _ = x.shape
    const2 = lambda b, t: (0, 0)
    qw = (KV_RANK if absorb else QK_NOPE) + LANES
    return pl.pallas_call(
        functools.partial(_b_pre_kernel, absorb=absorb),
        grid=(B // nb, T // tq),
        in_specs=[
            pl.BlockSpec((nb, tq, D_MODEL), lambda b, t: (b, t, 0)),
            pl.BlockSpec((1, D_MODEL), const2),
            pl.BlockSpec((None, D_MODEL, Q_RANK + Z_REST), lambda b, t: (j, 0, 0)),
            pl.BlockSpec((1, Q_RANK), const2),
            pl.BlockSpec((None, Q_RANK, MLA_HEADS * QK_NOPE), lambda b, t: (j, 0, 0)),
            pl.BlockSpec((None, Q_RANK, MLA_HEADS * QK_ROPE), lambda b, t: (j, 0, 0)),
            pl.BlockSpec((MLA_HEADS, QK_NOPE, KV_RANK), lambda b, t: (0, 0, 0)),
            pl.BlockSpec((tq, LANES), lambda b, t: (t, 0)),
            pl.BlockSpec((tq, LANES), lambda b, t: (t, 0)),
        ],
        out_specs=[
            pl.BlockSpec((nb, MLA_HEADS, tq, qw), lambda b, t: (b, 0, t, 0)),
            pl.BlockSpec((nb, tq, Z_REST), lambda b, t: (b, t, 0)),
        ],
        out_shape=[jax.ShapeDtypeStruct((B, MLA_HEADS, T, qw), BF16),
                   jax.ShapeDtypeStruct((B, T, Z_REST), F32)],
        compiler_params=_cparams(2),
        name=f"b_pre_{tag}",
    )(x, g, w_inb, g_q, w_qn, w_qr, w_kup, cos, sin)


def _flash_kernel(q_ref, knt_ref, krt_ref, v_ref, x_ref, z_ref, mk_ref, mv_ref, wout_ref, gf_ref, xo_ref,
                  m_sc, l_sc, acc_sc, tok_sc, mix_sc, *, tq, tk, final):
    i = pl.program_id(1)
    m_sc[...] = jnp.full(m_sc.shape, NEG, F32)
    l_sc[...] = jnp.zeros(l_sc.shape, F32)
    acc_sc[...] = jnp.zeros(acc_sc.shape, F32)

    def block(j, masked):
        krt = krt_ref[0, j]
        rows = pl.ds(pl.multiple_of(j * tk, tk), tk)
        if masked:
            keep = (lax.broadcasted_iota(jnp.int32, (tq, tk), 1)
                    <= lax.broadcasted_iota(jnp.int32, (tq, tk), 0))

        def scores(hd):
            kt = jnp.concatenate([knt_ref[0, j, hd * QK_NOPE:(hd + 1) * QK_NOPE, :], krt], axis=0)
            return _dot(q_ref[0, hd], kt)

        s_next = scores(0)
        for hd in range(MLA_HEADS):
            s = s_next
            if hd + 1 < MLA_HEADS:
                s_next = scores(hd + 1)
            if masked:
                s = jnp.where(keep, s, NEG)
            m_prev = m_sc[hd]
            m_new = jnp.maximum(m_prev, jnp.max(s, axis=-1, keepdims=True))
            alpha = jnp.exp2(m_prev - m_new)
            p = jnp.exp2(s - _rep(m_new, tk // LANES))
            p_lanes = p[:, 0:LANES]
            for c in range(1, tk // LANES):
                p_lanes = p_lanes + p[:, c * LANES:(c + 1) * LANES]
            l_sc[hd] = alpha * l_sc[hd] + p_lanes
            v = v_ref[0, rows, hd * V_HEAD:(hd + 1) * V_HEAD]
            acc_sc[hd] = alpha * acc_sc[hd] + _dot(p.astype(BF16), v)
            m_sc[hd] = m_new

    def body(jj, carry):
        block(2 * jj, False)
        block(2 * jj + 1, False)
        return carry

    lax.fori_loop(0, i // 2, body, 0)

    @pl.when(i % 2 == 1)
    def _():
        block(i - 1, False)

    block(i, True)

    for hd in range(MLA_HEADS):
        l = jnp.sum(l_sc[hd], axis=-1, keepdims=True)
        tok_sc[0, :, hd * V_HEAD:(hd + 1) * V_HEAD] = acc_sc[hd] * (1.0 / l)

    _mix_out(x_ref, tok_sc, z_ref, mk_ref, mv_ref, wout_ref, gf_ref, xo_ref, mix_sc, final)


def _flash_post(q, knt, krt, v, x, zrest, mem_k, mem_v, w_out, g_final, *, layer, tq, final):
    B, H, T, qw = q.shape
    tk = knt.shape[-1]
    assert tq == tk and qw == QK_NOPE + LANES
    kern = functools.partial(_flash_kernel, tq=tq, tk=tk, final=final)
    return pl.pallas_call(
        kern,
        grid=(B, T // tq),
        in_specs=[
            pl.BlockSpec((1, H, tq, qw), lambda b, i: (b, 0, i, 0)),
            pl.BlockSpec((1, T // tk, H * QK_NOPE, tk), lambda b, i: (b, 0, 0, 0)),
            pl.BlockSpec((1, T // tk, LANES, tk), lambda b, i: (b, 0, 0, 0)),
            pl.BlockSpec((1, T, MLA_WIDTH), lambda b, i: (b, 0, 0)),
            pl.BlockSpec((1, tq, D_MODEL), lambda b, i: (b, i, 0)),
            pl.BlockSpec((1, tq, Z_REST), lambda b, i: (b, i, 0)),
            pl.BlockSpec((1, 1, MEM_ROWS, MEM_HEAD_DIM), lambda b, i: (layer, b, 0, 0)),
            pl.BlockSpec((1, 1, MEM_ROWS, MEM_HEAD_DIM), lambda b, i: (layer, b, 0, 0)),
            pl.BlockSpec((None, OUT_W, D_MODEL), lambda b, i: (layer, 0, 0)),
            pl.BlockSpec((1, D_MODEL), lambda b, i: (0, 0)),
        ],
        out_specs=pl.BlockSpec((1, tq, D_MODEL), lambda b, i: (b, i, 0)),
        out_shape=jax.ShapeDtypeStruct((B, T, D_MODEL), F32),
        scratch_shapes=[pltpu.VMEM((H, tq, LANES), F32), pltpu.VMEM((H, tq, LANES), F32),
                        pltpu.VMEM((H, tq, V_HEAD), F32), pltpu.VMEM((1, tq, MLA_WIDTH), F32),
                        pltpu.VMEM((tq, OUT_W), BF16)],
        compiler_params=_cparams(2),
        name=f"flash_post{layer}",
    )(q, knt, krt, v, x, zrest, mem_k, mem_v, w_out, g_final)


def _paged_kernel(pt_ref, q_ref, knew_ref, ckv_hbm, krt_hbm, o_ref,
                  ckv_buf, krt_buf, kbf_sc, s_sc, sem, m_sc, l_sc, acc_sc,
                  *, n_b, n_ch, cp, page, tp, n_new):
    total = n_b * n_ch
    assert total >= 2

    def copies(bc):
        slot = bc % N_RAW
        b = bc // n_ch
        c = bc % n_ch
        out = []
        for p in range(cp):
            pg = pt_ref[b, c * cp + p]
            out.append(pltpu.make_async_copy(ckv_hbm.at[pg], ckv_buf.at[slot, pl.ds(p * page, page)],
                                             sem.at[0, slot]))
            out.append(pltpu.make_async_copy(krt_hbm.at[pg], krt_buf.at[slot, p], sem.at[1, slot]))
        return out

    def score(bc, w):
        slot = bc % N_RAW
        b = bc // n_ch
        for cpy in copies(bc):
            cpy.wait()
        q_lat = q_ref[b, :, 0:KV_RANK]
        q_rope = q_ref[b, :, KV_RANK:KV_RANK + QK_ROPE]
        kc = ckv_buf[slot].astype(BF16)
        krt = jnp.concatenate([krt_buf[slot, p] for p in range(cp)], axis=1).astype(BF16)
        kbf_sc[w] = kc
        s_sc[w] = _dot_nt(q_lat, kc) + _dot(q_rope, krt)

    def update(s, v_bf16, first):
        m_prev = jnp.where(first, NEG, m_sc[...])
        l_prev = jnp.where(first, 0.0, l_sc[...])
        acc_prev = jnp.where(first, 0.0, acc_sc[...])
        m_new = jnp.maximum(m_prev, jnp.max(s, axis=-1, keepdims=True))
        alpha = jnp.exp2(m_prev - m_new)
        p = jnp.exp2(s - _rep(m_new, s.shape[1] // LANES))
        l_sc[...] = alpha * l_prev + jnp.sum(p, axis=-1, keepdims=True)
        acc_sc[...] = _rep(alpha, KV_RANK // LANES) * acc_prev + _dot(p.astype(BF16), v_bf16)
        m_sc[...] = m_new

    def attend(bc, w):
        b = bc // n_ch
        c = bc % n_ch
        update(s_sc[w], kbf_sc[w], c == 0)

        def finish():
            kn = knew_ref[b]
            sn = _dot_nt(q_ref[b], kn)
            tpos = lax.broadcasted_iota(jnp.int32, sn.shape, 0) & (tp - 1)
            col = lax.broadcasted_iota(jnp.int32, sn.shape, 1)
            sn = jnp.where((col <= tpos) & (col < n_new), sn, NEG)
            update(sn, kn[:, 0:KV_RANK], False)
            o_ref[b] = acc_sc[...] * _rep(1.0 / l_sc[...], KV_RANK // LANES)

        last = c == n_ch - 1
        if isinstance(last, bool):
            if last:
                finish()
        else:
            pl.when(last)(finish)

    m_sc[...] = jnp.full(m_sc.shape, NEG, F32)
    l_sc[...] = jnp.zeros(l_sc.shape, F32)
    acc_sc[...] = jnp.zeros(acc_sc.shape, F32)
    for bc in range(min(N_RAW, total)):
        for cpy in copies(bc):
            cpy.start()
    score(0, 0)

    def step(i, w):
        def prefetch():
            for cpy in copies(i + N_RAW):
                cpy.start()

        more = i + N_RAW < total
        if isinstance(more, bool):
            if more:
                prefetch()
        else:
            pl.when(more)(prefetch)
        if isinstance(i, int) and i + 1 >= total:
            attend(i, w)
            return
        score(i + 1, 1 - w)
        attend(i, w)

    def body(k, carry):
        step(2 * k, 0)
        step(2 * k + 1, 1)
        return carry

    n_pairs = (total - 2) // 2
    lax.fori_loop(0, n_pairs, body, 0)
    for i in range(2 * n_pairs, total):
        step(i, i % 2)


def _paged_attention(page_table, q, knew, cache_ckv, cache_krt, *, cp, tp, n_new):
    n_b, n_pages = page_table.shape
    page = cache_ckv.shape[1]
    rows = q.shape[1]
    n_ch = n_pages // cp
    kern = functools.partial(_paged_kernel, n_b=n_b, n_ch=n_ch, cp=cp, page=page, tp=tp, n_new=n_new)
    grid_spec = pltpu.PrefetchScalarGridSpec(
        num_scalar_prefetch=1,
        grid=(1,),
        in_specs=[
            pl.BlockSpec(q.shape, lambda i, pt: (0, 0, 0)),
            pl.BlockSpec(knew.shape, lambda i, pt: (0, 0, 0)),
            pl.BlockSpec(memory_space=pl.ANY),
            pl.BlockSpec(memory_space=pl.ANY),
        ],
        out_specs=pl.BlockSpec((n_b, rows, KV_RANK), lambda i, pt: (0, 0, 0)),
        scratch_shapes=[
            pltpu.VMEM((N_RAW, cp * page, KV_RANK), F32),
            pltpu.VMEM((N_RAW, cp, QK_ROPE, page), F32),
            pltpu.VMEM((2, cp * page, KV_RANK), BF16),
            pltpu.VMEM((2, rows, cp * page), F32),
            pltpu.SemaphoreType.DMA((2, N_RAW)),
            pltpu.VMEM((rows, LANES), F32),
            pltpu.VMEM((rows, LANES), F32),
            pltpu.VMEM((rows, KV_RANK), F32),
        ],
    )
    return pl.pallas_call(
        kern,
        grid_spec=grid_spec,
        out_shape=jax.ShapeDtypeStruct((n_b, rows, KV_RANK), F32),
        compiler_params=_cparams(1),
        name="paged_attention",
    )(page_table, q, knew, cache_ckv, cache_krt)


def _v_up_kernel(o_ref, wv_ref, tok_ref):
    n_b, _, tp, _ = o_ref.shape
    for hd in range(MLA_HEADS):
        oh = o_ref[:, hd].reshape(n_b * tp, KV_RANK).astype(BF16)
        tok_ref[:, :, hd * V_HEAD:(hd + 1) * V_HEAD] = _dot(oh, wv_ref[hd]).reshape(n_b, tp, V_HEAD)


def _v_up(o, w_vup):
    n_b, H, tp, _ = o.shape
    return pl.pallas_call(
        _v_up_kernel,
        out_shape=jax.ShapeDtypeStruct((n_b, tp, MLA_WIDTH), F32),
        compiler_params=pltpu.CompilerParams(vmem_limit_bytes=VMEM_LIMIT),
        name="v_up_sample",
    )(o, w_vup)


def _b_post_kernel(x_ref, tok_ref, z_ref, mk_ref, mv_ref, wout_ref, gf_ref, xo_ref, mix_sc, *, final):
    _mix_out(x_ref, tok_ref, z_ref, mk_ref, mv_ref, wout_ref, gf_ref, xo_ref, mix_sc, final)


def _mix_out(x_ref, tok_ref, z_ref, mk_ref, mv_ref, wout_ref, gf_ref, xo_ref, mix_sc, final):
    nb, tq, _ = x_ref.shape
    m = nb * tq
    for g in range(4):
        c0, c1 = g * 256, (g + 1) * 256
        mix_sc[:, c0:c1] = (tok_ref[:, :, c0:c1] * _silu(z_ref[:, :, c0:c1])).reshape(m, 256).astype(BF16)
    g0 = MLA_WIDTH + MEM_WIDTH
    mem_o = _mem_attend(z_ref[:, :, MLA_WIDTH:g0].reshape(m, MEM_WIDTH), mk_ref, mv_ref, nb, tq)
    for hd in range(MEM_HEADS):
        c0, c1 = hd * MEM_HEAD_DIM, (hd + 1) * MEM_HEAD_DIM
        gate = z_ref[:, :, g0 + c0:g0 + c1].reshape(m, MEM_HEAD_DIM)
        mix_sc[:, POOL_WIDTH + c0:POOL_WIDTH + c1] =(mem_o[hd] * _silu(gate)).astype(BF16)
    acc = x_ref[...].reshape(m, D_MODEL) + _dot(mix_sc[...], wout_ref[...])
    if final:
        acc = _rms(acc, gf_ref[...])
    xo_ref[...] = acc.reshape(nb, tq, D_MODEL)


def _b_post(x, tok, zrest, mem_k, mem_v, w_out, g_final, *, layer, nb, tq, final, tag):
    B, T, _ = x.shape
    kern = functools.partial(_b_post_kernel, final=final)
    return pl.pallas_call(
        kern,
        grid=(B // nb, T // tq),
        in_specs=[
            pl.BlockSpec((nb, tq, D_MODEL), lambda b, t: (b, t, 0)),
            pl.BlockSpec((nb, tq, MLA_WIDTH), lambda b, t: (b, t, 0)),
            pl.BlockSpec((nb, tq, Z_REST), lambda b, t: (b, t, 0)),
            pl.BlockSpec((1, nb, MEM_ROWS, MEM_HEAD_DIM), lambda b, t: (layer, b, 0, 0)),
            pl.BlockSpec((1, nb, MEM_ROWS, MEM_HEAD_DIM), lambda b, t: (layer, b, 0, 0)),
            pl.BlockSpec((None, OUT_W, D_MODEL), lambda b, t: (layer, 0, 0)),
            pl.BlockSpec((1, D_MODEL), lambda b, t: (0, 0)),
        ],
        out_specs=pl.BlockSpec((nb, tq, D_MODEL), lambda b, t: (b, t, 0)),
        out_shape=jax.ShapeDtypeStruct((B, T, D_MODEL), F32),
        scratch_shapes=[pltpu.VMEM((nb * tq, OUT_W), BF16)],
        compiler_params=_cparams(2),
        name=f"b_post{layer}_{tag}",
    )(x, tok, zrest, mem_k, mem_v, w_out, g_final)


def _rope_tables(pos):
    half = QK_ROPE // 2
    inv = ROPE_THETA ** (-jnp.arange(half, dtype=F32) / half)
    ang = pos.astype(F32)[:, None] * inv[None, :]
    return jnp.tile(jnp.cos(ang), (1, LANES // half)), jnp.tile(jnp.sin(ang), (1, LANES // half))


def kernel(x_prompt, x_sample, state_pool, cache_ckv, cache_krope, cache_mem_k, cache_mem_v, page_table, mem_prompt, g_norm, w_in_a, w_pool_grp, pool_scale, w_in_b, g_q_latent, w_q_up, g_kv_in, w_kv_down, g_kv_latent, w_k_up, w_v_up, g_mem, w_mem_k, w_mem_v, w_out, g_final):
    B, T, _ = x_prompt.shape
    SB, ST, _ = x_sample.shape
    TP = SUBLANES
    past = page_table.shape[1] * cache_ckv.shape[1]

    w_in_a16 = w_in_a.astype(BF16)
    w_grp16 = w_pool_grp.astype(BF16)
    w_out16 = w_out.astype(BF16)
    w_inb16 = w_in_b.astype(BF16)
    wq = w_q_up.reshape(N_A, Q_RANK, MLA_HEADS, QK_NOPE + QK_ROPE)
    w_qn16 = wq[..., :QK_NOPE].reshape(N_A, Q_RANK, MLA_HEADS * QK_NOPE).astype(BF16)
    w_qr16 = wq[..., QK_NOPE:].reshape(N_A, Q_RANK, MLA_HEADS * QK_ROPE).astype(BF16)
    w_kup16 = jnp.transpose(w_k_up, (1, 2, 0)).astype(BF16)
    w_vup16 = jnp.transpose(w_v_up, (1, 0, 2)).astype(BF16)
    w_kv16 = jnp.pad(w_kv_down, ((0, 0), (0, KCAT - KV_RANK - QK_ROPE))).astype(BF16)
    w_mk16 = w_mem_k.astype(BF16)
    w_mv16 = w_mem_v.astype(BF16)
    g_kv_in2 = g_kv_in[None]
    g_lat2 = g_kv_latent[None]
    g_final2 = g_final[None]

    w_kupt16 = w_kup16.reshape(MLA_HEADS * QK_NOPE, KV_RANK)
    w_vall16 = w_v_up.reshape(KV_RANK, MLA_WIDTH).astype(BF16)

    def trunk(x, prev, mem_k, mem_v, pos, *, tag, nb, tq_a, tq, tq_b, heads, n_valid, attend):
        pools = []
        for l in range(N_A):
            x, pool = _a_layer(x, prev[l], g_norm[l][None], w_in_a16, w_grp16, pool_scale[l][None],
                               mem_k, mem_v, w_out16, layer=l, nb=nb, tq=tq_a, n_valid=n_valid, pos0=pos)
            pools.append(pool[:, 1:])
        cos, sin = _rope_tables(pos + jnp.arange(x.shape[1], dtype=jnp.int32))
        ckv, krope, *keys = _kv_latent(x, g_kv_in2, w_kv16, g_lat2, cos, sin, w_kupt16, w_vall16,
                                       nb=nb, tq=tq, tk=min(tq, ATT_TILE), heads=heads)
        for j in range(DEPTH - N_A):
            l = N_A + j
            q, zrest = _b_pre(x, g_norm[l][None], w_inb16, g_q_latent[j][None],
                              w_qn16, w_qr16, w_kup16, cos, sin, j=j, nb=nb, tq=tq_b, absorb=not heads,
                              tag=f"{tag}{l}")
            x = attend(x, q, zrest, *keys, layer=l, final=(l == DEPTH - 1))
        if heads:
            krope = jnp.transpose(krope, (0, 2, 1))
        return x, jnp.stack(pools, axis=0), ckv, krope

    mem_k_p, mem_v_p = _mem_project(mem_prompt, g_mem, w_mk16, w_mv16)
    prev_p = jnp.zeros((N_A, B, HALO, D_MODEL), F32)

    def attend_prompt(x, q, zrest, knt, krt, v, *, layer, final):
        return _flash_post(q, knt, krt, v, x, zrest, mem_k_p, mem_v_p, w_out16, g_final2,
                           layer=layer, tq=ATT_TILE, final=final)

    y_p, pool_p, ckv_p, krope_p = trunk(
        x_prompt, prev_p, mem_k_p, mem_v_p, 0, tag="p", nb=1, tq_a=512, tq=512, tq_b=512, heads=True, n_valid=512,
        attend=attend_prompt)

    xs = jnp.pad(x_sample, ((0, 0), (0, TP - ST), (0, 0)))
    prev_s = jnp.pad(state_pool, ((0, 0), (0, 0), (HALO - POOL_BUF, 0), (0, 0)))
    mem_k_s = cache_mem_k.reshape(DEPTH, SB, MEM_ROWS, MEM_HEAD_DIM)
    mem_v_s = cache_mem_v.reshape(DEPTH, SB, MEM_ROWS, MEM_HEAD_DIM)
    cache_krt = jnp.transpose(cache_krope, (0, 2, 1))

    def attend_sample(x, q, zrest, kcat, *, layer, final):
        qf = q.reshape(SB, MLA_HEADS * TP, KCAT)
        knew = jnp.pad(kcat, ((0, 0), (0, LANES - TP), (0, 0)))
        o = _paged_attention(page_table, qf, knew, cache_ckv, cache_krt, cp=32, tp=TP, n_new=ST)
        tok = _v_up(o.reshape(SB, MLA_HEADS, TP, KV_RANK), w_vup16)
        return _b_post(x, tok, zrest, mem_k_s, mem_v_s, w_out16, g_final2, layer=layer, nb=8, tq=TP,
                       final=final, tag="s")

    y_s, pool_s, ckv_s, krope_s = trunk(
        xs, prev_s, mem_k_s, mem_v_s, past, tag="s", nb=8, tq_a=TP, tq=TP, tq_b=TP, heads=False, n_valid=ST,
        attend=attend_sample)

    mem_shape = (DEPTH, B, MEM_TOKENS, MEM_HEADS, MEM_HEAD_DIM)
    return (y_p, y_s[:, :ST], pool_p, pool_s, ckv_p, krope_p, ckv_s[:, :ST], krope_s[:, :ST],
            mem_k_p.reshape(mem_shape), mem_v_p.reshape(mem_shape))
```

```python
import functools

import jax
import jax.numpy as jnp
from jax import lax
from jax.experimental import pallas as pl
from jax.experimental.pallas import tpu as pltpu

F32 = jnp.float32
BF16 = jnp.bfloat16

D_MODEL = 1024
DEPTH = 4
N_A = 2
POOL_WINDOWS = (2, 4, 8, 16)
POOL_GROUP = 256
POOL_BUF = 15
MLA_HEADS = 8
QK_NOPE = 128
QK_ROPE = 64
V_HEAD = 128
KV_RANK = 256
Q_RANK = 384
MLA_WIDTH = MLA_HEADS * V_HEAD
MLA_SCALE = (QK_NOPE + QK_ROPE) ** -0.5
ROPE_THETA = 10000.0
MEM_TOKENS = 256
MEM_HEADS = 4
MEM_HEAD_DIM = 128
MEM_WIDTH = MEM_HEADS * MEM_HEAD_DIM
MEM_SCALE = MEM_HEAD_DIM ** -0.5
POOL_WIDTH = D_MODEL
IN_A = 2 * POOL_WIDTH + 2 * MEM_WIDTH
Z_REST = MLA_WIDTH + 2 * MEM_WIDTH
OUT_W = POOL_WIDTH + MEM_WIDTH
EPS = 1e-6
NEG = -1e30
LOG2E = 1.4426950408889634

KCAT = 384
SUBLANES = 8
LANES = 128
HALO = 16
E_OFF = SUBLANES + HALO
VMEM_LIMIT = 56 * 1024 * 1024
MEM_ROWS = MEM_TOKENS * MEM_HEADS
ATT_TILE = 256
N_RAW = 4


def _cparams(n_axes):
    return pltpu.CompilerParams(dimension_semantics=("arbitrary",) * n_axes,
                                vmem_limit_bytes=VMEM_LIMIT)


def _rms(x, g):
    return x * lax.rsqrt(jnp.mean(x * x, axis=-1, keepdims=True) + EPS) * g


def _dot(a, b):
    return jnp.dot(a, b, preferred_element_type=F32)


def _dot_nt(a, b):
    return lax.dot_general(a, b, (((1,), (1,)), ((), ())), preferred_element_type=F32)


def _silu(x):
    return x * jax.nn.sigmoid(x)


def _rope_lanes(a, cos, sin):
    half = QK_ROPE // 2
    lane = lax.broadcasted_iota(jnp.int32, a.shape, 1)
    up = pltpu.roll(a, half, 1)
    dn = pltpu.roll(a, LANES - half, 1)
    rot = jnp.where((lane & (QK_ROPE - 1)) < half, -dn, up)
    return a * cos + rot * sin


def _rep(x, n):
    return x if n == 1 else jnp.concatenate([x] * n, axis=1)


def _tile_rows(x, n):
    return x if n == 1 else jnp.concatenate([x] * n, axis=0)


def _mem_attend(qm, mk_ref, mv_ref, nb, tq):
    rows, keys = nb * tq, nb * MEM_TOKENS
    if nb > 1:
        row_seq = lax.broadcasted_iota(jnp.int32, (rows, keys), 0) // tq
        col_seq = lax.broadcasted_iota(jnp.int32, (rows, keys), 1) // MEM_TOKENS
        keep = row_seq == col_seq
    outs = []
    for h in range(MEM_HEADS):
        c0, c1 = h * MEM_HEAD_DIM, (h + 1) * MEM_HEAD_DIM
        q = (qm[:, c0:c1] * (MEM_SCALE * LOG2E)).astype(BF16)
        k = mk_ref[0, :, pl.ds(h, MEM_TOKENS, stride=MEM_HEADS), :].reshape(keys, MEM_HEAD_DIM).astype(BF16)
        v = mv_ref[0, :, pl.ds(h, MEM_TOKENS, stride=MEM_HEADS), :].reshape(keys, MEM_HEAD_DIM).astype(BF16)
        s = _dot_nt(q, k)
        if nb > 1:
            s = jnp.where(keep, s, NEG)
        m = jnp.max(s, axis=-1, keepdims=True)
        p = jnp.exp2(s - m)
        l = jnp.sum(p, axis=-1, keepdims=True)
        outs.append(_dot(p.astype(BF16), v) * (1.0 / l))
    return outs


def _mem_proj_kernel(mem_ref, g_ref, wk_ref, wv_ref, mk_ref, mv_ref):
    x = mem_ref[0]
    xn = x * lax.rsqrt(jnp.mean(x * x, axis=-1, keepdims=True) + EPS)
    for l in range(DEPTH):
        h = xn * g_ref[l:l + 1, :]
        mk = _dot(h, wk_ref[l])
        mv = _dot(h, wv_ref[l])
        for hd in range(MEM_HEADS):
            c0, c1 = hd * MEM_HEAD_DIM, (hd + 1) * MEM_HEAD_DIM
            mk_ref[l, 0, pl.ds(hd, MEM_TOKENS, stride=MEM_HEADS), :] = mk[:, c0:c1]
            mv_ref[l, 0, pl.ds(hd, MEM_TOKENS, stride=MEM_HEADS), :] = mv[:, c0:c1]


def _mem_project(mem, g_mem, wk, wv):
    B, M, _ = mem.shape
    out = jax.ShapeDtypeStruct((DEPTH, B, MEM_ROWS, MEM_HEAD_DIM), F32)
    return pl.pallas_call(
        _mem_proj_kernel,
        grid=(B,),
        in_specs=[
            pl.BlockSpec((1, M, D_MODEL), lambda b: (b, 0, 0)),
            pl.BlockSpec((DEPTH, D_MODEL), lambda b: (0, 0)),
            pl.BlockSpec((DEPTH, D_MODEL, MEM_WIDTH), lambda b: (0, 0, 0)),
            pl.BlockSpec((DEPTH, D_MODEL, MEM_WIDTH), lambda b: (0, 0, 0)),
        ],
        out_specs=[
            pl.BlockSpec((DEPTH, 1, MEM_ROWS, MEM_HEAD_DIM), lambda b: (0, b, 0, 0)),
            pl.BlockSpec((DEPTH, 1, MEM_ROWS, MEM_HEAD_DIM), lambda b: (0, b, 0, 0)),
        ],
        out_shape=[out, out],
        compiler_params=_cparams(1),
        name="mem_project",
    )(mem, g_mem, wk, wv)


def _a_layer_kernel(x_ref, prev_ref, g_ref, win_ref, wgrp_ref, pscale_ref, mk_ref, mv_ref, wout_ref,
                    xo_ref, pool_ref, e_sc, sa_sc, sb_sc, mix_sc, *, nb, tq, n_t, n_valid, pos0):
    t = pl.program_id(1)
    rows = HALO + tq
    m = nb * tq

    @pl.when(t == 0)
    def _():
        zeros = jnp.zeros((nb, SUBLANES, D_MODEL), F32)
        e_sc[:, 0:SUBLANES, :] = zeros
        sa_sc[:, 0:SUBLANES, :] = zeros
        sb_sc[:, 0:SUBLANES, :] = zeros
        e_sc[:, SUBLANES:E_OFF, :] = prev_ref[...]

    x = x_ref[...].reshape(m, D_MODEL)
    h = _rms(x, g_ref[...])
    e_sc[:, E_OFF:E_OFF + tq, :] = _dot(h, win_ref[:, 0:POOL_WIDTH]).reshape(nb, tq, POOL_WIDTH)

    lo = SUBLANES
    sa_sc[:, lo:lo + rows, :] = e_sc[:, lo:lo + rows, :] + e_sc[:, lo - 1:lo - 1 + rows, :]
    sb_sc[:, lo:lo + rows, 256:1024] = (sa_sc[:, lo:lo + rows, 256:1024]
                                        + sa_sc[:, lo - 2:lo - 2 + rows, 256:1024])
    sa_sc[:, lo:lo + rows, 512:1024] = (sb_sc[:, lo:lo + rows, 512:1024]
                                        + sb_sc[:, lo - 4:lo - 4 + rows, 512:1024])
    s16 = sa_sc[:, E_OFF:E_OFF + tq, 768:1024] + sa_sc[:, E_OFF - 8:E_OFF - 8 + tq, 768:1024]
    sums = (sa_sc[:, E_OFF:E_OFF + tq, 0:256], sb_sc[:, E_OFF:E_OFF + tq, 256:512],
            sa_sc[:, E_OFF:E_OFF + tq, 512:768], s16)

    pos = pos0 + t * tq + lax.broadcasted_iota(jnp.int32, (1, tq, 1), 1)
    for g, w in enumerate(POOL_WINDOWS):
        c0, c1 = g * POOL_GROUP, (g + 1) * POOL_GROUP
        inv_cnt = 1.0 / jnp.minimum(pos + 1, w).astype(F32)
        pooled = (sums[g] * inv_cnt - e_sc[:, E_OFF:E_OFF + tq, c0:c1]).reshape(m, POOL_GROUP)
        mixed = _dot(pooled, wgrp_ref[g]) * pscale_ref[:, c0:c1]
        gate = _dot(h, win_ref[:, POOL_WIDTH + c0:POOL_WIDTH + c1])
        mix_sc[:, c0:c1] = mixed * _silu(gate)

    qm = _dot(h, win_ref[:, 2 * POOL_WIDTH:2 * POOL_WIDTH + MEM_WIDTH])
    gate_m = _dot(h, win_ref[:, 2 * POOL_WIDTH + MEM_WIDTH:IN_A])
    mem_o = _mem_attend(qm, mk_ref, mv_ref, nb, tq)
    for hd in range(MEM_HEADS):
        c0, c1 = hd * MEM_HEAD_DIM, (hd + 1) * MEM_HEAD_DIM
        mix_sc[:, POOL_WIDTH + c0:POOL_WIDTH + c1] = mem_o[hd] * _silu(gate_m[:, c0:c1])
    xo_ref[...] = (x + _dot(mix_sc[...], wout_ref[...])).reshape(nb, tq, D_MODEL)

    @pl.when(t == n_t - 1)
    def _():
        pool_ref[...] = e_sc[:, E_OFF + n_valid - HALO:E_OFF + n_valid, :]

    if n_t > 1:
        @pl.when(t < n_t - 1)
        def _():
            e_sc[:, SUBLANES:E_OFF, :] = e_sc[:, SUBLANES + tq:E_OFF + tq, :]


def _a_layer(x, prev, g, w_in, w_grp, pscale, mem_k, mem_v, w_out, *, layer, nb, tq, n_valid, pos0):
    B, T, _ = x.shape
    n_t = T // tq
    rows = E_OFF + tq
    kern = functools.partial(_a_layer_kernel, nb=nb, tq=tq, n_t=n_t, n_valid=n_valid, pos0=pos0)
    return pl.pallas_call(
        kern,
        grid=(B // nb, n_t),
        in_specs=[
            pl.BlockSpec((nb, tq, D_MODEL), lambda b, t: (b, t, 0)),
            pl.BlockSpec((nb, HALO, D_MODEL), lambda b, t: (b, 0, 0)),
            pl.BlockSpec((1, D_MODEL), lambda b, t: (0, 0)),
            pl.BlockSpec((None, D_MODEL, IN_A), lambda b, t: (layer, 0, 0), pipeline_mode=pl.Buffered(1)),
            pl.BlockSpec((None, 4, POOL_GROUP, POOL_GROUP), lambda b, t: (layer, 0, 0, 0),
                         pipeline_mode=pl.Buffered(1)),
            pl.BlockSpec((1, D_MODEL), lambda b, t: (0, 0)),
            pl.BlockSpec((1, nb, MEM_ROWS, MEM_HEAD_DIM), lambda b, t: (layer, b, 0, 0)),
            pl.BlockSpec((1, nb, MEM_ROWS, MEM_HEAD_DIM), lambda b, t: (layer, b, 0, 0)),
            pl.BlockSpec((None, OUT_W, D_MODEL), lambda b, t: (layer, 0, 0), pipeline_mode=pl.Buffered(1)),
        ],
        out_specs=[
            pl.BlockSpec((nb, tq, D_MODEL), lambda b, t: (b, t, 0)),
            pl.BlockSpec((nb, HALO, D_MODEL), lambda b, t: (b, 0, 0)),
        ],
        out_shape=[jax.ShapeDtypeStruct((B, T, D_MODEL), F32),
                   jax.ShapeDtypeStruct((B, HALO, D_MODEL), F32)],
        scratch_shapes=[pltpu.VMEM((nb, rows, D_MODEL), F32)] * 3 + [pltpu.VMEM((nb * tq, OUT_W), F32)],
        compiler_params=_cparams(2),
        name=f"a_layer{layer}_{B}",
    )(x, prev, g, w_in, w_grp, pscale, mem_k, mem_v, w_out)


def _kv_latent_kernel(x_ref, gin_ref, wkv_ref, glat_ref, cos_ref, sin_ref, wkupt_ref, wvup_ref,
                      ckv_ref, kr_ref, *out_refs, heads):
    nb, tq, _ = x_ref.shape
    m = nb * tq
    h = _rms(x_ref[...].reshape(m, D_MODEL), gin_ref[...]).astype(BF16)
    kv = _dot(h, wkv_ref[...])
    ckv = _rms(kv[:, 0:KV_RANK], glat_ref[...])
    kr = _rope_lanes(kv[:, KV_RANK:KCAT], _tile_rows(cos_ref[...], nb), _tile_rows(sin_ref[...], nb))
    ckv_ref[...] = ckv.reshape(nb, tq, KV_RANK)
    if not heads:
        kcat_ref, = out_refs
        kr_ref[...] = kr[:, 0:QK_ROPE].reshape(nb, tq, QK_ROPE)
        kcat_ref[:, :, 0:KV_RANK] = ckv.reshape(nb, tq, KV_RANK).astype(BF16)
        kcat_ref[:, :, KV_RANK:KCAT] = kr.reshape(nb, tq, LANES).astype(BF16)
    else:
        knt_ref, krt_ref, v_ref = out_refs
        tk = knt_ref.shape[-1]
        v_ref[0] = _dot(ckv.astype(BF16), wvup_ref[...]).astype(BF16)
        for sub in range(tq // tk):
            r0 = sub * tk
            ckv_t = ckv[r0:r0 + tk, :].T.astype(BF16)
            knt_ref[0, sub] = _dot(wkupt_ref[...], ckv_t).astype(BF16)
            kr_t = kr[r0:r0 + tk, :].T
            kr_ref[0, :, r0:r0 + tk] = kr_t[0:QK_ROPE, :]
            krt_ref[0, sub] = kr_t.astype(BF16)


def _kv_latent(x, g_in, w_kv, g_lat, cos, sin, w_kupt, w_vup, *, nb, tq, tk, heads):
    B, T, _ = x.shape
    assert nb == 1 or not heads
    n_sub = tq // tk
    out_specs = [
        pl.BlockSpec((nb, tq, KV_RANK), lambda b, t: (b, t, 0)),
        pl.BlockSpec((1, QK_ROPE, tq), lambda b, t: (b, 0, t)) if heads
        else pl.BlockSpec((nb, tq, QK_ROPE), lambda b, t: (b, t, 0)),
    ]
    out_shape = [jax.ShapeDtypeStruct((B, T, KV_RANK), F32),
                 jax.ShapeDtypeStruct((B, QK_ROPE, T) if heads else (B, T, QK_ROPE), F32)]
    if heads:
        out_specs += [pl.BlockSpec((1, n_sub, MLA_HEADS * QK_NOPE, tk), lambda b, t: (b, t, 0, 0)),
                      pl.BlockSpec((1, n_sub, LANES, tk), lambda b, t: (b, t, 0, 0)),
                      pl.BlockSpec((1, tq, MLA_WIDTH), lambda b, t: (b, t, 0))]
        out_shape += [jax.ShapeDtypeStruct((B, T // tk, MLA_HEADS * QK_NOPE, tk), BF16),
                      jax.ShapeDtypeStruct((B, T // tk, LANES, tk), BF16),
                      jax.ShapeDtypeStruct((B, T, MLA_WIDTH), BF16)]
    else:
        out_specs.append(pl.BlockSpec((nb, tq, KCAT), lambda b, t: (b, t, 0)))
        out_shape.append(jax.ShapeDtypeStruct((B, T, KCAT), BF16))
    return pl.pallas_call(
        functools.partial(_kv_latent_kernel, heads=heads),
        grid=(B // nb, T // tq),
        in_specs=[
            pl.BlockSpec((nb, tq, D_MODEL), lambda b, t: (b, t, 0)),
            pl.BlockSpec((1, D_MODEL), lambda b, t: (0, 0)),
            pl.BlockSpec((D_MODEL, KCAT), lambda b, t: (0, 0)),
            pl.BlockSpec((1, KV_RANK), lambda b, t: (0, 0)),
            pl.BlockSpec((tq, LANES), lambda b, t: (t, 0)),
            pl.BlockSpec((tq, LANES), lambda b, t: (t, 0)),
            pl.BlockSpec((MLA_HEADS * QK_NOPE, KV_RANK), lambda b, t: (0, 0)),
            pl.BlockSpec((KV_RANK, MLA_WIDTH), lambda b, t: (0, 0)),
        ],
        out_specs=out_specs,
        out_shape=out_shape,
        compiler_params=_cparams(2),
        name=f"kv_latent_{B}",
    )(x, g_in, w_kv, g_lat, cos, sin, w_kupt, w_vup)


def _b_pre_kernel(x_ref, g_ref, winb_ref, gq_ref, wqn_ref, wqr_ref, wkup_ref, cos_ref, sin_ref,
                  q_ref, zrest_ref, *, absorb):
    nb, tq, _ = x_ref.shape
    m = nb * tq
    n0 = KV_RANK if absorb else QK_NOPE
    h = _rms(x_ref[...].reshape(m, D_MODEL), g_ref[...])
    zrest_ref[...] = _dot(h, winb_ref[:, Q_RANK:]).reshape(nb, tq, Z_REST)
    cn = _rms(_dot(h, winb_ref[:, 0:Q_RANK]), gq_ref[...]).astype(BF16)
    qn = _dot(cn, wqn_ref[...])
    qr = _dot(cn, wqr_ref[...])
    for hd in range(MLA_HEADS):
        ql = qn[:, hd * QK_NOPE:(hd + 1) * QK_NOPE]
        if absorb:
            ql = _dot(ql.astype(BF16), wkup_ref[hd])
        q_ref[:, hd, :, 0:n0] = (ql * (MLA_SCALE * LOG2E)).reshape(nb, tq, n0).astype(BF16)
    cos = _tile_rows(cos_ref[...], nb)
    sin = _tile_rows(sin_ref[...], nb)
    lane = lax.broadcasted_iota(jnp.int32, cos.shape, 1)
    for j in range(MLA_HEADS // 2):
        rr = _rope_lanes(qr[:, j * LANES:(j + 1) * LANES], cos, sin) * (MLA_SCALE * LOG2E)
        even = jnp.where(lane < QK_ROPE, rr, 0.0)
        odd = jnp.where(lane < QK_ROPE, pltpu.roll(rr, 64, 1), 0.0)
        q_ref[:, 2 * j, :, n0:n0 + LANES] = even.reshape(nb, tq, LANES).astype(BF16)
        q_ref[:, 2 * j + 1, :, n0:n0 + LANES] = odd.reshape(nb, tq, LANES).astype(BF16)


def _b_pre(x, g, w_inb, g_q, w_qn, w_qr, w_kup, cos, sin, *, j, nb, tq, absorb, tag):
    B, T, _ = x.shape
    const2 = lambda b, t: (0, 0)
    qw = (KV_RANK if absorb else QK_NOPE) + LANES
    return pl.pallas_call(
        functools.partial(_b_pre_kernel, absorb=absorb),
        grid=(B // nb, T // tq),
        in_specs=[
            pl.BlockSpec((nb, tq, D_MODEL), lambda b, t: (b, t, 0)),
            pl.BlockSpec((1, D_MODEL), const2),
            pl.BlockSpec((None, D_MODEL, Q_RANK + Z_REST), lambda b, t: (j, 0, 0), pipeline_mode=pl.Buffered(1)),
            pl.BlockSpec((1, Q_RANK), const2),
            pl.BlockSpec((None, Q_RANK, MLA_HEADS * QK_NOPE), lambda b, t: (j, 0, 0)),
            pl.BlockSpec((None, Q_RANK, MLA_HEADS * QK_ROPE), lambda b, t: (j, 0, 0)),
            pl.BlockSpec((MLA_HEADS, QK_NOPE, KV_RANK), lambda b, t: (0, 0, 0)),
            pl.BlockSpec((tq, LANES), lambda b, t: (t, 0)),
            pl.BlockSpec((tq, LANES), lambda b, t: (t, 0)),
        ],
        out_specs=[
            pl.BlockSpec((nb, MLA_HEADS, tq, qw), lambda b, t: (b, 0, t, 0)),
            pl.BlockSpec((nb, tq, Z_REST), lambda b, t: (b, t, 0)),
        ],
        out_shape=[jax.ShapeDtypeStruct((B, MLA_HEADS, T, qw), BF16),
                   jax.ShapeDtypeStruct((B, T, Z_REST), F32)],
        compiler_params=_cparams(2),
        name=f"b_pre_{tag}",
    )(x, g, w_inb, g_q, w_qn, w_qr, w_kup, cos, sin)


def _flash_kernel(q_ref, knt_ref, krt_ref, v_ref, x_ref, z_ref, mk_ref, mv_ref, wout_ref, gf_ref, xo_ref,
                  m_sc, l_sc, acc_sc, tok_sc, mix_sc, *, tq, tk, final):
    i = pl.program_id(1)
    m_sc[...] = jnp.full(m_sc.shape, NEG, F32)
    l_sc[...] = jnp.zeros(l_sc.shape, F32)
    acc_sc[...] = jnp.zeros(acc_sc.shape, F32)

    def block(j, masked):
        krt = krt_ref[0, j]
        rows = pl.ds(pl.multiple_of(j * tk, tk), tk)
        if masked:
            keep = (lax.broadcasted_iota(jnp.int32, (tq, tk), 1)
                    <= lax.broadcasted_iota(jnp.int32, (tq, tk), 0))

        def scores(hd):
            kt = jnp.concatenate([knt_ref[0, j, hd * QK_NOPE:(hd + 1) * QK_NOPE, :], krt], axis=0)
            return _dot(q_ref[0, hd], kt)

        s_next = scores(0)
        for hd in range(MLA_HEADS):
            s = s_next
            if hd + 1 < MLA_HEADS:
                s_next = scores(hd + 1)
            if masked:
                s = jnp.where(keep, s, NEG)
            m_prev = m_sc[hd]
            m_new = jnp.maximum(m_prev, jnp.max(s, axis=-1, keepdims=True))
            alpha = jnp.exp2(m_prev - m_new)
            p = jnp.exp2(s - _rep(m_new, tk // LANES))
            p_lanes = p[:, 0:LANES]
            for c in range(1, tk // LANES):
                p_lanes = p_lanes + p[:, c * LANES:(c + 1) * LANES]
            l_sc[hd] = alpha * l_sc[hd] + p_lanes
            v = v_ref[0, rows, hd * V_HEAD:(hd + 1) * V_HEAD]
            acc_sc[hd] = alpha * acc_sc[hd] + _dot(p.astype(BF16), v)
            m_sc[hd] = m_new

    def body(jj, carry):
        block(2 * jj, False)
        block(2 * jj + 1, False)
        return carry

    lax.fori_loop(0, i // 2, body, 0)

    @pl.when(i % 2 == 1)
    def _():
        block(i - 1, False)

    block(i, True)

    for hd in range(MLA_HEADS):
        l = jnp.sum(l_sc[hd], axis=-1, keepdims=True)
        tok_sc[0, :, hd * V_HEAD:(hd + 1) * V_HEAD] = acc_sc[hd] * (1.0 / l)

    _mix_out(x_ref, tok_sc, z_ref, mk_ref, mv_ref, wout_ref, gf_ref, xo_ref, mix_sc, final)


def _flash_post(q, knt, krt, v, x, zrest, mem_k, mem_v, w_out, g_final, *, layer, tq, final):
    B, H, T, qw = q.shape
    tk = knt.shape[-1]
    assert tq == tk and qw == QK_NOPE + LANES
    kern = functools.partial(_flash_kernel, tq=tq, tk=tk, final=final)
    return pl.pallas_call(
        kern,
        grid=(B, T // tq),
        in_specs=[
            pl.BlockSpec((1, H, tq, qw), lambda b, i: (b, 0, i, 0)),
            pl.BlockSpec((1, T // tk, H * QK_NOPE, tk), lambda b, i: (b, 0, 0, 0)),
            pl.BlockSpec((1, T // tk, LANES, tk), lambda b, i: (b, 0, 0, 0)),
            pl.BlockSpec((1, T, MLA_WIDTH), lambda b, i: (b, 0, 0)),
            pl.BlockSpec((1, tq, D_MODEL), lambda b, i: (b, i, 0)),
            pl.BlockSpec((1, tq, Z_REST), lambda b, i: (b, i, 0)),
            pl.BlockSpec((1, 1, MEM_ROWS, MEM_HEAD_DIM), lambda b, i: (layer, b, 0, 0)),
            pl.BlockSpec((1, 1, MEM_ROWS, MEM_HEAD_DIM), lambda b, i: (layer, b, 0, 0)),
            pl.BlockSpec((None, OUT_W, D_MODEL), lambda b, i: (layer, 0, 0), pipeline_mode=pl.Buffered(1)),
            pl.BlockSpec((1, D_MODEL), lambda b, i: (0, 0)),
        ],
        out_specs=pl.BlockSpec((1, tq, D_MODEL), lambda b, i: (b, i, 0)),
        out_shape=jax.ShapeDtypeStruct((B, T, D_MODEL), F32),
        scratch_shapes=[pltpu.VMEM((H, tq, LANES), F32), pltpu.VMEM((H, tq, LANES), F32),
                        pltpu.VMEM((H, tq, V_HEAD), F32), pltpu.VMEM((1, tq, MLA_WIDTH), F32),
                        pltpu.VMEM((tq, OUT_W), F32)],
        compiler_params=_cparams(2),
        name=f"flash_post{layer}",
    )(q, knt, krt, v, x, zrest, mem_k, mem_v, w_out, g_final)


def _paged_kernel(pt_ref, q_ref, knew_ref, ckv_hbm, krt_hbm, o_ref,
                  ckv_buf, krt_buf, kbf_sc, s_sc, sem, m_sc, l_sc, acc_sc,
                  *, n_b, n_ch, cp, page, tp, n_new):
    total = n_b * n_ch
    assert total >= 2

    def copies(bc):
        slot = bc % N_RAW
        b = bc // n_ch
        c = bc % n_ch
        out = []
        for p in range(cp):
            pg = pt_ref[b, c * cp + p]
            out.append(pltpu.make_async_copy(ckv_hbm.at[pg], ckv_buf.at[slot, pl.ds(p * page, page)],
                                             sem.at[0, slot]))
            out.append(pltpu.make_async_copy(krt_hbm.at[pg], krt_buf.at[slot, p], sem.at[1, slot]))
        return out

    def score(bc, w):
        slot = bc % N_RAW
        b = bc // n_ch
        for cpy in copies(bc):
            cpy.wait()
        q_lat = q_ref[b, :, 0:KV_RANK]
        q_rope = q_ref[b, :, KV_RANK:KV_RANK + QK_ROPE]
        kc = ckv_buf[slot].astype(BF16)
        krt = jnp.concatenate([krt_buf[slot, p] for p in range(cp)], axis=1).astype(BF16)
        kbf_sc[w] = kc
        s_sc[w] = _dot_nt(q_lat, kc) + _dot(q_rope, krt)

    def update(s, v_bf16, first):
        m_prev = jnp.where(first, NEG, m_sc[...])
        l_prev = jnp.where(first, 0.0, l_sc[...])
        acc_prev = jnp.where(first, 0.0, acc_sc[...])
        m_new = jnp.maximum(m_prev, jnp.max(s, axis=-1, keepdims=True))
        alpha = jnp.exp2(m_prev - m_new)
        p = jnp.exp2(s - _rep(m_new, s.shape[1] // LANES))
        l_sc[...] = alpha * l_prev + jnp.sum(p, axis=-1, keepdims=True)
        acc_sc[...] = _rep(alpha, KV_RANK // LANES) * acc_prev + _dot(p.astype(BF16), v_bf16)
        m_sc[...] = m_new

    def attend(bc, w):
        b = bc // n_ch
        c = bc % n_ch
        update(s_sc[w], kbf_sc[w], c == 0)

        def finish():
            kn = knew_ref[b]
            sn = _dot_nt(q_ref[b], kn)
            tpos = lax.broadcasted_iota(jnp.int32, sn.shape, 0) & (tp - 1)
            col = lax.broadcasted_iota(jnp.int32, sn.shape, 1)
            sn = jnp.where((col <= tpos) & (col < n_new), sn, NEG)
            update(sn, kn[:, 0:KV_RANK], False)
            o_ref[b] = acc_sc[...] * _rep(1.0 / l_sc[...], KV_RANK // LANES)

        last = c == n_ch - 1
        if isinstance(last, bool):
            if last:
                finish()
        else:
            pl.when(last)(finish)

    m_sc[...] = jnp.full(m_sc.shape, NEG, F32)
    l_sc[...] = jnp.zeros(l_sc.shape, F32)
    acc_sc[...] = jnp.zeros(acc_sc.shape, F32)
    for bc in range(min(N_RAW, total)):
        for cpy in copies(bc):
            cpy.start()
    score(0, 0)

    def step(i, w):
        def prefetch():
            for cpy in copies(i + N_RAW):
                cpy.start()

        more = i + N_RAW < total
        if isinstance(more, bool):
            if more:
                prefetch()
        else:
            pl.when(more)(prefetch)
        if isinstance(i, int) and i + 1 >= total:
            attend(i, w)
            return
        score(i + 1, 1 - w)
        attend(i, w)

    def body(k, carry):
        step(2 * k, 0)
        step(2 * k + 1, 1)
        return carry

    n_pairs = (total - 2) // 2
    lax.fori_loop(0, n_pairs, body, 0)
    for i in range(2 * n_pairs, total):
        step(i, i % 2)


def _paged_attention(page_table, q, knew, cache_ckv, cache_krt, *, cp, tp, n_new):
    n_b, n_pages = page_table.shape
    page = cache_ckv.shape[1]
    rows = q.shape[1]
    n_ch = n_pages // cp
    kern = functools.partial(_paged_kernel, n_b=n_b, n_ch=n_ch, cp=cp, page=page, tp=tp, n_new=n_new)
    grid_spec = pltpu.PrefetchScalarGridSpec(
        num_scalar_prefetch=1,
        grid=(1,),
        in_specs=[
            pl.BlockSpec(q.shape, lambda i, pt: (0, 0, 0)),
            pl.BlockSpec(knew.shape, lambda i, pt: (0, 0, 0)),
            pl.BlockSpec(memory_space=pl.ANY),
            pl.BlockSpec(memory_space=pl.ANY),
        ],
        out_specs=pl.BlockSpec((n_b, rows, KV_RANK), lambda i, pt: (0, 0, 0)),
        scratch_shapes=[
            pltpu.VMEM((N_RAW, cp * page, KV_RANK), F32),
            pltpu.VMEM((N_RAW, cp, QK_ROPE, page), F32),
            pltpu.VMEM((2, cp * page, KV_RANK), BF16),
            pltpu.VMEM((2, rows, cp * page), F32),
            pltpu.SemaphoreType.DMA((2, N_RAW)),
            pltpu.VMEM((rows, LANES), F32),
            pltpu.VMEM((rows, LANES), F32),
            pltpu.VMEM((rows, KV_RANK), F32),
        ],
    )
    return pl.pallas_call(
        kern,
        grid_spec=grid_spec,
        out_shape=jax.ShapeDtypeStruct((n_b, rows, KV_RANK), F32),
        compiler_params=_cparams(1),
        name="paged_attention",
    )(page_table, q, knew, cache_ckv, cache_krt)


def _v_up_kernel(o_ref, wv_ref, tok_ref):
    n_b, _, tp, _ = o_ref.shape
    for hd in range(MLA_HEADS):
        oh = o_ref[:, hd].reshape(n_b * tp, KV_RANK).astype(BF16)
        tok_ref[:, :, hd * V_HEAD:(hd + 1) * V_HEAD] = _dot(oh, wv_ref[hd]).reshape(n_b, tp, V_HEAD)


def _v_up(o, w_vup):
    n_b, H, tp, _ = o.shape
    return pl.pallas_call(
        _v_up_kernel,
        out_shape=jax.ShapeDtypeStruct((n_b, tp, MLA_WIDTH), F32),
        compiler_params=pltpu.CompilerParams(vmem_limit_bytes=VMEM_LIMIT),
        name="v_up_sample",
    )(o, w_vup)


def _b_post_kernel(x_ref, tok_ref, z_ref, mk_ref, mv_ref, wout_ref, gf_ref, xo_ref, mix_sc, *, final):
    _mix_out(x_ref, tok_ref, z_ref, mk_ref, mv_ref, wout_ref, gf_ref, xo_ref, mix_sc, final)


def _mix_out(x_ref, tok_ref, z_ref, mk_ref, mv_ref, wout_ref, gf_ref, xo_ref, mix_sc, final):
    nb, tq, _ = x_ref.shape
    m = nb * tq
    for g in range(4):
        c0, c1 = g * 256, (g + 1) * 256
        mix_sc[:, c0:c1] = (tok_ref[:, :, c0:c1] * _silu(z_ref[:, :, c0:c1])).reshape(m, 256)
    g0 = MLA_WIDTH + MEM_WIDTH
    mem_o = _mem_attend(z_ref[:, :, MLA_WIDTH:g0].reshape(m, MEM_WIDTH), mk_ref, mv_ref, nb, tq)
    for hd in range(MEM_HEADS):
        c0, c1 = hd * MEM_HEAD_DIM, (hd + 1) * MEM_HEAD_DIM
        gate = z_ref[:, :, g0 + c0:g0 + c1].reshape(m, MEM_HEAD_DIM)
        mix_sc[:, POOL_WIDTH + c0:POOL_WIDTH + c1] = mem_o[hd] * _silu(gate)
    acc = x_ref[...].reshape(m, D_MODEL) + _dot(mix_sc[...], wout_ref[...])
    if final:
        acc = _rms(acc, gf_ref[...])
    xo_ref[...] = acc.reshape(nb, tq, D_MODEL)


def _b_post(x, tok, zrest, mem_k, mem_v, w_out, g_final, *, layer, nb, tq, final, tag):
    B, T, _ = x.shape
    kern = functools.partial(_b_post_kernel, final=final)
    return pl.pallas_call(
        kern,
        grid=(B // nb, T // tq),
        in_specs=[
            pl.BlockSpec((nb, tq, D_MODEL), lambda b, t: (b, t, 0)),
            pl.BlockSpec((nb, tq, MLA_WIDTH), lambda b, t: (b, t, 0)),
            pl.BlockSpec((nb, tq, Z_REST), lambda b, t: (b, t, 0)),
            pl.BlockSpec((1, nb, MEM_ROWS, MEM_HEAD_DIM), lambda b, t: (layer, b, 0, 0)),
            pl.BlockSpec((1, nb, MEM_ROWS, MEM_HEAD_DIM), lambda b, t: (layer, b, 0, 0)),
            pl.BlockSpec((None, OUT_W, D_MODEL), lambda b, t: (layer, 0, 0), pipeline_mode=pl.Buffered(1)),
            pl.BlockSpec((1, D_MODEL), lambda b, t: (0, 0)),
        ],
        out_specs=pl.BlockSpec((nb, tq, D_MODEL), lambda b, t: (b, t, 0)),
        out_shape=jax.ShapeDtypeStruct((B, T, D_MODEL), F32),
        scratch_shapes=[pltpu.VMEM((nb * tq, OUT_W), F32)],
        compiler_params=_cparams(2),
        name=f"b_post{layer}_{tag}",
    )(x, tok, zrest, mem_k, mem_v, w_out, g_final)


def _rope_tables(pos):
    half = QK_ROPE // 2
    inv = ROPE_THETA ** (-jnp.arange(half, dtype=F32) / half)
    ang = pos.astype(F32)[:, None] * inv[None, :]
    return jnp.tile(jnp.cos(ang), (1, LANES // half)), jnp.tile(jnp.sin(ang), (1, LANES // half))


def kernel(x_prompt, x_sample, state_pool, cache_ckv, cache_krope, cache_mem_k, cache_mem_v, page_table, mem_prompt, g_norm, w_in_a, w_pool_grp, pool_scale, w_in_b, g_q_latent, w_q_up, g_kv_in, w_kv_down, g_kv_latent, w_k_up, w_v_up, g_mem, w_mem_k, w_mem_v, w_out, g_final):
    B, T, _ = x_prompt.shape
    SB, ST, _ = x_sample.shape
    TP = SUBLANES
    past = page_table.shape[1] * cache_ckv.shape[1]

    w_in_a16, w_grp16, w_out16, w_inb16 = w_in_a, w_pool_grp, w_out, w_in_b
    wq = w_q_up.reshape(N_A, Q_RANK, MLA_HEADS, QK_NOPE + QK_ROPE)
    w_qn16 = wq[..., :QK_NOPE].reshape(N_A, Q_RANK, MLA_HEADS * QK_NOPE).astype(BF16)
    w_qr16 = wq[..., QK_NOPE:].reshape(N_A, Q_RANK, MLA_HEADS * QK_ROPE).astype(BF16)
    w_kup16 = jnp.transpose(w_k_up, (1, 2, 0)).astype(BF16)
    w_vup16 = jnp.transpose(w_v_up, (1, 0, 2)).astype(BF16)
    w_kv16 = jnp.pad(w_kv_down, ((0, 0), (0, KCAT - KV_RANK - QK_ROPE))).astype(BF16)
    w_mk16 = w_mem_k.astype(BF16)
    w_mv16 = w_mem_v.astype(BF16)
    g_kv_in2 = g_kv_in[None]
    g_lat2 = g_kv_latent[None]
    g_final2 = g_final[None]

    w_kupt16 = w_kup16.reshape(MLA_HEADS * QK_NOPE, KV_RANK)
    w_vall16 = w_v_up.reshape(KV_RANK, MLA_WIDTH).astype(BF16)

    def trunk(x, prev, mem_k, mem_v, pos, *, tag, nb, tq_a, tq, tq_b, heads, n_valid, attend):
        pools = []
        for l in range(N_A):
            x, pool = _a_layer(x, prev[l], g_norm[l][None], w_in_a16, w_grp16, pool_scale[l][None],
                               mem_k, mem_v, w_out16, layer=l, nb=nb, tq=tq_a, n_valid=n_valid, pos0=pos)
            pools.append(pool[:, 1:])
        cos, sin = _rope_tables(pos + jnp.arange(x.shape[1], dtype=jnp.int32))
        ckv, krope, *keys = _kv_latent(x, g_kv_in2, w_kv16, g_lat2, cos, sin, w_kupt16, w_vall16,
                                       nb=nb, tq=tq, tk=min(tq, ATT_TILE), heads=heads)
        for j in range(DEPTH - N_A):
            l = N_A + j
            q, zrest = _b_pre(x, g_norm[l][None], w_inb16, g_q_latent[j][None],
                              w_qn16, w_qr16, w_kup16, cos, sin, j=j, nb=nb, tq=tq_b, absorb=not heads,
                              tag=f"{tag}{l}")
            x = attend(x, q, zrest, *keys, layer=l, final=(l == DEPTH - 1))
        if heads:
            krope = jnp.transpose(krope, (0, 2, 1))
        return x, jnp.stack(pools, axis=0), ckv, krope

    mem_k_p, mem_v_p = _mem_project(mem_prompt, g_mem, w_mem_k, w_mem_v)
    prev_p = jnp.zeros((N_A, B, HALO, D_MODEL), F32)

    def attend_prompt(x, q, zrest, knt, krt, v, *, layer, final):
        return _flash_post(q, knt, krt, v, x, zrest, mem_k_p, mem_v_p, w_out16, g_final2,
                           layer=layer, tq=ATT_TILE, final=final)

    y_p, pool_p, ckv_p, krope_p = trunk(
        x_prompt, prev_p, mem_k_p, mem_v_p, 0, tag="p", nb=1, tq_a=512, tq=512, tq_b=512, heads=True, n_valid=512,
        attend=attend_prompt)

    xs = jnp.pad(x_sample, ((0, 0), (0, TP - ST), (0, 0)))
    prev_s = jnp.pad(state_pool, ((0, 0), (0, 0), (HALO - POOL_BUF, 0), (0, 0)))
    mem_k_s = cache_mem_k.reshape(DEPTH, SB, MEM_ROWS, MEM_HEAD_DIM)
    mem_v_s = cache_mem_v.reshape(DEPTH, SB, MEM_ROWS, MEM_HEAD_DIM)
    cache_krt = jnp.transpose(cache_krope, (0, 2, 1))

    def attend_sample(x, q, zrest, kcat, *, layer, final):
        qf = q.reshape(SB, MLA_HEADS * TP, KCAT)
        knew = jnp.pad(kcat, ((0, 0), (0, LANES - TP), (0, 0)))
        o = _paged_attention(page_table, qf, knew, cache_ckv, cache_krt, cp=32, tp=TP, n_new=ST)
        tok = _v_up(o.reshape(SB, MLA_HEADS, TP, KV_RANK), w_vup16)
        return _b_post(x, tok, zrest, mem_k_s, mem_v_s, w_out16, g_final2, layer=layer, nb=8, tq=TP,
                       final=final, tag="s")

    y_s, pool_s, ckv_s, krope_s = trunk(
        xs, prev_s, mem_k_s, mem_v_s, past, tag="s", nb=8, tq_a=TP, tq=TP, tq_b=TP, heads=False, n_valid=ST,
        attend=attend_sample)

    mem_shape = (DEPTH, B, MEM_TOKENS, MEM_HEADS, MEM_HEAD_DIM)
    return (y_p, y_s[:, :ST], pool_p, pool_s, ckv_p, krope_p, ckv_s[:, :ST], krope_s[:, :ST],
            mem_k_p.reshape(mem_shape), mem_v_p.reshape(mem_shape))
```

```python
import functools

import jax
import jax.numpy as jnp
from jax import lax
from jax.experimental import pallas as pl
from jax.experimental.pallas import tpu as pltpu

F32 = jnp.float32
BF16 = jnp.bfloat16

D_MODEL = 1024
DEPTH = 4
N_A = 2
POOL_WINDOWS = (2, 4, 8, 16)
POOL_GROUP = 256
POOL_BUF = 15
MLA_HEADS = 8
QK_NOPE = 128
QK_ROPE = 64
V_HEAD = 128
KV_RANK = 256
Q_RANK = 384
MLA_WIDTH = MLA_HEADS * V_HEAD
MLA_SCALE = (QK_NOPE + QK_ROPE) ** -0.5
ROPE_THETA = 10000.0
MEM_TOKENS = 256
MEM_HEADS = 4
MEM_HEAD_DIM = 128
MEM_WIDTH = MEM_HEADS * MEM_HEAD_DIM
MEM_SCALE = MEM_HEAD_DIM ** -0.5
POOL_WIDTH = D_MODEL
IN_A = 2 * POOL_WIDTH + 2 * MEM_WIDTH
Z_REST = MLA_WIDTH + 2 * MEM_WIDTH
OUT_W = POOL_WIDTH + MEM_WIDTH
EPS = 1e-6
NEG = -1e30
LOG2E = 1.4426950408889634

KCAT = 384
SUBLANES = 8
LANES = 128
HALO = 16
E_OFF = SUBLANES + HALO
VMEM_LIMIT = 56 * 1024 * 1024
MEM_ROWS = MEM_TOKENS * MEM_HEADS
ATT_TILE = 256
N_RAW = 4


def _cparams(n_axes):
    return pltpu.CompilerParams(dimension_semantics=("arbitrary",) * n_axes,
                                vmem_limit_bytes=VMEM_LIMIT)


def _rms(x, g):
    return x * lax.rsqrt(jnp.mean(x * x, axis=-1, keepdims=True) + EPS) * g


def _dot(a, b):
    return jnp.dot(a, b, preferred_element_type=F32)


def _dot_nt(a, b):
    return lax.dot_general(a, b, (((1,), (1,)), ((), ())), preferred_element_type=F32)


def _silu(x):
    return x * jax.nn.sigmoid(x)


def _rope_lanes(a, cos, sin):
    half = QK_ROPE // 2
    lane = lax.broadcasted_iota(jnp.int32, a.shape, 1)
    up = pltpu.roll(a, half, 1)
    dn = pltpu.roll(a, LANES - half, 1)
    rot = jnp.where((lane & (QK_ROPE - 1)) < half, -dn, up)
    return a * cos + rot * sin


def _rep(x, n):
    return x if n == 1 else jnp.concatenate([x] * n, axis=1)


def _tile_rows(x, n):
    return x if n == 1 else jnp.concatenate([x] * n, axis=0)


def _mem_attend(qm, mk_ref, mv_ref, nb, tq):
    rows, keys = nb * tq, nb * MEM_TOKENS
    if nb > 1:
        row_seq = lax.broadcasted_iota(jnp.int32, (rows, keys), 0) // tq
        col_seq = lax.broadcasted_iota(jnp.int32, (rows, keys), 1) // MEM_TOKENS
        keep = row_seq == col_seq
    outs = []
    for h in range(MEM_HEADS):
        c0, c1 = h * MEM_HEAD_DIM, (h + 1) * MEM_HEAD_DIM
        q = (qm[:, c0:c1] * (MEM_SCALE * LOG2E)).astype(BF16)
        k = mk_ref[0, :, pl.ds(h, MEM_TOKENS, stride=MEM_HEADS), :].reshape(keys, MEM_HEAD_DIM).astype(BF16)
        v = mv_ref[0, :, pl.ds(h, MEM_TOKENS, stride=MEM_HEADS), :].reshape(keys, MEM_HEAD_DIM).astype(BF16)
        s = _dot_nt(q, k)
        if nb > 1:
            s = jnp.where(keep, s, NEG)
        m = jnp.max(s, axis=-1, keepdims=True)
        p = jnp.exp2(s - m)
        l = jnp.sum(p, axis=-1, keepdims=True)
        outs.append(_dot(p.astype(BF16), v) * (1.0 / l))
    return outs


def _mem_proj_kernel(mem_ref, g_ref, wk_ref, wv_ref, mk_ref, mv_ref):
    x = mem_ref[0]
    xn = x * lax.rsqrt(jnp.mean(x * x, axis=-1, keepdims=True) + EPS)
    for l in range(DEPTH):
        h = xn * g_ref[l:l + 1, :]
        mk = _dot(h, wk_ref[l])
        mv = _dot(h, wv_ref[l])
        for hd in range(MEM_HEADS):
            c0, c1 = hd * MEM_HEAD_DIM, (hd + 1) * MEM_HEAD_DIM
            mk_ref[l, 0, pl.ds(hd, MEM_TOKENS, stride=MEM_HEADS), :] = mk[:, c0:c1]
            mv_ref[l, 0, pl.ds(hd, MEM_TOKENS, stride=MEM_HEADS), :] = mv[:, c0:c1]


def _mem_project(mem, g_mem, wk, wv):
    B, M, _ = mem.shape
    out = jax.ShapeDtypeStruct((DEPTH, B, MEM_ROWS, MEM_HEAD_DIM), F32)
    return pl.pallas_call(
        _mem_proj_kernel,
        grid=(B,),
        in_specs=[
            pl.BlockSpec((1, M, D_MODEL), lambda b: (b, 0, 0)),
            pl.BlockSpec((DEPTH, D_MODEL), lambda b: (0, 0)),
            pl.BlockSpec((DEPTH, D_MODEL, MEM_WIDTH), lambda b: (0, 0, 0)),
            pl.BlockSpec((DEPTH, D_MODEL, MEM_WIDTH), lambda b: (0, 0, 0)),
        ],
        out_specs=[
            pl.BlockSpec((DEPTH, 1, MEM_ROWS, MEM_HEAD_DIM), lambda b: (0, b, 0, 0)),
            pl.BlockSpec((DEPTH, 1, MEM_ROWS, MEM_HEAD_DIM), lambda b: (0, b, 0, 0)),
        ],
        out_shape=[out, out],
        compiler_params=_cparams(1),
        name="mem_project",
    )(mem, g_mem, wk, wv)


def _a_layer_kernel(x_ref, prev_ref, g_ref, win_ref, wgrp_ref, pscale_ref, mk_ref, mv_ref, wout_ref,
                    xo_ref, pool_ref, e_sc, sa_sc, sb_sc, mix_sc, *, nb, tq, n_t, n_valid, pos0):
    t = pl.program_id(1)
    rows = HALO + tq
    m = nb * tq

    @pl.when(t == 0)
    def _():
        zeros = jnp.zeros((nb, SUBLANES, D_MODEL), F32)
        e_sc[:, 0:SUBLANES, :] = zeros
        sa_sc[:, 0:SUBLANES, :] = zeros
        sb_sc[:, 0:SUBLANES, :] = zeros
        e_sc[:, SUBLANES:E_OFF, :] = prev_ref[...]

    x = x_ref[...].reshape(m, D_MODEL)
    h = _rms(x, g_ref[...])
    e_sc[:, E_OFF:E_OFF + tq, :] = _dot(h, win_ref[:, 0:POOL_WIDTH]).reshape(nb, tq, POOL_WIDTH)

    lo = SUBLANES
    sa_sc[:, lo:lo + rows, :] = e_sc[:, lo:lo + rows, :] + e_sc[:, lo - 1:lo - 1 + rows, :]
    sb_sc[:, lo:lo + rows, 256:1024] = (sa_sc[:, lo:lo + rows, 256:1024]
                                        + sa_sc[:, lo - 2:lo - 2 + rows, 256:1024])
    sa_sc[:, lo:lo + rows, 512:1024] = (sb_sc[:, lo:lo + rows, 512:1024]
                                        + sb_sc[:, lo - 4:lo - 4 + rows, 512:1024])
    s16 = sa_sc[:, E_OFF:E_OFF + tq, 768:1024] + sa_sc[:, E_OFF - 8:E_OFF - 8 + tq, 768:1024]
    sums = (sa_sc[:, E_OFF:E_OFF + tq, 0:256], sb_sc[:, E_OFF:E_OFF + tq, 256:512],
            sa_sc[:, E_OFF:E_OFF + tq, 512:768], s16)

    pos = pos0 + t * tq + lax.broadcasted_iota(jnp.int32, (1, tq, 1), 1)
    for g, w in enumerate(POOL_WINDOWS):
        c0, c1 = g * POOL_GROUP, (g + 1) * POOL_GROUP
        inv_cnt = 1.0 / jnp.minimum(pos + 1, w).astype(F32)
        pooled = (sums[g] * inv_cnt - e_sc[:, E_OFF:E_OFF + tq, c0:c1]).reshape(m, POOL_GROUP)
        mixed = _dot(pooled, wgrp_ref[g]) * pscale_ref[:, c0:c1]
        gate = _dot(h, win_ref[:, POOL_WIDTH + c0:POOL_WIDTH + c1])
        mix_sc[:, c0:c1] = mixed * _silu(gate)

    qm = _dot(h, win_ref[:, 2 * POOL_WIDTH:2 * POOL_WIDTH + MEM_WIDTH])
    gate_m = _dot(h, win_ref[:, 2 * POOL_WIDTH + MEM_WIDTH:IN_A])
    mem_o = _mem_attend(qm, mk_ref, mv_ref, nb, tq)
    for hd in range(MEM_HEADS):
        c0, c1 = hd * MEM_HEAD_DIM, (hd + 1) * MEM_HEAD_DIM
        mix_sc[:, POOL_WIDTH + c0:POOL_WIDTH + c1] = mem_o[hd] * _silu(gate_m[:, c0:c1])
    xo_ref[...] = (x + _dot(mix_sc[...], wout_ref[...])).reshape(nb, tq, D_MODEL)

    @pl.when(t == n_t - 1)
    def _():
        pool_ref[...] = e_sc[:, E_OFF + n_valid - HALO:E_OFF + n_valid, :]

    if n_t > 1:
        @pl.when(t < n_t - 1)
        def _():
            e_sc[:, SUBLANES:E_OFF, :] = e_sc[:, SUBLANES + tq:E_OFF + tq, :]


def _a_layer(x, prev, g, w_in, w_grp, pscale, mem_k, mem_v, w_out, *, layer, nb, tq, n_valid, pos0):
    B, T, _ = x.shape
    n_t = T // tq
    rows = E_OFF + tq
    kern = functools.partial(_a_layer_kernel, nb=nb, tq=tq, n_t=n_t, n_valid=n_valid, pos0=pos0)
    return pl.pallas_call(
        kern,
        grid=(B // nb, n_t),
        in_specs=[
            pl.BlockSpec((nb, tq, D_MODEL), lambda b, t: (b, t, 0)),
            pl.BlockSpec((nb, HALO, D_MODEL), lambda b, t: (b, 0, 0)),
            pl.BlockSpec((1, D_MODEL), lambda b, t: (0, 0)),
            pl.BlockSpec((None, D_MODEL, IN_A), lambda b, t: (layer, 0, 0), pipeline_mode=pl.Buffered(1)),
            pl.BlockSpec((None, 4, POOL_GROUP, POOL_GROUP), lambda b, t: (layer, 0, 0, 0),
                         pipeline_mode=pl.Buffered(1)),
            pl.BlockSpec((1, D_MODEL), lambda b, t: (0, 0)),
            pl.BlockSpec((1, nb, MEM_ROWS, MEM_HEAD_DIM), lambda b, t: (layer, b, 0, 0)),
            pl.BlockSpec((1, nb, MEM_ROWS, MEM_HEAD_DIM), lambda b, t: (layer, b, 0, 0)),
            pl.BlockSpec((None, OUT_W, D_MODEL), lambda b, t: (layer, 0, 0), pipeline_mode=pl.Buffered(1)),
        ],
        out_specs=[
            pl.BlockSpec((nb, tq, D_MODEL), lambda b, t: (b, t, 0)),
            pl.BlockSpec((nb, HALO, D_MODEL), lambda b, t: (b, 0, 0)),
        ],
        out_shape=[jax.ShapeDtypeStruct((B, T, D_MODEL), F32),
                   jax.ShapeDtypeStruct((B, HALO, D_MODEL), F32)],
        scratch_shapes=[pltpu.VMEM((nb, rows, D_MODEL), F32)] * 3 + [pltpu.VMEM((nb * tq, OUT_W), F32)],
        compiler_params=_cparams(2),
        name=f"a_layer{layer}_{B}",
    )(x, prev, g, w_in, w_grp, pscale, mem_k, mem_v, w_out)


def _kv_latent_kernel(x_ref, gin_ref, wkv_ref, glat_ref, cos_ref, sin_ref, wkupt_ref, wvup_ref,
                      ckv_ref, kr_ref, *out_refs, heads):
    nb, tq, _ = x_ref.shape
    m = nb * tq
    h = _rms(x_ref[...].reshape(m, D_MODEL), gin_ref[...]).astype(BF16)
    kv = _dot(h, wkv_ref[...])
    ckv = _rms(kv[:, 0:KV_RANK], glat_ref[...])
    kr = _rope_lanes(kv[:, KV_RANK:KCAT], _tile_rows(cos_ref[...], nb), _tile_rows(sin_ref[...], nb))
    ckv_ref[...] = ckv.reshape(nb, tq, KV_RANK)
    if not heads:
        kcat_ref, = out_refs
        kr_ref[...] = kr[:, 0:QK_ROPE].reshape(nb, tq, QK_ROPE)
        kcat_ref[:, :, 0:KV_RANK] = ckv.reshape(nb, tq, KV_RANK).astype(BF16)
        kcat_ref[:, :, KV_RANK:KCAT] = kr.reshape(nb, tq, LANES).astype(BF16)
    else:
        knt_ref, krt_ref, v_ref = out_refs
        tk = knt_ref.shape[-1]
        v_ref[0] = _dot(ckv.astype(BF16), wvup_ref[...]).astype(BF16)
        for sub in range(tq // tk):
            r0 = sub * tk
            ckv_t = ckv[r0:r0 + tk, :].T.astype(BF16)
            knt_ref[0, sub] = _dot(wkupt_ref[...], ckv_t).astype(BF16)
            kr_t = kr[r0:r0 + tk, :].T
            kr_ref[0, :, r0:r0 + tk] = kr_t[0:QK_ROPE, :]
            krt_ref[0, sub] = kr_t.astype(BF16)


def _kv_latent(x, g_in, w_kv, g_lat, cos, sin, w_kupt, w_vup, *, nb, tq, tk, heads):
    B, T, _ = x.shape
    assert nb == 1 or not heads
    n_sub = tq // tk
    out_specs = [
        pl.BlockSpec((nb, tq, KV_RANK), lambda b, t: (b, t, 0)),
        pl.BlockSpec((1, QK_ROPE, tq), lambda b, t: (b, 0, t)) if heads
        else pl.BlockSpec((nb, tq, QK_ROPE), lambda b, t: (b, t, 0)),
    ]
    out_shape = [jax.ShapeDtypeStruct((B, T, KV_RANK), F32),
                 jax.ShapeDtypeStruct((B, QK_ROPE, T) if heads else (B, T, QK_ROPE), F32)]
    if heads:
        out_specs += [pl.BlockSpec((1, n_sub, MLA_HEADS * QK_NOPE, tk), lambda b, t: (b, t, 0, 0)),
                      pl.BlockSpec((1, n_sub, LANES, tk), lambda b, t: (b, t, 0, 0)),
                      pl.BlockSpec((1, tq, MLA_WIDTH), lambda b, t: (b, t, 0))]
        out_shape += [jax.ShapeDtypeStruct((B, T // tk, MLA_HEADS * QK_NOPE, tk), BF16),
                      jax.ShapeDtypeStruct((B, T // tk, LANES, tk), BF16),
                      jax.ShapeDtypeStruct((B, T, MLA_WIDTH), BF16)]
    else:
        out_specs.append(pl.BlockSpec((nb, tq, KCAT), lambda b, t: (b, t, 0)))
        out_shape.append(jax.ShapeDtypeStruct((B, T, KCAT), BF16))
    return pl.pallas_call(
        functools.partial(_kv_latent_kernel, heads=heads),
        grid=(B // nb, T // tq),
        in_specs=[
            pl.BlockSpec((nb, tq, D_MODEL), lambda b, t: (b, t, 0)),
            pl.BlockSpec((1, D_MODEL), lambda b, t: (0, 0)),
            pl.BlockSpec((D_MODEL, KCAT), lambda b, t: (0, 0)),
            pl.BlockSpec((1, KV_RANK), lambda b, t: (0, 0)),
            pl.BlockSpec((tq, LANES), lambda b, t: (t, 0)),
            pl.BlockSpec((tq, LANES), lambda b, t: (t, 0)),
            pl.BlockSpec((MLA_HEADS * QK_NOPE, KV_RANK), lambda b, t: (0, 0)),
            pl.BlockSpec((KV_RANK, MLA_WIDTH), lambda b, t: (0, 0)),
        ],
        out_specs=out_specs,
        out_shape=out_shape,
        compiler_params=_cparams(2),
        name=f"kv_latent_{B}",
    )(x, g_in, w_kv, g_lat, cos, sin, w_kupt, w_vup)


def _b_pre_kernel(x_ref, g_ref, winb_ref, gq_ref, wqn_ref, wqr_ref, wkup_ref, cos_ref, sin_ref,
                  q_ref, zrest_ref, *, absorb):
    nb, tq, _ = x_ref.shape
    m = nb * tq
    n0 = KV_RANK if absorb else QK_NOPE
    h = _rms(x_ref[...].reshape(m, D_MODEL), g_ref[...])
    zrest_ref[...] = _dot(h, winb_ref[:, Q_RANK:]).reshape(nb, tq, Z_REST)
    cn = _rms(_dot(h, winb_ref[:, 0:Q_RANK]), gq_ref[...]).astype(BF16)
    qn = _dot(cn, wqn_ref[...])
    qr = _dot(cn, wqr_ref[...])
    for hd in range(MLA_HEADS):
        ql = qn[:, hd * QK_NOPE:(hd + 1) * QK_NOPE]
        if absorb:
            ql = _dot(ql.astype(BF16), wkup_ref[hd])
        q_ref[:, hd, :, 0:n0] = (ql * (MLA_SCALE * LOG2E)).reshape(nb, tq, n0).astype(BF16)
    cos = _tile_rows(cos_ref[...], nb)
    sin = _tile_rows(sin_ref[...], nb)
    lane = lax.broadcasted_iota(jnp.int32, cos.shape, 1)
    for j in range(MLA_HEADS // 2):
        rr = _rope_lanes(qr[:, j * LANES:(j + 1) * LANES], cos, sin) * (MLA_SCALE * LOG2E)
        even = jnp.where(lane < QK_ROPE, rr, 0.0)
        odd = jnp.where(lane < QK_ROPE, pltpu.roll(rr, 64, 1), 0.0)
        q_ref[:, 2 * j, :, n0:n0 + LANES] = even.reshape(nb, tq, LANES).astype(BF16)
        q_ref[:, 2 * j + 1, :, n0:n0 + LANES] = odd.reshape(nb, tq, LANES).astype(BF16)


def _b_pre(x, g, w_inb, g_q, w_qn, w_qr, w_kup, cos, sin, *, j, nb, tq, absorb, tag):
    B, T, _ = x.shape
    const2 = lambda b, t: (0, 0)
    qw = (KV_RANK if absorb else QK_NOPE) + LANES
    return pl.pallas_call(
        functools.partial(_b_pre_kernel, absorb=absorb),
        grid=(B // nb, T // tq),
        in_specs=[
            pl.BlockSpec((nb, tq, D_MODEL), lambda b, t: (b, t, 0)),
            pl.BlockSpec((1, D_MODEL), const2),
            pl.BlockSpec((None, D_MODEL, Q_RANK + Z_REST), lambda b, t: (j, 0, 0), pipeline_mode=pl.Buffered(1)),
            pl.BlockSpec((1, Q_RANK), const2),
            pl.BlockSpec((None, Q_RANK, MLA_HEADS * QK_NOPE), lambda b, t: (j, 0, 0)),
            pl.BlockSpec((None, Q_RANK, MLA_HEADS * QK_ROPE), lambda b, t: (j, 0, 0)),
            pl.BlockSpec((MLA_HEADS, QK_NOPE, KV_RANK), lambda b, t: (0, 0, 0)),
            pl.BlockSpec((tq, LANES), lambda b, t: (t, 0)),
            pl.BlockSpec((tq, LANES), lambda b, t: (t, 0)),
        ],
        out_specs=[
            pl.BlockSpec((nb, MLA_HEADS, tq, qw), lambda b, t: (b, 0, t, 0)),
            pl.BlockSpec((nb, tq, Z_REST), lambda b, t: (b, t, 0)),
        ],
        out_shape=[jax.ShapeDtypeStruct((B, MLA_HEADS, T, qw), BF16),
                   jax.ShapeDtypeStruct((B, T, Z_REST), F32)],
        compiler_params=_cparams(2),
        name=f"b_pre_{tag}",
    )(x, g, w_inb, g_q, w_qn, w_qr, w_kup, cos, sin)


def _flash_kernel(q_ref, knt_ref, krt_ref, v_ref, x_ref, z_ref, mk_ref, mv_ref, wout_ref, gf_ref, xo_ref,
                  m_sc, l_sc, acc_sc, tok_sc, mix_sc, *, tq, tk, final):
    i = pl.program_id(1)
    m_sc[...] = jnp.full(m_sc.shape, NEG, F32)
    l_sc[...] = jnp.zeros(l_sc.shape, F32)
    acc_sc[...] = jnp.zeros(acc_sc.shape, F32)

    def block(j, masked):
        krt = krt_ref[0, j]
        rows = pl.ds(pl.multiple_of(j * tk, tk), tk)
        if masked:
            keep = (lax.broadcasted_iota(jnp.int32, (tq, tk), 1)
                    <= lax.broadcasted_iota(jnp.int32, (tq, tk), 0))

        def scores(hd):
            kt = jnp.concatenate([knt_ref[0, j, hd * QK_NOPE:(hd + 1) * QK_NOPE, :], krt], axis=0)
            return _dot(q_ref[0, hd], kt)

        s_next = scores(0)
        for hd in range(MLA_HEADS):
            s = s_next
            if hd + 1 < MLA_HEADS:
                s_next = scores(hd + 1)
            if masked:
                s = jnp.where(keep, s, NEG)
            m_prev = m_sc[hd]
            m_new = jnp.maximum(m_prev, jnp.max(s, axis=-1, keepdims=True))
            alpha = jnp.exp2(m_prev - m_new)
            p = jnp.exp2(s - _rep(m_new, tk // LANES))
            p_lanes = p[:, 0:LANES]
            for c in range(1, tk // LANES):
                p_lanes = p_lanes + p[:, c * LANES:(c + 1) * LANES]
            l_sc[hd] = alpha * l_sc[hd] + p_lanes
            v = v_ref[0, rows, hd * V_HEAD:(hd + 1) * V_HEAD]
            acc_sc[hd] = alpha * acc_sc[hd] + _dot(p.astype(BF16), v)
            m_sc[hd] = m_new

    def body(jj, carry):
        block(2 * jj, False)
        block(2 * jj + 1, False)
        return carry

    lax.fori_loop(0, i // 2, body, 0)

    @pl.when(i % 2 == 1)
    def _():
        block(i - 1, False)

    block(i, True)

    for hd in range(MLA_HEADS):
        l = jnp.sum(l_sc[hd], axis=-1, keepdims=True)
        tok_sc[0, :, hd * V_HEAD:(hd + 1) * V_HEAD] = acc_sc[hd] * (1.0 / l)

    _mix_out(x_ref, tok_sc, z_ref, mk_ref, mv_ref, wout_ref, gf_ref, xo_ref, mix_sc, final)


def _flash_post(q, knt, krt, v, x, zrest, mem_k, mem_v, w_out, g_final, *, layer, tq, final):
    B, H, T, qw = q.shape
    tk = knt.shape[-1]
    assert tq == tk and qw == QK_NOPE + LANES
    kern = functools.partial(_flash_kernel, tq=tq, tk=tk, final=final)
    return pl.pallas_call(
        kern,
        grid=(B, T // tq),
        in_specs=[
            pl.BlockSpec((1, H, tq, qw), lambda b, i: (b, 0, i, 0)),
            pl.BlockSpec((1, T // tk, H * QK_NOPE, tk), lambda b, i: (b, 0, 0, 0)),
            pl.BlockSpec((1, T // tk, LANES, tk), lambda b, i: (b, 0, 0, 0)),
            pl.BlockSpec((1, T, MLA_WIDTH), lambda b, i: (b, 0, 0)),
            pl.BlockSpec((1, tq, D_MODEL), lambda b, i: (b, i, 0)),
            pl.BlockSpec((1, tq, Z_REST), lambda b, i: (b, i, 0)),
            pl.BlockSpec((1, 1, MEM_ROWS, MEM_HEAD_DIM), lambda b, i: (layer, b, 0, 0)),
            pl.BlockSpec((1, 1, MEM_ROWS, MEM_HEAD_DIM), lambda b, i: (layer, b, 0, 0)),
            pl.BlockSpec((None, OUT_W, D_MODEL), lambda b, i: (layer, 0, 0), pipeline_mode=pl.Buffered(1)),
            pl.BlockSpec((1, D_MODEL), lambda b, i: (0, 0)),
        ],
        out_specs=pl.BlockSpec((1, tq, D_MODEL), lambda b, i: (b, i, 0)),
        out_shape=jax.ShapeDtypeStruct((B, T, D_MODEL), F32),
        scratch_shapes=[pltpu.VMEM((H, tq, LANES), F32), pltpu.VMEM((H, tq, LANES), F32),
                        pltpu.VMEM((H, tq, V_HEAD), F32), pltpu.VMEM((1, tq, MLA_WIDTH), F32),
                        pltpu.VMEM((tq, OUT_W), F32)],
        compiler_params=_cparams(2),
        name=f"flash_post{layer}",
    )(q, knt, krt, v, x, zrest, mem_k, mem_v, w_out, g_final)


def _paged_kernel(pt_ref, q_ref, knew_ref, ckv_hbm, krt_hbm, o_ref,
                  ckv_buf, krt_buf, kbf_sc, s_sc, sem, m_sc, l_sc, acc_sc,
                  *, n_b, n_ch, cp, page, tp, n_new):
    total = n_b * n_ch
    assert total >= 2

    def copies(bc):
        slot = bc % N_RAW
        b = bc // n_ch
        c = bc % n_ch
        out = []
        for p in range(cp):
            pg = pt_ref[b, c * cp + p]
            out.append(pltpu.make_async_copy(ckv_hbm.at[pg], ckv_buf.at[slot, pl.ds(p * page, page)],
                                             sem.at[0, slot]))
            out.append(pltpu.make_async_copy(krt_hbm.at[pg], krt_buf.at[slot, p], sem.at[1, slot]))
        return out

    def score(bc, w):
        slot = bc % N_RAW
        b = bc // n_ch
        for cpy in copies(bc):
            cpy.wait()
        q_lat = q_ref[b, :, 0:KV_RANK]
        q_rope = q_ref[b, :, KV_RANK:KV_RANK + QK_ROPE]
        kc = ckv_buf[slot].astype(BF16)
        krt = jnp.concatenate([krt_buf[slot, p] for p in range(cp)], axis=1).astype(BF16)
        kbf_sc[w] = kc
        s_sc[w] = _dot_nt(q_lat, kc) + _dot(q_rope, krt)

    def update(s, v_bf16, first):
        m_prev = jnp.where(first, NEG, m_sc[...])
        l_prev = jnp.where(first, 0.0, l_sc[...])
        acc_prev = jnp.where(first, 0.0, acc_sc[...])
        m_new = jnp.maximum(m_prev, jnp.max(s, axis=-1, keepdims=True))
        alpha = jnp.exp2(m_prev - m_new)
        p = jnp.exp2(s - _rep(m_new, s.shape[1] // LANES))
        l_sc[...] = alpha * l_prev + jnp.sum(p, axis=-1, keepdims=True)
        acc_sc[...] = _rep(alpha, KV_RANK // LANES) * acc_prev + _dot(p.astype(BF16), v_bf16)
        m_sc[...] = m_new

    def attend(bc, w):
        b = bc // n_ch
        c = bc % n_ch
        update(s_sc[w], kbf_sc[w], c == 0)

        def finish():
            kn = knew_ref[b]
            sn = _dot_nt(q_ref[b], kn)
            tpos = lax.broadcasted_iota(jnp.int32, sn.shape, 0) & (tp - 1)
            col = lax.broadcasted_iota(jnp.int32, sn.shape, 1)
            sn = jnp.where((col <= tpos) & (col < n_new), sn, NEG)
            update(sn, kn[:, 0:KV_RANK], False)
            o_ref[b] = acc_sc[...] * _rep(1.0 / l_sc[...], KV_RANK // LANES)

        last = c == n_ch - 1
        if isinstance(last, bool):
            if last:
                finish()
        else:
            pl.when(last)(finish)

    m_sc[...] = jnp.full(m_sc.shape, NEG, F32)
    l_sc[...] = jnp.zeros(l_sc.shape, F32)
    acc_sc[...] = jnp.zeros(acc_sc.shape, F32)
    for bc in range(min(N_RAW, total)):
        for cpy in copies(bc):
            cpy.start()
    score(0, 0)

    def step(i, w):
        def prefetch():
            for cpy in copies(i + N_RAW):
                cpy.start()

        more = i + N_RAW < total
        if isinstance(more, bool):
            if more:
                prefetch()
        else:
            pl.when(more)(prefetch)
        if isinstance(i, int) and i + 1 >= total:
            attend(i, w)
            return
        score(i + 1, 1 - w)
        attend(i, w)

    def body(k, carry):
        step(2 * k, 0)
        step(2 * k + 1, 1)
        return carry

    n_pairs = (total - 2) // 2
    lax.fori_loop(0, n_pairs, body, 0)
    for i in range(2 * n_pairs, total):
        step(i, i % 2)


def _paged_attention(page_table, q, knew, cache_ckv, cache_krt, *, cp, tp, n_new):
    n_b, n_pages = page_table.shape
    page = cache_ckv.shape[1]
    rows = q.shape[1]
    n_ch = n_pages // cp
    kern = functools.partial(_paged_kernel, n_b=n_b, n_ch=n_ch, cp=cp, page=page, tp=tp, n_new=n_new)
    grid_spec = pltpu.PrefetchScalarGridSpec(
        num_scalar_prefetch=1,
        grid=(1,),
        in_specs=[
            pl.BlockSpec(q.shape, lambda i, pt: (0, 0, 0)),
            pl.BlockSpec(knew.shape, lambda i, pt: (0, 0, 0)),
            pl.BlockSpec(memory_space=pl.ANY),
            pl.BlockSpec(memory_space=pl.ANY),
        ],
        out_specs=pl.BlockSpec((n_b, rows, KV_RANK), lambda i, pt: (0, 0, 0)),
        scratch_shapes=[
            pltpu.VMEM((N_RAW, cp * page, KV_RANK), F32),
            pltpu.VMEM((N_RAW, cp, QK_ROPE, page), F32),
            pltpu.VMEM((2, cp * page, KV_RANK), BF16),
            pltpu.VMEM((2, rows, cp * page), F32),
            pltpu.SemaphoreType.DMA((2, N_RAW)),
            pltpu.VMEM((rows, LANES), F32),
            pltpu.VMEM((rows, LANES), F32),
            pltpu.VMEM((rows, KV_RANK), F32),
        ],
    )
    return pl.pallas_call(
        kern,
        grid_spec=grid_spec,
        out_shape=jax.ShapeDtypeStruct((n_b, rows, KV_RANK), F32),
        compiler_params=_cparams(1),
        name="paged_attention",
    )(page_table, q, knew, cache_ckv, cache_krt)


def _v_up_kernel(o_ref, wv_ref, tok_ref):
    n_b, _, tp, _ = o_ref.shape
    for hd in range(MLA_HEADS):
        oh = o_ref[:, hd].reshape(n_b * tp, KV_RANK).astype(BF16)
        tok_ref[:, :, hd * V_HEAD:(hd + 1) * V_HEAD] = _dot(oh, wv_ref[hd]).reshape(n_b, tp, V_HEAD)


def _v_up(o, w_vup):
    n_b, H, tp, _ = o.shape
    return pl.pallas_call(
        _v_up_kernel,
        out_shape=jax.ShapeDtypeStruct((n_b, tp, MLA_WIDTH), F32),
        compiler_params=pltpu.CompilerParams(vmem_limit_bytes=VMEM_LIMIT),
        name="v_up_sample",
    )(o, w_vup)


def _b_post_kernel(x_ref, tok_ref, z_ref, mk_ref, mv_ref, wout_ref, gf_ref, xo_ref, mix_sc, *, final):
    _mix_out(x_ref, tok_ref, z_ref, mk_ref, mv_ref, wout_ref, gf_ref, xo_ref, mix_sc, final)


def _mix_out(x_ref, tok_ref, z_ref, mk_ref, mv_ref, wout_ref, gf_ref, xo_ref, mix_sc, final):
    nb, tq, _ = x_ref.shape
    m = nb * tq
    for g in range(4):
        c0, c1 = g * 256, (g + 1) * 256
        mix_sc[:, c0:c1] = (tok_ref[:, :, c0:c1] * _silu(z_ref[:, :, c0:c1])).reshape(m, 256)
    g0 = MLA_WIDTH + MEM_WIDTH
    mem_o = _mem_attend(z_ref[:, :, MLA_WIDTH:g0].reshape(m, MEM_WIDTH), mk_ref, mv_ref, nb, tq)
    for hd in range(MEM_HEADS):
        c0, c1 = hd * MEM_HEAD_DIM, (hd + 1) * MEM_HEAD_DIM
        gate = z_ref[:, :, g0 + c0:g0 + c1].reshape(m, MEM_HEAD_DIM)
        mix_sc[:, POOL_WIDTH + c0:POOL_WIDTH + c1] = mem_o[hd] * _silu(gate)
    acc = x_ref[...].reshape(m, D_MODEL) + _dot(mix_sc[...], wout_ref[...])
    if final:
        acc = _rms(acc, gf_ref[...])
    xo_ref[...] = acc.reshape(nb, tq, D_MODEL)


def _b_post(x, tok, zrest, mem_k, mem_v, w_out, g_final, *, layer, nb, tq, final, tag):
    B, T, _ = x.shape
    kern = functools.partial(_b_post_kernel, final=final)
    return pl.pallas_call(
        kern,
        grid=(B // nb, T // tq),
        in_specs=[
            pl.BlockSpec((nb, tq, D_MODEL), lambda b, t: (b, t, 0)),
            pl.BlockSpec((nb, tq, MLA_WIDTH), lambda b, t: (b, t, 0)),
            pl.BlockSpec((nb, tq, Z_REST), lambda b, t: (b, t, 0)),
            pl.BlockSpec((1, nb, MEM_ROWS, MEM_HEAD_DIM), lambda b, t: (layer, b, 0, 0)),
            pl.BlockSpec((1, nb, MEM_ROWS, MEM_HEAD_DIM), lambda b, t: (layer, b, 0, 0)),
            pl.BlockSpec((None, OUT_W, D_MODEL), lambda b, t: (layer, 0, 0), pipeline_mode=pl.Buffered(1)),
            pl.BlockSpec((1, D_MODEL), lambda b, t: (0, 0)),
        ],
        out_specs=pl.BlockSpec((nb, tq, D_MODEL), lambda b, t: (b, t, 0)),
        out_shape=jax.ShapeDtypeStruct((B, T, D_MODEL), F32),
        scratch_shapes=[pltpu.VMEM((nb * tq, OUT_W), F32)],
        compiler_params=_cparams(2),
        name=f"b_post{layer}_{tag}",
    )(x, tok, zrest, mem_k, mem_v, w_out, g_final)


def _rope_tables(pos):
    half = QK_ROPE // 2
    inv = ROPE_THETA ** (-jnp.arange(half, dtype=F32) / half)
    ang = pos.astype(F32)[:, None] * inv[None, :]
    return jnp.tile(jnp.cos(ang), (1, LANES // half)), jnp.tile(jnp.sin(ang), (1, LANES // half))


def kernel(x_prompt, x_sample, state_pool, cache_ckv, cache_krope, cache_mem_k, cache_mem_v, page_table, mem_prompt, g_norm, w_in_a, w_pool_grp, pool_scale, w_in_b, g_q_latent, w_q_up, g_kv_in, w_kv_down, g_kv_latent, w_k_up, w_v_up, g_mem, w_mem_k, w_mem_v, w_out, g_final):
    B, T, _ = x_prompt.shape
    SB, ST, _ = x_sample.shape
    TP = SUBLANES
    past = page_table.shape[1] * cache_ckv.shape[1]

    w_in_a16, w_grp16, w_out16, w_inb16 = w_in_a, w_pool_grp, w_out, w_in_b
    wq = w_q_up.reshape(N_A, Q_RANK, MLA_HEADS, QK_NOPE + QK_ROPE)
    w_qn16 = wq[..., :QK_NOPE].reshape(N_A, Q_RANK, MLA_HEADS * QK_NOPE).astype(BF16)
    w_qr16 = wq[..., QK_NOPE:].reshape(N_A, Q_RANK, MLA_HEADS * QK_ROPE).astype(BF16)
    w_kup16 = jnp.transpose(w_k_up, (1, 2, 0)).astype(BF16)
    w_vup16 = jnp.transpose(w_v_up, (1, 0, 2)).astype(BF16)
    w_kv16 = jnp.pad(w_kv_down, ((0, 0), (0, KCAT - KV_RANK - QK_ROPE))).astype(BF16)
    w_mk16 = w_mem_k.astype(BF16)
    w_mv16 = w_mem_v.astype(BF16)
    g_kv_in2 = g_kv_in[None]
    g_lat2 = g_kv_latent[None]
    g_final2 = g_final[None]

    w_kupt16 = w_kup16.reshape(MLA_HEADS * QK_NOPE, KV_RANK)
    w_vall16 = w_v_up.reshape(KV_RANK, MLA_WIDTH).astype(BF16)

    def trunk(x, prev, mem_k, mem_v, pos, *, tag, nb, tq_a, tq, tq_b, heads, n_valid, attend):
        pools = []
        for l in range(N_A):
            x, pool = _a_layer(x, prev[l], g_norm[l][None], w_in_a16, w_grp16, pool_scale[l][None],
                               mem_k, mem_v, w_out16, layer=l, nb=nb, tq=tq_a, n_valid=n_valid, pos0=pos)
            pools.append(pool[:, 1:])
        cos, sin = _rope_tables(pos + jnp.arange(x.shape[1], dtype=jnp.int32))
        ckv, krope, *keys = _kv_latent(x, g_kv_in2, w_kv16, g_lat2, cos, sin, w_kupt16, w_vall16,
                                       nb=nb, tq=tq, tk=min(tq, ATT_TILE), heads=heads)
        for j in range(DEPTH - N_A):
            l = N_A + j
            q, zrest = _b_pre(x, g_norm[l][None], w_inb16, g_q_latent[j][None],
                              w_qn16, w_qr16, w_kup16, cos, sin, j=j, nb=nb, tq=tq_b, absorb=not heads,
                              tag=f"{tag}{l}")
            x = attend(x, q, zrest, *keys, layer=l, final=(l == DEPTH - 1))
        if heads:
            krope = jnp.transpose(krope, (0, 2, 1))
        return x, jnp.stack(pools, axis=0), ckv, krope

    mem_k_p, mem_v_p = _mem_project(mem_prompt, g_mem, w_mem_k, w_mem_v)
    prev_p = jnp.zeros((N_A, B, HALO, D_MODEL), F32)

    def attend_prompt(x, q, zrest, knt, krt, v, *, layer, final):
        return _flash_post(q, knt, krt, v, x, zrest, mem_k_p, mem_v_p, w_out16, g_final2,
                           layer=layer, tq=ATT_TILE, final=final)

    y_p, pool_p, ckv_p, krope_p = trunk(
        x_prompt, prev_p, mem_k_p, mem_v_p, 0, tag="p", nb=1, tq_a=512, tq=1024, tq_b=1024, heads=True, n_valid=512,
        attend=attend_prompt)

    xs = jnp.pad(x_sample, ((0, 0), (0, TP - ST), (0, 0)))
    prev_s = jnp.pad(state_pool, ((0, 0), (0, 0), (HALO - POOL_BUF, 0), (0, 0)))
    mem_k_s = cache_mem_k.reshape(DEPTH, SB, MEM_ROWS, MEM_HEAD_DIM)
    mem_v_s = cache_mem_v.reshape(DEPTH, SB, MEM_ROWS, MEM_HEAD_DIM)
    cache_krt = jnp.transpose(cache_krope, (0, 2, 1))

    def attend_sample(x, q, zrest, kcat, *, layer, final):
        qf = q.reshape(SB, MLA_HEADS * TP, KCAT)
        knew = jnp.pad(kcat, ((0, 0), (0, LANES - TP), (0, 0)))
        o = _paged_attention(page_table, qf, knew, cache_ckv, cache_krt, cp=32, tp=TP, n_new=ST)
        tok = _v_up(o.reshape(SB, MLA_HEADS, TP, KV_RANK), w_vup16)
        return _b_post(x, tok, zrest, mem_k_s, mem_v_s, w_out16, g_final2, layer=layer, nb=8, tq=TP,
                       final=final, tag="s")

    y_s, pool_s, ckv_s, krope_s = trunk(
        xs, prev_s, mem_k_s, mem_v_s, past, tag="s", nb=8, tq_a=TP, tq=TP, tq_b=TP, heads=False, n_valid=ST,
        attend=attend_sample)

    mem_shape = (DEPTH, B, MEM_TOKENS, MEM_HEADS, MEM_HEAD_DIM)
    return (y_p, y_s[:, :ST], pool_p, pool_s, ckv_p, krope_p, ckv_s[:, :ST], krope_s[:, :ST],
            mem_k_p.reshape(mem_shape), mem_v_p.reshape(mem_shape))
```
